```python
import jax, jax.numpy as jnp
from jax import lax
import numpy as np

D_MODEL = 1024
BATCH = 2
SEQ = 8192
DEPTH = 1

DN_HEADS = 8
DN_HEAD_DIM = 128
DN_WIDTH = DN_HEADS * DN_HEAD_DIM
DN_CONV = 4
DN_CHUNK = 64
MOBA_HEADS = 8
MOBA_HEAD_DIM = 128
MOBA_WIDTH = MOBA_HEADS * MOBA_HEAD_DIM
MOBA_BLOCK = 256
MOBA_TOPK = 3
MOBA_Q_CHUNK = 32
ROPE_THETA = 10000.0
D_FF = 2816
FFN_CONV = 3
NORM_EPS = 1e-6
IN_SPLITS = (DN_WIDTH, DN_WIDTH, DN_WIDTH, DN_WIDTH, DN_HEADS, DN_HEADS,
             MOBA_WIDTH, MOBA_WIDTH, MOBA_WIDTH, D_MODEL, D_MODEL)
IN_WIDTH = sum(IN_SPLITS)

kernel_name = "hybrid_gated_deltanet_moba_convffn"


def rms_norm(x, w):
    xf = x.astype(jnp.float32)
    y = xf * lax.rsqrt(jnp.mean(xf * xf, axis=-1, keepdims=True) + NORM_EPS)
    return (y * w.astype(jnp.float32)).astype(x.dtype)


def l2_normalize(x):
    xf = x.astype(jnp.float32)
    return (xf * lax.rsqrt(jnp.sum(xf * xf, axis=-1, keepdims=True) + NORM_EPS)).astype(x.dtype)


def causal_depthwise_conv(x, w, b=None):
    k_width = w.shape[0]
    s = x.shape[1]
    xp = jnp.pad(x, ((0, 0), (k_width - 1, 0), (0, 0)))
    y = xp[:, k_width - 1:k_width - 1 + s] * w[k_width - 1]
    for j in range(k_width - 1):
        y = y + xp[:, j:j + s] * w[j]
    if b is not None:
        y = y + b
    return y


def apply_rope(x, positions):
    half = x.shape[-1] // 2
    inv_freq = jnp.power(ROPE_THETA, -jnp.arange(half, dtype=jnp.float32) / half)
    ang = positions.astype(jnp.float32)[:, None] * inv_freq[None, :]
    cos, sin = jnp.cos(ang), jnp.sin(ang)
    xf = x.astype(jnp.float32)
    x1, x2 = xf[..., :half], xf[..., half:]
    return jnp.concatenate([x1 * cos - x2 * sin, x2 * cos + x1 * sin], axis=-1).astype(x.dtype)


def gated_delta_rule(q, k, v, beta, g):
    out_dtype = v.dtype
    b_sz, n_h, s_len, d_k = q.shape
    d_v = v.shape[-1]
    n_c = s_len // DN_CHUNK
    f32 = jnp.float32

    def chunks(t):
        return t.astype(f32).reshape(b_sz, n_h, n_c, DN_CHUNK, *t.shape[3:])

    q, k, v, beta, g = chunks(q), chunks(k), chunks(v), chunks(beta), chunks(g)
    g = jnp.cumsum(g, axis=-1)
    tri_incl = jnp.tril(jnp.ones((DN_CHUNK, DN_CHUNK), dtype=bool))
    tri_strict = jnp.tril(jnp.ones((DN_CHUNK, DN_CHUNK), dtype=bool), -1)
    decay = jnp.exp(jnp.where(tri_incl, g[..., :, None] - g[..., None, :], -jnp.inf))
    k_beta = k * beta[..., None]
    v_beta = v * beta[..., None]
    strict = jnp.einsum('bhnid,bhnjd->bhnij', k_beta, k) * jnp.where(tri_strict, decay, 0.0)
    eye = jnp.eye(DN_CHUNK, dtype=f32)
    t_mat = lax.linalg.triangular_solve(eye + strict, jnp.broadcast_to(eye, strict.shape),
                                        left_side=True, lower=True, unit_diagonal=True)
    u = jnp.einsum('bhnij,bhnjd->bhnid', t_mat, v_beta)
    w = jnp.einsum('bhnij,bhnjd->bhnid', t_mat, k_beta * jnp.exp(g)[..., None])
    qk = jnp.einsum('bhnid,bhnjd->bhnij', q, k) * decay
    q_dec = q * jnp.exp(g)[..., None]
    k_dec = k * jnp.exp(g[..., -1:] - g)[..., None]
    g_last = jnp.exp(g[..., -1])

    def step(state, xs):
        qk_c, q_c, k_c, u_c, w_c, gl = xs
        v_new = u_c - jnp.einsum('bhcd,bhde->bhce', w_c, state)
        o_c = jnp.einsum('bhcd,bhde->bhce', q_c, state) + jnp.einsum('bhij,bhje->bhie', qk_c, v_new)
        state = state * gl[..., None, None] + jnp.einsum('bhcd,bhce->bhde', k_c, v_new)
        return state, o_c

    xs = tuple(jnp.moveaxis(t, 2, 0) for t in (qk, q_dec, k_dec, u, w, g_last))
    state0 = jnp.zeros((b_sz, n_h, d_k, d_v), dtype=f32)
    _, o = lax.scan(step, state0, xs)
    return jnp.moveaxis(o, 0, 2).reshape(b_sz, n_h, s_len, d_v).astype(out_dtype)


def moba_attention(q, k, v):
    b_sz, n_h, s_len, d = q.shape
    f32 = jnp.float32
    n_blocks = -(-s_len // MOBA_BLOCK)
    s_pad = n_blocks * MOBA_BLOCK
    pad = ((0, 0), (0, 0), (0, s_pad - s_len), (0, 0))
    kb = jnp.pad(k, pad).reshape(b_sz, n_h, n_blocks, MOBA_BLOCK, d)
    vb = jnp.pad(v, pad).reshape(b_sz, n_h, n_blocks, MOBA_BLOCK, d)
    k_mean = jnp.mean(kb.astype(f32), axis=3)
    pos = jnp.arange(s_len)
    q_block = pos // MOBA_BLOCK
    gate = jnp.einsum('bhsd,bhnd->bhsn', q.astype(f32), k_mean)
    fully_past = jnp.arange(n_blocks)[None, :] < q_block[:, None]
    gate = jnp.where(fully_past, gate, -jnp.inf)
    n_cand = max(n_blocks, MOBA_TOPK)
    gate = jnp.pad(gate, ((0, 0), (0, 0), (0, 0), (0, n_cand - n_blocks)), constant_values=-jnp.inf)
    top_score, top_idx = lax.top_k(gate, MOBA_TOPK)
    sel_valid = top_score > -jnp.inf
    sel_idx = jnp.minimum(top_idx, n_blocks - 1)

    n_q = s_len // MOBA_Q_CHUNK

    def to_chunks(t):
        return jnp.moveaxis(t.reshape(b_sz, n_h, n_q, MOBA_Q_CHUNK, *t.shape[3:]), 2, 0)

    b_ix = jnp.arange(b_sz)[:, None, None, None]
    h_ix = jnp.arange(n_h)[None, :, None, None]
    key_off = jnp.arange(MOBA_BLOCK)
    scale = d ** -0.5
    n_sel = MOBA_TOPK * MOBA_BLOCK

    def attend_chunk(args):
        q_c, idx_c, valid_c, pos_c = args
        q_len = q_c.shape[2]
        k_sel = kb[b_ix, h_ix, idx_c].astype(f32)
        v_sel = vb[b_ix, h_ix, idx_c].astype(f32)
        own = pos_c[0] // MOBA_BLOCK
        k_own = lax.dynamic_index_in_dim(kb, own, axis=2, keepdims=False).astype(f32)
        v_own = lax.dynamic_index_in_dim(vb, own, axis=2, keepdims=False).astype(f32)
        qf = q_c.astype(f32)
        s_sel = jnp.einsum('bhqd,bhqnkd->bhqnk', qf, k_sel) * scale
        s_sel = jnp.where(valid_c[..., None], s_sel, -jnp.inf)
        s_own = jnp.einsum('bhqd,bhkd->bhqk', qf, k_own) * scale
        causal = (own * MOBA_BLOCK + key_off)[None, :] <= pos_c[:, None]
        s_own = jnp.where(causal, s_own, -jnp.inf)
        scores = jnp.concatenate([s_sel.reshape(b_sz, n_h, q_len, n_sel), s_own], axis=-1)
        p = jax.nn.softmax(scores, axis=-1)
        p_sel = p[..., :n_sel].reshape(b_sz, n_h, q_len, MOBA_TOPK, MOBA_BLOCK)
        p_own = p[..., n_sel:]
        o = (jnp.einsum('bhqnk,bhqnkd->bhqd', p_sel, v_sel)
             + jnp.einsum('bhqk,bhkd->bhqd', p_own, v_own))
        return o.astype(q_c.dtype)

    out = lax.map(attend_chunk, (to_chunks(q), to_chunks(sel_idx), to_chunks(sel_valid),
                                 pos.reshape(n_q, MOBA_Q_CHUNK)))
    return jnp.moveaxis(out, 0, 2).reshape(b_sz, n_h, s_len, d)


def hybrid_mixer(h, w_in, dn_conv_w, dn_A_log, dn_dt_bias, dn_norm_w, w_dn_out, w_moba_out, w_o):
    b_sz, s_len, _ = h.shape
    proj = h @ w_in
    offsets = [int(o) for o in np.cumsum(IN_SPLITS)[:-1]]
    (dn_q, dn_k, dn_v, dn_z, dn_b, dn_a,
     mb_q, mb_k, mb_v, gate_dn, gate_mb) = jnp.split(proj, offsets, axis=-1)

    def heads(t, n):
        return t.reshape(b_sz, s_len, n, -1).transpose(0, 2, 1, 3)

    qkv = jax.nn.silu(causal_depthwise_conv(jnp.concatenate([dn_q, dn_k, dn_v], axis=-1), dn_conv_w))
    dq, dk, dv = jnp.split(qkv, 3, axis=-1)
    dq = l2_normalize(heads(dq, DN_HEADS)) * (DN_HEAD_DIM ** -0.5)
    dk = l2_normalize(heads(dk, DN_HEADS))
    dv = heads(dv, DN_HEADS)
    beta = jax.nn.sigmoid(dn_b.astype(jnp.float32)).transpose(0, 2, 1)
    g = -jnp.exp(dn_A_log.astype(jnp.float32)) * jax.nn.softplus(
        dn_a.astype(jnp.float32) + dn_dt_bias.astype(jnp.float32))
    g = g.transpose(0, 2, 1)
    o_dn = gated_delta_rule(dq, dk, dv, beta, g).transpose(0, 2, 1, 3)
    z = dn_z.reshape(b_sz, s_len, DN_HEADS, DN_HEAD_DIM)
    o_dn = rms_norm(o_dn, dn_norm_w) * jax.nn.silu(z)
    y_dn = o_dn.reshape(b_sz, s_len, DN_WIDTH) @ w_dn_out

    pos = jnp.arange(s_len)
    mq = apply_rope(heads(mb_q, MOBA_HEADS), pos)
    mk = apply_rope(heads(mb_k, MOBA_HEADS), pos)
    mv = heads(mb_v, MOBA_HEADS)
    o_mb = moba_attention(mq, mk, mv).transpose(0, 2, 1, 3).reshape(b_sz, s_len, MOBA_WIDTH)
    y_mb = o_mb @ w_moba_out

    merged = jax.nn.sigmoid(gate_dn) * y_dn + jax.nn.sigmoid(gate_mb) * y_mb
    return merged @ w_o


def conv_ffn(h, w_up, conv_w, conv_b, w_down):
    u = causal_depthwise_conv(h @ w_up, conv_w, conv_b)
    gate, up = jnp.split(u, 2, axis=-1)
    return (jax.nn.silu(gate) * up) @ w_down


def setup_inputs(seed: int = 0) -> dict:
    key = jax.random.key(seed)
    ks = jax.random.split(key, 20)
    f32 = jnp.float32
    L = DEPTH

    def normal(k, shape, scale):
        return jax.random.normal(k, shape, dtype=f32) * scale

    x = normal(ks[0], (BATCH, SEQ, D_MODEL), 1.0)
    attn_norm_w = 1.0 + normal(ks[1], (L, D_MODEL), 0.02)
    w_in = normal(ks[2], (L, D_MODEL, IN_WIDTH), D_MODEL ** -0.5)
    dn_conv_w = normal(ks[3], (L, DN_CONV, 3 * DN_WIDTH), DN_CONV ** -0.5)
    dn_A_log = jnp.log(jax.random.uniform(ks[4], (L, DN_HEADS), dtype=f32, minval=1.0, maxval=16.0))
    dt = jnp.exp(jax.random.uniform(ks[5], (L, DN_HEADS), dtype=f32,
                                    minval=float(np.log(1e-3)), maxval=float(np.log(1e-1))))
    dn_dt_bias = dt + jnp.log(-jnp.expm1(-dt))
    dn_norm_w = 1.0 + normal(ks[6], (L, DN_HEAD_DIM), 0.02)
    w_dn_out = normal(ks[7], (L, DN_WIDTH, D_MODEL), DN_WIDTH ** -0.5)
    w_moba_out = normal(ks[8], (L, MOBA_WIDTH, D_MODEL), MOBA_WIDTH ** -0.5)
    w_o = normal(ks[9], (L, D_MODEL, D_MODEL), D_MODEL ** -0.5)
    ffn_norm_w = 1.0 + normal(ks[10], (L, D_MODEL), 0.02)
    w_up = normal(ks[11], (L, D_MODEL, 2 * D_FF), D_MODEL ** -0.5)
    ffn_conv_w = normal(ks[12], (L, FFN_CONV, 2 * D_FF), FFN_CONV ** -0.5)
    ffn_conv_b = normal(ks[13], (L, 2 * D_FF), 0.01)
    w_down = normal(ks[14], (L, D_FF, D_MODEL), D_FF ** -0.5)
    final_norm_w = 1.0 + normal(ks[15], (D_MODEL,), 0.02)
    return {"x": x, "attn_norm_w": attn_norm_w, "w_in": w_in, "dn_conv_w": dn_conv_w,
            "dn_A_log": dn_A_log, "dn_dt_bias": dn_dt_bias, "dn_norm_w": dn_norm_w,
            "w_dn_out": w_dn_out, "w_moba_out": w_moba_out, "w_o": w_o,
            "ffn_norm_w": ffn_norm_w, "w_up": w_up, "ffn_conv_w": ffn_conv_w,
            "ffn_conv_b": ffn_conv_b, "w_down": w_down, "final_norm_w": final_norm_w}


def reference(x, attn_norm_w, w_in, dn_conv_w, dn_A_log, dn_dt_bias, dn_norm_w, w_dn_out,
              w_moba_out, w_o, ffn_norm_w, w_up, ffn_conv_w, ffn_conv_b, w_down, final_norm_w):
    h = x
    for l in range(DEPTH):
        n = rms_norm(h, attn_norm_w[l])
        h = h + hybrid_mixer(n, w_in[l], dn_conv_w[l], dn_A_log[l], dn_dt_bias[l], dn_norm_w[l],
                             w_dn_out[l], w_moba_out[l], w_o[l])
        n = rms_norm(h, ffn_norm_w[l])
        h = h + conv_ffn(n, w_up[l], ffn_conv_w[l], ffn_conv_b[l], w_down[l])
    return rms_norm(h, final_norm_w)
```

```python
import functools

import jax
import jax.numpy as jnp
import numpy as np
from jax import lax
from jax.experimental import pallas as pl
from jax.experimental.pallas import tpu as pltpu

D_MODEL = 1024
DN_HEADS = 8
DN_HEAD_DIM = 128
DN_WIDTH = DN_HEADS * DN_HEAD_DIM
DN_CONV = 4
DN_CHUNK = 64
MOBA_HEADS = 8
MOBA_HEAD_DIM = 128
MOBA_WIDTH = MOBA_HEADS * MOBA_HEAD_DIM
MOBA_BLOCK = 256
MOBA_TOPK = 3
ROPE_THETA = 10000.0
D_FF = 2816
FFN_CONV = 3
NORM_EPS = 1e-6

SMALL_W = 128
SUBLANES = 8
AUG_W = 2 * MOBA_HEAD_DIM
MASK_PENALTY = -(2.0 ** 100)
VMEM_LIMIT = 52 * 1024 * 1024

F32 = jnp.float32
BF16 = jnp.bfloat16
NEG_INF = float("-inf")


def _dot(a, b):
    return jnp.dot(a.astype(BF16), b.astype(BF16), preferred_element_type=F32)


def _dot_nt(a, b):
    return lax.dot_general(a.astype(BF16), b.astype(BF16), (((1,), (1,)), ((), ())),
                           preferred_element_type=F32)


def _dot_tn(a, b):
    return lax.dot_general(a.astype(BF16), b.astype(BF16), (((0,), (0,)), ((), ())),
                           preferred_element_type=F32)


def _sigmoid(x):
    return 1.0 / (1.0 + jnp.exp(-x))


def _silu(x):
    return x * _sigmoid(x)


def _softplus(x):
    return jnp.maximum(x, 0.0) + jnp.log1p(jnp.exp(-jnp.abs(x)))


def _in_proj_kernel(x_ref, nw_ref, wm_ref, ws_ref, wst_ref, main_ref, small_ref, smallt_ref, n_scr):
    @pl.when(pl.program_id(1) == 0)
    def _():
        x = x_ref[...]
        var = jnp.mean(x * x, axis=-1, keepdims=True)
        n = ((x * lax.rsqrt(var + NORM_EPS)) * nw_ref[...]).astype(BF16)
        n_scr[...] = n
        small_ref[...] = jnp.dot(n, ws_ref[...], preferred_element_type=F32)
        smallt_ref[...] = lax.dot_general(wst_ref[...], n, (((1,), (1,)), ((), ())),
                                          preferred_element_type=F32)

    main_ref[...] = jnp.dot(n_scr[...], wm_ref[...], preferred_element_type=F32)


def _in_proj(x2d, norm_w, w_main, w_small, w_small_t, tm):
    t, d = x2d.shape
    n_main = w_main.shape[1]
    tn = D_MODEL
    grid = (t // tm, n_main // tn)
    return pl.pallas_call(
        _in_proj_kernel,
        grid=grid,
        in_specs=[
            pl.BlockSpec((tm, d), lambda i, j: (i, 0)),
            pl.BlockSpec((1, d), lambda i, j: (0, 0)),
            pl.BlockSpec((d, tn), lambda i, j: (0, j)),
            pl.BlockSpec((d, SMALL_W), lambda i, j: (0, 0)),
            pl.BlockSpec((2 * DN_HEADS, d), lambda i, j: (0, 0)),
        ],
        out_specs=[
            pl.BlockSpec((tm, tn), lambda i, j: (i, j)),
            pl.BlockSpec((tm, SMALL_W), lambda i, j: (i, 0)),
            pl.BlockSpec((2 * DN_HEADS, tm), lambda i, j: (0, i)),
        ],
        out_shape=[
            jax.ShapeDtypeStruct((t, n_main), F32),
            jax.ShapeDtypeStruct((t, SMALL_W), F32),
            jax.ShapeDtypeStruct((2 * DN_HEADS, t), F32),
        ],
        scratch_shapes=[pltpu.VMEM((tm, d), BF16)],
        compiler_params=pltpu.CompilerParams(
            dimension_semantics=("parallel", "arbitrary"), vmem_limit_bytes=VMEM_LIMIT),
        name="in_proj",
    )(x2d, norm_w, w_main, w_small, w_small_t)


def _chunk_cumsum(x, axis):
    idx = lax.broadcasted_iota(jnp.int32, x.shape, axis) % DN_CHUNK
    shift = 1
    while shift < DN_CHUNK:
        x = x + jnp.where(idx >= shift, pltpu.roll(x, shift, axis), 0.0)
        shift *= 2
    return x


def _unit_lower_inverse(a, row, col):
    c = a.shape[0]
    eye = (row == col).astype(F32)
    blk = 8
    n = jnp.where((row // blk) == (col // blk), -a, 0.0)
    n2 = _dot(n, n)
    n4 = _dot(n2, n2)
    t = _dot(eye + n, eye + n2)
    t = _dot(t, eye + n4)
    while blk < c:
        off = ((row // (2 * blk)) == (col // (2 * blk))) & ((row // blk) != (col // blk))
        a_off = jnp.where(off, a, 0.0)
        t = t - _dot(t, _dot(a_off, t))
        blk *= 2
    return t


def _deltanet_kernel(q_ref, k_ref, v_ref, qh_ref, kh_ref, vh_ref, z_ref, small_ref, smallt_ref,
                     cwq_ref, cwk_ref, cwv_ref, alog_r_ref, dtb_r_ref, alog_c_ref, dtb_c_ref,
                     normw_ref, out_ref,
                     state_scr, conv_scr, qc_scr, kc_scr, vc_scr, u_scr, w_scr, qd_scr, kd_scr,
                     qk_scr, gl_scr, bcol_scr, gcol_scr, grow_scr):
    tc = q_ref.shape[0]
    n_chunks = tc // DN_CHUNK
    first = pl.program_id(1) == 0

    @pl.when(first)
    def _():
        state_scr[...] = jnp.zeros_like(state_scr)

    def conv_silu(x_ref, halo_ref, cw_ref):
        halo = jnp.where(first, 0.0, halo_ref[...])
        conv_scr[0:SUBLANES, :] = halo
        conv_scr[SUBLANES:SUBLANES + tc, :] = x_ref[...]
        base = SUBLANES - (DN_CONV - 1)
        y = conv_scr[SUBLANES:SUBLANES + tc, :] * cw_ref[DN_CONV - 1:DN_CONV, :]
        for j in range(DN_CONV - 1):
            y = y + conv_scr[base + j:base + j + tc, :] * cw_ref[j:j + 1, :]
        return _silu(y)

    def l2norm_heads(y, dst, scale):
        for h in range(DN_HEADS):
            seg = y[:, h * DN_HEAD_DIM:(h + 1) * DN_HEAD_DIM]
            ss = jnp.sum(seg * seg, axis=-1, keepdims=True)
            seg = seg * lax.rsqrt(ss + NORM_EPS)
            if scale is not None:
                seg = seg * scale
            dst[:, h * DN_HEAD_DIM:(h + 1) * DN_HEAD_DIM] = seg

    l2norm_heads(conv_silu(q_ref, qh_ref, cwq_ref), qc_scr, DN_HEAD_DIM ** -0.5)
    l2norm_heads(conv_silu(k_ref, kh_ref, cwk_ref), kc_scr, None)
    vc_scr[...] = conv_silu(v_ref, vh_ref, cwv_ref)

    small = small_ref[...]
    bcol_scr[...] = _sigmoid(small)
    gcol_scr[...] = _chunk_cumsum(-jnp.exp(alog_r_ref[...]) * _softplus(small + dtb_r_ref[...]), 0)
    small_t = smallt_ref[...]
    g_t = _chunk_cumsum(-jnp.exp(alog_c_ref[...]) * _softplus(small_t + dtb_c_ref[...]), 1)
    for c in range(n_chunks):
        grow_scr[c] = g_t[:, c * DN_CHUNK:(c + 1) * DN_CHUNK]

    row = lax.broadcasted_iota(jnp.int32, (DN_CHUNK, DN_CHUNK), 0)
    col = lax.broadcasted_iota(jnp.int32, (DN_CHUNK, DN_CHUNK), 1)
    lower_incl = row >= col
    lower_strict = row > col

    def prep(c, carry):
        r0 = pl.multiple_of(c * DN_CHUNK, DN_CHUNK)
        rows = pl.ds(r0, DN_CHUNK)
        g_rows = grow_scr[c]
        for h in range(DN_HEADS):
            cols = slice(h * DN_HEAD_DIM, (h + 1) * DN_HEAD_DIM)
            q = qc_scr[rows, cols]
            k = kc_scr[rows, cols]
            v = vc_scr[rows, cols]
            beta = bcol_scr[rows, h:h + 1]
            gc = gcol_scr[rows, DN_HEADS + h:DN_HEADS + h + 1]
            gc_b = jnp.broadcast_to(gc, (DN_CHUNK, DN_HEAD_DIM))
            g_last_b = jnp.broadcast_to(gc_b[DN_CHUNK - 1:DN_CHUNK, :], (DN_CHUNK, DN_HEAD_DIM))
            eg_b = jnp.exp(gc_b)
            g_row = g_rows[DN_HEADS + h:DN_HEADS + h + 1, :]
            decay = jnp.exp(jnp.where(lower_incl, gc_b[:, :DN_CHUNK] - g_row, NEG_INF))
            kb = k * beta
            vb = v * beta
            strict = _dot_nt(kb, k) * jnp.where(lower_strict, decay, 0.0)
            t_mat = _unit_lower_inverse(strict, row, col)
            u_scr[rows, cols] = _dot(t_mat, vb)
            w_scr[rows, cols] = _dot(t_mat, kb * eg_b).astype(BF16)
            qk_scr[h, rows, :] = (_dot_nt(q, k) * decay).astype(BF16)
            qd_scr[rows, cols] = (q * eg_b).astype(BF16)
            kd_scr[rows, cols] = (k * jnp.exp(g_last_b - gc_b)).astype(BF16)
            gl_scr[c, h:h + 1, :] = eg_b[DN_CHUNK - 1:DN_CHUNK, :]
        return carry

    lax.fori_loop(0, n_chunks, prep, 0)

    normw = normw_ref[...]

    def scan(c, carry):
        r0 = pl.multiple_of(c * DN_CHUNK, DN_CHUNK)
        rows = pl.ds(r0, DN_CHUNK)
        gl_all = gl_scr[c]
        for h in range(DN_HEADS):
            cols = slice(h * DN_HEAD_DIM, (h + 1) * DN_HEAD_DIM)
            state = state_scr[h]
            state_b = state.astype(BF16)
            v_new = u_scr[rows, cols] - jnp.dot(w_scr[rows, cols], state_b,
                                                preferred_element_type=F32)
            v_new_b = v_new.astype(BF16)
            o = (jnp.dot(qd_scr[rows, cols], state_b, preferred_element_type=F32)
                 + jnp.dot(qk_scr[h, rows, :], v_new_b, preferred_element_type=F32))
            state_scr[h] = state * gl_all[h:h + 1, :] + lax.dot_general(
                kd_scr[rows, cols], v_new_b, (((0,), (0,)), ((), ())), preferred_element_type=F32)
            var = jnp.mean(o * o, axis=-1, keepdims=True)
            o = (o * lax.rsqrt(var + NORM_EPS)) * normw
            out_ref[rows, cols] = (o * _silu(z_ref[rows, cols])).astype(out_ref.dtype)
        return carry

    lax.fori_loop(0, n_chunks, scan, 0)


def _deltanet(main3d, small3d, small_t3, conv_w, alog_r, dtb_r, alog_c, dtb_c, norm_w, tc):
    b_sz, s_len, _ = main3d.shape
    n_t = s_len // tc
    n_chunks = tc // DN_CHUNK
    halo_blocks = tc // SUBLANES

    def tile_spec(cb):
        return pl.BlockSpec((None, tc, DN_WIDTH), lambda b, i, cb=cb: (b, i, cb))

    def halo_spec(cb):
        return pl.BlockSpec((None, SUBLANES, DN_WIDTH),
                            lambda b, i, cb=cb: (b, jnp.maximum(i * halo_blocks - 1, 0), cb))

    def const_spec(shape):
        return pl.BlockSpec(shape, lambda b, i: tuple(0 for _ in shape))

    in_specs = [
        tile_spec(0), tile_spec(1), tile_spec(2),
        halo_spec(0), halo_spec(1), halo_spec(2),
        tile_spec(3),
        pl.BlockSpec((None, tc, SMALL_W), lambda b, i: (b, i, 0)),
        pl.BlockSpec((2 * DN_HEADS, tc), lambda b, i: (0, b * n_t + i)),
        pl.BlockSpec((DN_CONV, DN_WIDTH), lambda b, i: (0, 0)),
        pl.BlockSpec((DN_CONV, DN_WIDTH), lambda b, i: (0, 1)),
        pl.BlockSpec((DN_CONV, DN_WIDTH), lambda b, i: (0, 2)),
        const_spec((1, SMALL_W)), const_spec((1, SMALL_W)),
        const_spec((2 * DN_HEADS, 1)), const_spec((2 * DN_HEADS, 1)),
        const_spec((1, DN_HEAD_DIM)),
    ]
    scratch = [
        pltpu.VMEM((DN_HEADS, DN_HEAD_DIM, DN_HEAD_DIM), F32),
        pltpu.VMEM((tc + SUBLANES, DN_WIDTH), F32),
        pltpu.VMEM((tc, DN_WIDTH), F32),
        pltpu.VMEM((tc, DN_WIDTH), F32),
        pltpu.VMEM((tc, DN_WIDTH), F32),
        pltpu.VMEM((tc, DN_WIDTH), F32),
        pltpu.VMEM((tc, DN_WIDTH), BF16),
        pltpu.VMEM((tc, DN_WIDTH), BF16),
        pltpu.VMEM((tc, DN_WIDTH), BF16),
        pltpu.VMEM((DN_HEADS, tc, DN_CHUNK), BF16),
        pltpu.VMEM((n_chunks, DN_HEADS, DN_HEAD_DIM), F32),
        pltpu.VMEM((tc, SMALL_W), F32),
        pltpu.VMEM((tc, SMALL_W), F32),
        pltpu.VMEM((n_chunks, 2 * DN_HEADS, DN_CHUNK), F32),
    ]
    return pl.pallas_call(
        _deltanet_kernel,
        grid=(b_sz, n_t),
        in_specs=in_specs,
        out_specs=pl.BlockSpec((None, tc, DN_WIDTH), lambda b, i: (b, i, 0)),
        out_shape=jax.ShapeDtypeStruct((b_sz, s_len, DN_WIDTH), BF16),
        scratch_shapes=scratch,
        compiler_params=pltpu.CompilerParams(
            dimension_semantics=("parallel", "arbitrary"), vmem_limit_bytes=VMEM_LIMIT),
        name="deltanet",
    )(main3d, main3d, main3d, main3d, main3d, main3d, main3d, small3d, small_t3,
      conv_w, conv_w, conv_w, alog_r, dtb_r, alog_c, dtb_c, norm_w)


def _moba_prep_kernel(q_ref, k_ref, v_ref, cos_ref, sin_ref, qa_ref, ka_ref, vo_ref, kmean_scr):
    blk = pl.program_id(1)
    n_rows = q_ref.shape[0]

    @pl.when(blk == 0)
    def _():
        kmean_scr[...] = jnp.zeros_like(kmean_scr)

    cos = cos_ref[...]
    sin = sin_ref[...]
    lane = lax.broadcasted_iota(jnp.int32, (n_rows, MOBA_HEAD_DIM), 1)
    lane_f = lane.astype(F32)
    mean_row = lax.broadcasted_iota(jnp.int32, (MOBA_HEAD_DIM, MOBA_HEAD_DIM), 0)
    onehot = (lane == blk).astype(BF16)
    vo_ref[...] = v_ref[...].astype(BF16)

    for h in range(MOBA_HEADS):
        cols = slice(h * MOBA_HEAD_DIM, (h + 1) * MOBA_HEAD_DIM)
        acols = slice(h * AUG_W, h * AUG_W + MOBA_HEAD_DIM)
        pcols = slice(h * AUG_W + MOBA_HEAD_DIM, (h + 1) * AUG_W)
        q = q_ref[:, cols]
        k = k_ref[:, cols]
        half = MOBA_HEAD_DIM // 2
        q = q * cos + pltpu.roll(q, half, 1) * sin
        k = k * cos + pltpu.roll(k, half, 1) * sin

        kmean = kmean_scr[h]
        gate = lax.dot_general(q, kmean, (((1,), (1,)), ((), ())),
                               precision=lax.Precision.HIGHEST, preferred_element_type=F32)
        gate = jnp.where(lane < blk, gate, NEG_INF)
        sel = lane == blk
        for _ in range(MOBA_TOPK):
            mx = jnp.max(gate, axis=-1, keepdims=True)
            first_idx = jnp.min(jnp.where(gate == mx, lane_f, float(MOBA_HEAD_DIM)),
                                axis=-1, keepdims=True)
            hit = lane_f == first_idx
            sel = sel | (hit & (mx > NEG_INF))
            gate = jnp.where(hit, NEG_INF, gate)
        qa_ref[:, acols] = q.astype(BF16)
        qa_ref[:, pcols] = jnp.where(sel, 0.0, MASK_PENALTY).astype(BF16)
        ka_ref[:, acols] = k.astype(BF16)
        ka_ref[:, pcols] = onehot

        k_mean_row = jnp.mean(k, axis=0, keepdims=True)
        kmean_scr[h] = jnp.where(mean_row == blk, k_mean_row, kmean)


def _moba_prep(main3d, cos_t, sin_t):
    b_sz, s_len, _ = main3d.shape
    n_blk = s_len // MOBA_BLOCK
    assert n_blk <= MOBA_HEAD_DIM

    def tile_spec(cb):
        return pl.BlockSpec((None, MOBA_BLOCK, MOBA_WIDTH), lambda b, i, cb=cb: (b, i, cb))

    tab_spec = pl.BlockSpec((MOBA_BLOCK, MOBA_HEAD_DIM), lambda b, i: (i, 0))
    return pl.pallas_call(
        _moba_prep_kernel,
        grid=(b_sz, n_blk),
        in_specs=[tile_spec(4), tile_spec(5), tile_spec(6), tab_spec, tab_spec],
        out_specs=[
            pl.BlockSpec((None, MOBA_BLOCK, MOBA_HEADS * AUG_W), lambda b, i: (b, i, 0)),
            pl.BlockSpec((None, MOBA_BLOCK, MOBA_HEADS * AUG_W), lambda b, i: (b, i, 0)),
            pl.BlockSpec((None, MOBA_BLOCK, MOBA_WIDTH), lambda b, i: (b, i, 0)),
        ],
        out_shape=[
            jax.ShapeDtypeStruct((b_sz, s_len, MOBA_HEADS * AUG_W), BF16),
            jax.ShapeDtypeStruct((b_sz, s_len, MOBA_HEADS * AUG_W), BF16),
            jax.ShapeDtypeStruct((b_sz, s_len, MOBA_WIDTH), BF16),
        ],
        scratch_shapes=[pltpu.VMEM((MOBA_HEADS, MOBA_HEAD_DIM, MOBA_HEAD_DIM), F32)],
        compiler_params=pltpu.CompilerParams(
            dimension_semantics=("parallel", "arbitrary"), vmem_limit_bytes=VMEM_LIMIT),
        name="moba_prep",
    )(main3d, main3d, main3d, cos_t, sin_t)


def _moba_attn_kernel(qa_ref, ka_ref, v_ref, out_ref):
    blk = pl.program_id(2)
    q = qa_ref[...]
    scale = MOBA_HEAD_DIM ** -0.5

    def scores(j):
        r0 = pl.multiple_of(j * MOBA_BLOCK, MOBA_BLOCK)
        k = ka_ref[pl.ds(r0, MOBA_BLOCK), :]
        v = v_ref[pl.ds(r0, MOBA_BLOCK), :]
        s = lax.dot_general(q, k, (((1,), (1,)), ((), ())), preferred_element_type=F32) * scale
        return s, v

    s, v = scores(blk)
    row = lax.broadcasted_iota(jnp.int32, s.shape, 0)
    col = lax.broadcasted_iota(jnp.int32, s.shape, 1)
    s = jnp.where(col <= row, s, NEG_INF)
    m = jnp.max(s, axis=-1, keepdims=True)
    p = jnp.exp(s - m)
    l = jnp.sum(p, axis=-1, keepdims=True)
    acc = jnp.dot(p.astype(BF16), v, preferred_element_type=F32)

    def body(j, carry):
        m, l, acc = carry
        s, v = scores(j)
        m_new = jnp.maximum(m, jnp.max(s, axis=-1, keepdims=True))
        alpha = jnp.exp(m - m_new)
        p = jnp.exp(s - m_new)
        l = alpha * l + jnp.sum(p, axis=-1, keepdims=True)
        acc = alpha * acc + jnp.dot(p.astype(BF16), v, preferred_element_type=F32)
        return m_new, l, acc

    m, l, acc = lax.fori_loop(0, blk, body, (m, l, acc))
    out_ref[...] = (acc / l).astype(out_ref.dtype)


def _moba_attn(qa, ka, vb):
    b_sz, s_len, _ = qa.shape
    n_blk = s_len // MOBA_BLOCK
    return pl.pallas_call(
        _moba_attn_kernel,
        grid=(b_sz, MOBA_HEADS, n_blk),
        in_specs=[
            pl.BlockSpec((None, MOBA_BLOCK, AUG_W), lambda b, h, i: (b, i, h)),
            pl.BlockSpec((None, s_len, AUG_W), lambda b, h, i: (b, 0, h)),
            pl.BlockSpec((None, s_len, MOBA_HEAD_DIM), lambda b, h, i: (b, 0, h)),
        ],
        out_specs=pl.BlockSpec((None, MOBA_BLOCK, MOBA_HEAD_DIM), lambda b, h, i: (b, i, h)),
        out_shape=jax.ShapeDtypeStruct((b_sz, s_len, MOBA_WIDTH), BF16),
        compiler_params=pltpu.CompilerParams(
            dimension_semantics=("parallel", "parallel", "arbitrary"),
            vmem_limit_bytes=VMEM_LIMIT),
        name="moba_attn",
    )(qa, ka, vb)


def _out_proj_kernel(x_ref, odn_ref, omb_ref, gdn_ref, gmb_ref, wdn_ref, wmb_ref, wo_ref, h_ref):
    y_dn = jnp.dot(odn_ref[...], wdn_ref[...], preferred_element_type=F32)
    y_mb = jnp.dot(omb_ref[...], wmb_ref[...], preferred_element_type=F32)
    merged = _sigmoid(gdn_ref[...]) * y_dn + _sigmoid(gmb_ref[...]) * y_mb
    h_ref[...] = x_ref[...] + jnp.dot(merged.astype(BF16), wo_ref[...], preferred_element_type=F32)


def _out_proj(x2d, o_dn, o_mb, main2d, w_dn, w_mb, w_o, tm):
    t, d = x2d.shape
    row_spec = pl.BlockSpec((tm, d), lambda i: (i, 0))
    w_spec = pl.BlockSpec((d, d), lambda i: (0, 0))
    return pl.pallas_call(
        _out_proj_kernel,
        grid=(t // tm,),
        in_specs=[row_spec, row_spec, row_spec,
                  pl.BlockSpec((tm, d), lambda i: (i, 7)),
                  pl.BlockSpec((tm, d), lambda i: (i, 8)),
                  w_spec, w_spec, w_spec],
        out_specs=row_spec,
        out_shape=jax.ShapeDtypeStruct((t, d), F32),
        compiler_params=pltpu.CompilerParams(
            dimension_semantics=("parallel",), vmem_limit_bytes=VMEM_LIMIT),
        name="out_proj",
    )(x2d, o_dn, o_mb, main2d, main2d, w_dn, w_mb, w_o)


def _conv_ffn_kernel(h_ref, nw_ref, wup_ref, cw_ref, cb_ref, wdown_ref, fw_ref, out_ref,
                     u_scr, *, tiles_per_seq, n_split):
    tm = h_ref.shape[0]
    seq_start = (pl.program_id(0) % tiles_per_seq) == 0
    h = h_ref[...]
    var = jnp.mean(h * h, axis=-1, keepdims=True)
    n = ((h * lax.rsqrt(var + NORM_EPS)) * nw_ref[...]).astype(BF16)

    width = D_FF // n_split
    base = SUBLANES - (FFN_CONV - 1)
    acc = jnp.zeros((tm, D_MODEL), F32)
    for part in range(n_split):
        halves = []
        for half in range(2):
            c0 = half * D_FF + part * width
            slot = 2 * part + half
            @pl.when(seq_start)
            def _(slot=slot):
                u_scr[slot, 0:SUBLANES, :] = jnp.zeros((SUBLANES, width), F32)

            @pl.when(jnp.logical_not(seq_start))
            def _(slot=slot):
                u_scr[slot, 0:SUBLANES, :] = u_scr[slot, tm:tm + SUBLANES, :]

            u_scr[slot, SUBLANES:SUBLANES + tm, :] = jnp.dot(
                n, wup_ref[:, c0:c0 + width], preferred_element_type=F32)
            y = (u_scr[slot, SUBLANES:SUBLANES + tm, :]
                 * cw_ref[FFN_CONV - 1:FFN_CONV, c0:c0 + width])
            for j in range(FFN_CONV - 1):
                y = y + u_scr[slot, base + j:base + j + tm, :] * cw_ref[j:j + 1, c0:c0 + width]
            halves.append(y + cb_ref[:, c0:c0 + width])
        act = (_silu(halves[0]) * halves[1]).astype(BF16)
        acc = acc + jnp.dot(act, wdown_ref[part * width:(part + 1) * width, :],
                            preferred_element_type=F32)
    h2 = h + acc
    var2 = jnp.mean(h2 * h2, axis=-1, keepdims=True)
    out_ref[...] = (h2 * lax.rsqrt(var2 + NORM_EPS)) * fw_ref[...]


def _conv_ffn(h2d, norm_w, w_up, conv_w, conv_b, w_down, final_w, tm, s_len, n_split):
    t, d = h2d.shape
    width = D_FF // n_split
    kern = functools.partial(_conv_ffn_kernel, tiles_per_seq=s_len // tm, n_split=n_split)

    def const_spec(shape):
        return pl.BlockSpec(shape, lambda i: (0, 0))

    return pl.pallas_call(
        kern,
        grid=(t // tm,),
        in_specs=[
            pl.BlockSpec((tm, d), lambda i: (i, 0)),
            const_spec((1, d)),
            const_spec((d, 2 * D_FF)),
            const_spec((FFN_CONV, 2 * D_FF)),
            const_spec((1, 2 * D_FF)),
            const_spec((D_FF, d)),
            const_spec((1, d)),
        ],
        out_specs=pl.BlockSpec((tm, d), lambda i: (i, 0)),
        out_shape=jax.ShapeDtypeStruct((t, d), F32),
        scratch_shapes=[pltpu.VMEM((2 * n_split, tm + SUBLANES, width), F32)],
        compiler_params=pltpu.CompilerParams(
            dimension_semantics=("arbitrary",), vmem_limit_bytes=VMEM_LIMIT),
        name="conv_ffn",
    )(h2d, norm_w, w_up, conv_w, conv_b, w_down, final_w)


def _rope_tables(s_len):
    half = MOBA_HEAD_DIM // 2
    inv_freq = jnp.power(ROPE_THETA, -jnp.arange(half, dtype=F32) / half)
    ang = jnp.arange(s_len).astype(F32)[:, None] * inv_freq[None, :]
    cos, sin = jnp.cos(ang), jnp.sin(ang)
    return jnp.concatenate([cos, cos], axis=-1), jnp.concatenate([-sin, sin], axis=-1)


def _pick_tile(n, want):
    t = min(n, want)
    while n % t:
        t //= 2
    return t


def _layer(x, attn_norm_w, w_in, dn_conv_w, dn_a_log, dn_dt_bias, dn_norm_w, w_dn_out,
           w_moba_out, w_o, ffn_norm_w, w_up, ffn_conv_w, ffn_conv_b, w_down, final_norm_w):
    b_sz, s_len, d = x.shape
    t = b_sz * s_len
    x2d = x.reshape(t, d)

    small_lo = 4 * DN_WIDTH
    small_hi = small_lo + 2 * DN_HEADS
    w_main = jnp.concatenate([w_in[:, :small_lo], w_in[:, small_hi:]], axis=1).astype(BF16)
    w_small_cols = w_in[:, small_lo:small_hi]
    w_small = jnp.pad(w_small_cols, ((0, 0), (0, SMALL_W - 2 * DN_HEADS))).astype(BF16)
    w_small_t = w_small_cols.T.astype(BF16)

    main, small, small_t = _in_proj(x2d, attn_norm_w.reshape(1, d), w_main, w_small, w_small_t,
                                    _pick_tile(t, 1024))
    main3d = main.reshape(b_sz, s_len, main.shape[1])
    small3d = small.reshape(b_sz, s_len, SMALL_W)

    pad_r = ((0, 0), (DN_HEADS, SMALL_W - 2 * DN_HEADS))
    alog_r = jnp.pad(dn_a_log.reshape(1, DN_HEADS), pad_r)
    dtb_r = jnp.pad(dn_dt_bias.reshape(1, DN_HEADS), pad_r)
    alog_c = jnp.pad(dn_a_log.reshape(DN_HEADS, 1), ((DN_HEADS, 0), (0, 0)))
    dtb_c = jnp.pad(dn_dt_bias.reshape(DN_HEADS, 1), ((DN_HEADS, 0), (0, 0)))
    o_dn = _deltanet(main3d, small3d, small_t, dn_conv_w, alog_r, dtb_r, alog_c, dtb_c,
                     dn_norm_w.reshape(1, DN_HEAD_DIM), _pick_tile(s_len, 256))

    cos_t, sin_t = _rope_tables(s_len)
    qa, ka, vb = _moba_prep(main3d, cos_t, sin_t)
    o_mb = _moba_attn(qa, ka, vb)

    h = _out_proj(x2d, o_dn.reshape(t, DN_WIDTH), o_mb.reshape(t, MOBA_WIDTH), main,
                  w_dn_out.astype(BF16), w_moba_out.astype(BF16), w_o.astype(BF16),
                  _pick_tile(t, 512))

    tm_ffn = _pick_tile(s_len, 256)
    out = _conv_ffn(h, ffn_norm_w.reshape(1, d), w_up.astype(BF16), ffn_conv_w,
                    ffn_conv_b.reshape(1, 2 * D_FF), w_down.astype(BF16),
                    final_norm_w.reshape(1, d), tm_ffn, s_len, 2)
    return out.reshape(b_sz, s_len, d)


def kernel(x, attn_norm_w, w_in, dn_conv_w, dn_A_log, dn_dt_bias, dn_norm_w, w_dn_out, w_moba_out,
           w_o, ffn_norm_w, w_up, ffn_conv_w, ffn_conv_b, w_down, final_norm_w):
    depth = w_in.shape[0]
    assert depth == 1, "the final RMSNorm is fused into the layer's conv_ffn call"
    return _layer(x, attn_norm_w[0], w_in[0], dn_conv_w[0], dn_A_log[0], dn_dt_bias[0],
                  dn_norm_w[0], w_dn_out[0], w_moba_out[0], w_o[0], ffn_norm_w[0], w_up[0],
                  ffn_conv_w[0], ffn_conv_b[0], w_down[0], final_norm_w)
```

```python
import functools

import jax
import jax.numpy as jnp
import numpy as np
from jax import lax
from jax.experimental import pallas as pl
from jax.experimental.pallas import tpu as pltpu

D_MODEL = 1024
DN_HEADS = 8
DN_HEAD_DIM = 128
DN_WIDTH = DN_HEADS * DN_HEAD_DIM
DN_CONV = 4
DN_CHUNK = 64
MOBA_HEADS = 8
MOBA_HEAD_DIM = 128
MOBA_WIDTH = MOBA_HEADS * MOBA_HEAD_DIM
MOBA_BLOCK = 256
MOBA_TOPK = 3
ROPE_THETA = 10000.0
D_FF = 2816
FFN_CONV = 3
NORM_EPS = 1e-6

SMALL_W = 128
SUBLANES = 8
AUG_W = 2 * MOBA_HEAD_DIM
MASK_PENALTY = -(2.0 ** 100)
VMEM_LIMIT = 52 * 1024 * 1024

F32 = jnp.float32
BF16 = jnp.bfloat16
NEG_INF = float("-inf")


def _dot(a, b):
    return jnp.dot(a.astype(BF16), b.astype(BF16), preferred_element_type=F32)


def _dot_nt(a, b):
    return lax.dot_general(a.astype(BF16), b.astype(BF16), (((1,), (1,)), ((), ())),
                           preferred_element_type=F32)


def _dot_tn(a, b):
    return lax.dot_general(a.astype(BF16), b.astype(BF16), (((0,), (0,)), ((), ())),
                           preferred_element_type=F32)


def _sigmoid(x):
    return 1.0 / (1.0 + jnp.exp(-x))


def _silu(x):
    return x * _sigmoid(x)


def _softplus(x):
    return jnp.maximum(x, 0.0) + jnp.log1p(jnp.exp(-jnp.abs(x)))


def _in_proj_kernel(x_ref, nw_ref, wm_ref, ws_ref, wst_ref, main_ref, small_ref, smallt_ref, n_scr):
    @pl.when(pl.program_id(1) == 0)
    def _():
        x = x_ref[...]
        var = jnp.mean(x * x, axis=-1, keepdims=True)
        n = ((x * lax.rsqrt(var + NORM_EPS)) * nw_ref[...]).astype(BF16)
        n_scr[...] = n
        small_ref[...] = jnp.dot(n, ws_ref[...], preferred_element_type=F32)
        smallt_ref[...] = lax.dot_general(wst_ref[...], n, (((1,), (1,)), ((), ())),
                                          preferred_element_type=F32)

    main_ref[...] = jnp.dot(n_scr[...], wm_ref[...], preferred_element_type=F32)


def _in_proj(x2d, norm_w, w_main, w_small, w_small_t, tm):
    t, d = x2d.shape
    n_main = w_main.shape[1]
    tn = D_MODEL
    grid = (t // tm, n_main // tn)
    return pl.pallas_call(
        _in_proj_kernel,
        grid=grid,
        in_specs=[
            pl.BlockSpec((tm, d), lambda i, j: (i, 0)),
            pl.BlockSpec((1, d), lambda i, j: (0, 0)),
            pl.BlockSpec((d, tn), lambda i, j: (0, j)),
            pl.BlockSpec((d, SMALL_W), lambda i, j: (0, 0)),
            pl.BlockSpec((2 * DN_HEADS, d), lambda i, j: (0, 0)),
        ],
        out_specs=[
            pl.BlockSpec((tm, tn), lambda i, j: (i, j)),
            pl.BlockSpec((tm, SMALL_W), lambda i, j: (i, 0)),
            pl.BlockSpec((2 * DN_HEADS, tm), lambda i, j: (0, i)),
        ],
        out_shape=[
            jax.ShapeDtypeStruct((t, n_main), F32),
            jax.ShapeDtypeStruct((t, SMALL_W), F32),
            jax.ShapeDtypeStruct((2 * DN_HEADS, t), F32),
        ],
        scratch_shapes=[pltpu.VMEM((tm, d), BF16)],
        compiler_params=pltpu.CompilerParams(
            dimension_semantics=("parallel", "arbitrary"), vmem_limit_bytes=VMEM_LIMIT),
        name="in_proj",
    )(x2d, norm_w, w_main, w_small, w_small_t)


def _chunk_cumsum(x, axis):
    idx = lax.broadcasted_iota(jnp.int32, x.shape, axis) % DN_CHUNK
    shift = 1
    while shift < DN_CHUNK:
        x = x + jnp.where(idx >= shift, pltpu.roll(x, shift, axis), 0.0)
        shift *= 2
    return x


def _unit_lower_inverse(mats, row, col):
    c = mats[0].shape[0]
    eye = (row == col).astype(F32)
    blk = 8
    diag_blk = (row // blk) == (col // blk)
    n = [jnp.where(diag_blk, -a, 0.0) for a in mats]
    n2 = [_dot(x, x) for x in n]
    n4 = [_dot(x, x) for x in n2]
    t = [_dot(eye + x, eye + y) for x, y in zip(n, n2)]
    t = [_dot(x, eye + y) for x, y in zip(t, n4)]
    while blk < c:
        off = ((row // (2 * blk)) == (col // (2 * blk))) & ((row // blk) != (col // blk))
        a_t = [_dot(jnp.where(off, a, 0.0), x) for a, x in zip(mats, t)]
        t = [x - _dot(x, y) for x, y in zip(t, a_t)]
        blk *= 2
    return t


def _deltanet_kernel(q_ref, k_ref, v_ref, qh_ref, kh_ref, vh_ref, z_ref, small_ref, smallt_ref,
                     cwq_ref, cwk_ref, cwv_ref, alog_r_ref, dtb_r_ref, alog_c_ref, dtb_c_ref,
                     normw_ref, out_ref,
                     state_scr, conv_scr, qc_scr, kc_scr, vc_scr, u_scr, wq_scr, kd_scr,
                     qk_scr, gl_scr, bcol_scr, gcol_scr, grow_scr):
    tc = q_ref.shape[0]
    n_chunks = tc // DN_CHUNK
    first = pl.program_id(1) == 0
    heads = range(DN_HEADS)
    cols = [slice(h * DN_HEAD_DIM, (h + 1) * DN_HEAD_DIM) for h in heads]

    @pl.when(first)
    def _():
        state_scr[...] = jnp.zeros_like(state_scr)

    def conv_silu(x_ref, halo_ref, cw_ref):
        halo = jnp.where(first, 0.0, halo_ref[...])
        conv_scr[0:SUBLANES, :] = halo
        conv_scr[SUBLANES:SUBLANES + tc, :] = x_ref[...]
        base = SUBLANES - (DN_CONV - 1)
        y = conv_scr[SUBLANES:SUBLANES + tc, :] * cw_ref[DN_CONV - 1:DN_CONV, :]
        for j in range(DN_CONV - 1):
            y = y + conv_scr[base + j:base + j + tc, :] * cw_ref[j:j + 1, :]
        return _silu(y)

    def l2norm_heads(y, dst, scale):
        for h in range(DN_HEADS):
            seg = y[:, h * DN_HEAD_DIM:(h + 1) * DN_HEAD_DIM]
            ss = jnp.sum(seg * seg, axis=-1, keepdims=True)
            seg = seg * lax.rsqrt(ss + NORM_EPS)
            if scale is not None:
                seg = seg * scale
            dst[:, h * DN_HEAD_DIM:(h + 1) * DN_HEAD_DIM] = seg

    l2norm_heads(conv_silu(q_ref, qh_ref, cwq_ref), qc_scr, DN_HEAD_DIM ** -0.5)
    l2norm_heads(conv_silu(k_ref, kh_ref, cwk_ref), kc_scr, None)
    vc_scr[...] = conv_silu(v_ref, vh_ref, cwv_ref)

    small = small_ref[...]
    bcol_scr[...] = _sigmoid(small)
    gcol_scr[...] = _chunk_cumsum(-jnp.exp(alog_r_ref[...]) * _softplus(small + dtb_r_ref[...]), 0)
    small_t = smallt_ref[...]
    g_t = _chunk_cumsum(-jnp.exp(alog_c_ref[...]) * _softplus(small_t + dtb_c_ref[...]), 1)
    for c in range(n_chunks):
        grow_scr[c] = g_t[:, c * DN_CHUNK:(c + 1) * DN_CHUNK]

    row = lax.broadcasted_iota(jnp.int32, (DN_CHUNK, DN_CHUNK), 0)
    col = lax.broadcasted_iota(jnp.int32, (DN_CHUNK, DN_CHUNK), 1)
    lower_incl = row >= col
    lower_strict = row > col

    def prep(c, carry):
        rows = pl.ds(pl.multiple_of(c * DN_CHUNK, DN_CHUNK), DN_CHUNK)
        w_rows = pl.ds(pl.multiple_of(c * 2 * DN_CHUNK, 2 * DN_CHUNK), DN_CHUNK)
        qd_rows = pl.ds(pl.multiple_of(c * 2 * DN_CHUNK + DN_CHUNK, DN_CHUNK), DN_CHUNK)
        g_rows = grow_scr[c]
        q = [qc_scr[rows, cols[h]] for h in heads]
        k = [kc_scr[rows, cols[h]] for h in heads]
        beta = [bcol_scr[rows, h:h + 1] for h in heads]
        gc_b = [jnp.broadcast_to(gcol_scr[rows, DN_HEADS + h:DN_HEADS + h + 1],
                                 (DN_CHUNK, DN_HEAD_DIM)) for h in heads]
        eg_b = [jnp.exp(x) for x in gc_b]
        decay = [jnp.exp(jnp.where(lower_incl,
                                   gc_b[h][:, :DN_CHUNK] - g_rows[DN_HEADS + h:DN_HEADS + h + 1, :],
                                   NEG_INF)) for h in heads]
        kb = [k[h] * beta[h] for h in heads]
        kq = [_dot_nt(jnp.concatenate([kb[h], q[h]], axis=0), k[h]) for h in heads]
        strict = [kq[h][:DN_CHUNK] * jnp.where(lower_strict, decay[h], 0.0) for h in heads]
        for h in heads:
            qk_scr[h, rows, :] = (kq[h][DN_CHUNK:] * decay[h]).astype(BF16)
        t_mat = _unit_lower_inverse(strict, row, col)
        uw = [_dot(t_mat[h], jnp.concatenate([vc_scr[rows, cols[h]] * beta[h], kb[h] * eg_b[h]],
                                             axis=1)) for h in heads]
        for h in heads:
            u_scr[rows, cols[h]] = uw[h][:, :DN_HEAD_DIM]
            wq_scr[w_rows, cols[h]] = uw[h][:, DN_HEAD_DIM:].astype(BF16)
            wq_scr[qd_rows, cols[h]] = (q[h] * eg_b[h]).astype(BF16)
            g_last_b = jnp.broadcast_to(gc_b[h][DN_CHUNK - 1:DN_CHUNK, :], (DN_CHUNK, DN_HEAD_DIM))
            kd_scr[rows, cols[h]] = (k[h] * jnp.exp(g_last_b - gc_b[h])).astype(BF16)
            gl_scr[c, h:h + 1, :] = eg_b[h][DN_CHUNK - 1:DN_CHUNK, :]
        return carry

    lax.fori_loop(0, n_chunks, prep, 0)

    normw = normw_ref[...]

    def scan(c, carry):
        rows = pl.ds(pl.multiple_of(c * DN_CHUNK, DN_CHUNK), DN_CHUNK)
        wq_rows = pl.ds(pl.multiple_of(c * 2 * DN_CHUNK, 2 * DN_CHUNK), 2 * DN_CHUNK)
        gl_all = gl_scr[c]
        state = [state_scr[h] for h in heads]
        state_b = [x.astype(BF16) for x in state]
        ws_qs = [jnp.dot(wq_scr[wq_rows, cols[h]], state_b[h], preferred_element_type=F32)
                 for h in heads]
        v_new_b = [(u_scr[rows, cols[h]] - ws_qs[h][:DN_CHUNK]).astype(BF16) for h in heads]
        intra = [jnp.dot(qk_scr[h, rows, :], v_new_b[h], preferred_element_type=F32) for h in heads]
        d_state = [lax.dot_general(kd_scr[rows, cols[h]], v_new_b[h], (((0,), (0,)), ((), ())),
                                   preferred_element_type=F32) for h in heads]
        for h in heads:
            state_scr[h] = state[h] * gl_all[h:h + 1, :] + d_state[h]
            o = ws_qs[h][DN_CHUNK:] + intra[h]
            var = jnp.mean(o * o, axis=-1, keepdims=True)
            o = (o * lax.rsqrt(var + NORM_EPS)) * normw
            out_ref[rows, cols[h]] = (o * _silu(z_ref[rows, cols[h]])).astype(out_ref.dtype)
        return carry

    lax.fori_loop(0, n_chunks, scan, 0)


def _deltanet(main3d, small3d, small_t3, conv_w, alog_r, dtb_r, alog_c, dtb_c, norm_w, tc):
    b_sz, s_len, _ = main3d.shape
    n_t = s_len // tc
    n_chunks = tc // DN_CHUNK
    halo_blocks = tc // SUBLANES

    def tile_spec(cb):
        return pl.BlockSpec((None, tc, DN_WIDTH), lambda b, i, cb=cb: (b, i, cb))

    def halo_spec(cb):
        return pl.BlockSpec((None, SUBLANES, DN_WIDTH),
                            lambda b, i, cb=cb: (b, jnp.maximum(i * halo_blocks - 1, 0), cb))

    def const_spec(shape):
        return pl.BlockSpec(shape, lambda b, i: tuple(0 for _ in shape))

    in_specs = [
        tile_spec(0), tile_spec(1), tile_spec(2),
        halo_spec(0), halo_spec(1), halo_spec(2),
        tile_spec(3),
        pl.BlockSpec((None, tc, SMALL_W), lambda b, i: (b, i, 0)),
        pl.BlockSpec((2 * DN_HEADS, tc), lambda b, i: (0, b * n_t + i)),
        pl.BlockSpec((DN_CONV, DN_WIDTH), lambda b, i: (0, 0)),
        pl.BlockSpec((DN_CONV, DN_WIDTH), lambda b, i: (0, 1)),
        pl.BlockSpec((DN_CONV, DN_WIDTH), lambda b, i: (0, 2)),
        const_spec((1, SMALL_W)), const_spec((1, SMALL_W)),
        const_spec((2 * DN_HEADS, 1)), const_spec((2 * DN_HEADS, 1)),
        const_spec((1, DN_HEAD_DIM)),
    ]
    scratch = [
        pltpu.VMEM((DN_HEADS, DN_HEAD_DIM, DN_HEAD_DIM), F32),
        pltpu.VMEM((tc + SUBLANES, DN_WIDTH), F32),
        pltpu.VMEM((tc, DN_WIDTH), F32),
        pltpu.VMEM((tc, DN_WIDTH), F32),
        pltpu.VMEM((tc, DN_WIDTH), F32),
        pltpu.VMEM((tc, DN_WIDTH), F32),
        pltpu.VMEM((2 * tc, DN_WIDTH), BF16),
        pltpu.VMEM((tc, DN_WIDTH), BF16),
        pltpu.VMEM((DN_HEADS, tc, DN_CHUNK), BF16),
        pltpu.VMEM((n_chunks, DN_HEADS, DN_HEAD_DIM), F32),
        pltpu.VMEM((tc, SMALL_W), F32),
        pltpu.VMEM((tc, SMALL_W), F32),
        pltpu.VMEM((n_chunks, 2 * DN_HEADS, DN_CHUNK), F32),
    ]
    return pl.pallas_call(
        _deltanet_kernel,
        grid=(b_sz, n_t),
        in_specs=in_specs,
        out_specs=pl.BlockSpec((None, tc, DN_WIDTH), lambda b, i: (b, i, 0)),
        out_shape=jax.ShapeDtypeStruct((b_sz, s_len, DN_WIDTH), BF16),
        scratch_shapes=scratch,
        compiler_params=pltpu.CompilerParams(
            dimension_semantics=("parallel", "arbitrary"), vmem_limit_bytes=VMEM_LIMIT),
        name="deltanet",
    )(main3d, main3d, main3d, main3d, main3d, main3d, main3d, small3d, small_t3,
      conv_w, conv_w, conv_w, alog_r, dtb_r, alog_c, dtb_c, norm_w)


def _moba_prep_kernel(q_ref, k_ref, v_ref, cos_ref, sin_ref, qat_ref, ka_ref, vt_ref, kmean_scr):
    blk = pl.program_id(1)
    n_rows = q_ref.shape[0]

    @pl.when(blk == 0)
    def _():
        kmean_scr[...] = jnp.zeros_like(kmean_scr)

    cos = cos_ref[...]
    sin = sin_ref[...]
    half = MOBA_HEAD_DIM // 2
    lane = lax.broadcasted_iota(jnp.int32, (n_rows, MOBA_HEAD_DIM), 1)
    onehot = (lane == blk).astype(BF16)
    blk_row = lax.broadcasted_iota(jnp.int32, (MOBA_HEAD_DIM, n_rows), 0)
    blk_row_f = blk_row.astype(F32)
    mean_row = lax.broadcasted_iota(jnp.int32, (MOBA_HEAD_DIM, MOBA_HEAD_DIM), 0)

    for h in range(MOBA_HEADS):
        cols = slice(h * MOBA_HEAD_DIM, (h + 1) * MOBA_HEAD_DIM)
        q = q_ref[:, cols]
        k = k_ref[:, cols]
        q = q * cos + pltpu.roll(q, half, 1) * sin
        k = k * cos + pltpu.roll(k, half, 1) * sin
        q_t = q.T

        kmean = kmean_scr[h]
        gate = jnp.dot(kmean, q_t, precision=lax.Precision.HIGHEST, preferred_element_type=F32)
        gate = jnp.where(blk_row < blk, gate, NEG_INF)
        sel = blk_row == blk
        for _ in range(MOBA_TOPK):
            mx = jnp.max(gate, axis=0, keepdims=True)
            first_idx = jnp.min(jnp.where(gate == mx, blk_row_f, float(MOBA_HEAD_DIM)),
                                axis=0, keepdims=True)
            hit = blk_row_f == first_idx
            sel = sel | (hit & (mx > NEG_INF))
            gate = jnp.where(hit, NEG_INF, gate)
        qat_ref[h * AUG_W:h * AUG_W + MOBA_HEAD_DIM, :] = q_t.astype(BF16)
        qat_ref[h * AUG_W + MOBA_HEAD_DIM:(h + 1) * AUG_W, :] = jnp.where(
            sel, 0.0, MASK_PENALTY).astype(BF16)
        ka_ref[:, h * AUG_W:h * AUG_W + MOBA_HEAD_DIM] = k.astype(BF16)
        ka_ref[:, h * AUG_W + MOBA_HEAD_DIM:(h + 1) * AUG_W] = onehot
        vt_ref[h] = v_ref[:, cols].T.astype(BF16)

        k_mean_row = jnp.mean(k, axis=0, keepdims=True)
        kmean_scr[h] = jnp.where(mean_row == blk, k_mean_row, kmean)


def _moba_prep(main3d, cos_t, sin_t):
    b_sz, s_len, _ = main3d.shape
    n_blk = s_len // MOBA_BLOCK
    assert n_blk <= MOBA_HEAD_DIM

    def tile_spec(cb):
        return pl.BlockSpec((None, MOBA_BLOCK, MOBA_WIDTH), lambda b, i, cb=cb: (b, i, cb))

    tab_spec = pl.BlockSpec((MOBA_BLOCK, MOBA_HEAD_DIM), lambda b, i: (i, 0))
    return pl.pallas_call(
        _moba_prep_kernel,
        grid=(b_sz, n_blk),
        in_specs=[tile_spec(4), tile_spec(5), tile_spec(6), tab_spec, tab_spec],
        out_specs=[
            pl.BlockSpec((None, MOBA_HEADS * AUG_W, MOBA_BLOCK), lambda b, i: (b, 0, i)),
            pl.BlockSpec((None, MOBA_BLOCK, MOBA_HEADS * AUG_W), lambda b, i: (b, i, 0)),
            pl.BlockSpec((None, MOBA_HEADS, None, MOBA_HEAD_DIM, MOBA_BLOCK),
                         lambda b, i: (b, 0, i, 0, 0)),
        ],
        out_shape=[
            jax.ShapeDtypeStruct((b_sz, MOBA_HEADS * AUG_W, s_len), BF16),
            jax.ShapeDtypeStruct((b_sz, s_len, MOBA_HEADS * AUG_W), BF16),
            jax.ShapeDtypeStruct((b_sz, MOBA_HEADS, n_blk, MOBA_HEAD_DIM, MOBA_BLOCK), BF16),
        ],
        scratch_shapes=[pltpu.VMEM((MOBA_HEADS, MOBA_HEAD_DIM, MOBA_HEAD_DIM), F32)],
        compiler_params=pltpu.CompilerParams(
            dimension_semantics=("parallel", "arbitrary"), vmem_limit_bytes=VMEM_LIMIT),
        name="moba_prep",
    )(main3d, main3d, main3d, cos_t, sin_t)


def _moba_attn_kernel(qat_ref, ka_ref, vt_ref, out_ref, *, heads_per_step):
    blk = pl.program_id(2)
    heads = range(heads_per_step)
    exp2_scale = (MOBA_HEAD_DIM ** -0.5) * float(np.log2(np.e))
    q_t = [qat_ref[h * AUG_W:(h + 1) * AUG_W, :] for h in heads]

    def scores(j):
        rows = pl.ds(pl.multiple_of(j * MOBA_BLOCK, MOBA_BLOCK), MOBA_BLOCK)
        return [jnp.dot(ka_ref[rows, h * AUG_W:(h + 1) * AUG_W], q_t[h],
                        preferred_element_type=F32) for h in heads]

    s = scores(blk)
    key = lax.broadcasted_iota(jnp.int32, s[0].shape, 0)
    qry = lax.broadcasted_iota(jnp.int32, s[0].shape, 1)
    s = [jnp.where(key <= qry, x, NEG_INF) for x in s]
    m = [jnp.max(x, axis=0, keepdims=True) for x in s]
    p = [jnp.exp2((s[h] - m[h]) * exp2_scale) for h in heads]
    l = [jnp.sum(x, axis=0, keepdims=True) for x in p]
    acc = [jnp.dot(vt_ref[h, blk], p[h].astype(BF16), preferred_element_type=F32) for h in heads]

    def body(j, carry):
        m, l, acc = carry
        s = scores(j)
        m_new = [jnp.maximum(m[h], jnp.max(s[h], axis=0, keepdims=True)) for h in heads]
        alpha = [jnp.exp2((m[h] - m_new[h]) * exp2_scale) for h in heads]
        p = [jnp.exp2((s[h] - m_new[h]) * exp2_scale) for h in heads]
        l = [alpha[h] * l[h] + jnp.sum(p[h], axis=0, keepdims=True) for h in heads]
        acc = [alpha[h] * acc[h] + jnp.dot(vt_ref[h, j], p[h].astype(BF16),
                                           preferred_element_type=F32) for h in heads]
        return m_new, l, acc

    m, l, acc = lax.fori_loop(0, blk, body, (m, l, acc))
    for h in heads:
        out_ref[:, h * MOBA_HEAD_DIM:(h + 1) * MOBA_HEAD_DIM] = (acc[h] / l[h]).T.astype(out_ref.dtype)


def _moba_attn(qat, ka, vt, heads_per_step):
    b_sz, s_len, _ = ka.shape
    n_blk = s_len // MOBA_BLOCK
    hb = heads_per_step
    resident = pl.Buffered(1)
    return pl.pallas_call(
        functools.partial(_moba_attn_kernel, heads_per_step=hb),
        grid=(b_sz, MOBA_HEADS // hb, n_blk),
        in_specs=[
            pl.BlockSpec((None, hb * AUG_W, MOBA_BLOCK), lambda b, g, i: (b, g, i)),
            pl.BlockSpec((None, s_len, hb * AUG_W), lambda b, g, i: (b, 0, g),
                         pipeline_mode=resident),
            pl.BlockSpec((None, hb, n_blk, MOBA_HEAD_DIM, MOBA_BLOCK),
                         lambda b, g, i: (b, g, 0, 0, 0), pipeline_mode=resident),
        ],
        out_specs=pl.BlockSpec((None, MOBA_BLOCK, hb * MOBA_HEAD_DIM), lambda b, g, i: (b, i, g)),
        out_shape=jax.ShapeDtypeStruct((b_sz, s_len, MOBA_WIDTH), BF16),
        compiler_params=pltpu.CompilerParams(
            dimension_semantics=("parallel", "parallel", "arbitrary"),
            vmem_limit_bytes=VMEM_LIMIT),
        name="moba_attn",
    )(qat, ka, vt)


def _out_proj_kernel(x_ref, odn_ref, omb_ref, gdn_ref, gmb_ref, wdn_ref, wmb_ref, wo_ref, h_ref):
    y_dn = jnp.dot(odn_ref[...], wdn_ref[...], preferred_element_type=F32)
    y_mb = jnp.dot(omb_ref[...], wmb_ref[...], preferred_element_type=F32)
    merged = _sigmoid(gdn_ref[...]) * y_dn + _sigmoid(gmb_ref[...]) * y_mb
    h_ref[...] = x_ref[...] + jnp.dot(merged.astype(BF16), wo_ref[...], preferred_element_type=F32)


def _out_proj(x2d, o_dn, o_mb, main2d, w_dn, w_mb, w_o, tm):
    t, d = x2d.shape
    row_spec = pl.BlockSpec((tm, d), lambda i: (i, 0))
    w_spec = pl.BlockSpec((d, d), lambda i: (0, 0))
    return pl.pallas_call(
        _out_proj_kernel,
        grid=(t // tm,),
        in_specs=[row_spec, row_spec, row_spec,
                  pl.BlockSpec((tm, d), lambda i: (i, 7)),
                  pl.BlockSpec((tm, d), lambda i: (i, 8)),
                  w_spec, w_spec, w_spec],
        out_specs=row_spec,
        out_shape=jax.ShapeDtypeStruct((t, d), F32),
        compiler_params=pltpu.CompilerParams(
            dimension_semantics=("parallel",), vmem_limit_bytes=VMEM_LIMIT),
        name="out_proj",
    )(x2d, o_dn, o_mb, main2d, main2d, w_dn, w_mb, w_o)


def _conv_ffn_kernel(h_ref, nw_ref, wup_ref, cw_ref, cb_ref, wdown_ref, fw_ref, out_ref,
                     u_scr, *, tiles_per_seq, n_split):
    tm = h_ref.shape[0]
    seq_start = (pl.program_id(0) % tiles_per_seq) == 0
    h = h_ref[...]
    var = jnp.mean(h * h, axis=-1, keepdims=True)
    n = ((h * lax.rsqrt(var + NORM_EPS)) * nw_ref[...]).astype(BF16)

    width = D_FF // n_split
    base = SUBLANES - (FFN_CONV - 1)
    acc = jnp.zeros((tm, D_MODEL), F32)
    for part in range(n_split):
        halves = []
        for half in range(2):
            c0 = half * D_FF + part * width
            slot = 2 * part + half
            @pl.when(seq_start)
            def _(slot=slot):
                u_scr[slot, 0:SUBLANES, :] = jnp.zeros((SUBLANES, width), F32)

            @pl.when(jnp.logical_not(seq_start))
            def _(slot=slot):
                u_scr[slot, 0:SUBLANES, :] = u_scr[slot, tm:tm + SUBLANES, :]

            u_scr[slot, SUBLANES:SUBLANES + tm, :] = jnp.dot(
                n, wup_ref[:, c0:c0 + width], preferred_element_type=F32)
            y = (u_scr[slot, SUBLANES:SUBLANES + tm, :]
                 * cw_ref[FFN_CONV - 1:FFN_CONV, c0:c0 + width])
            for j in range(FFN_CONV - 1):
                y = y + u_scr[slot, base + j:base + j + tm, :] * cw_ref[j:j + 1, c0:c0 + width]
            halves.append(y + cb_ref[:, c0:c0 + width])
        act = (_silu(halves[0]) * halves[1]).astype(BF16)
        acc = acc + jnp.dot(act, wdown_ref[part * width:(part + 1) * width, :],
                            preferred_element_type=F32)
    h2 = h + acc
    var2 = jnp.mean(h2 * h2, axis=-1, keepdims=True)
    out_ref[...] = (h2 * lax.rsqrt(var2 + NORM_EPS)) * fw_ref[...]


def _conv_ffn(h2d, norm_w, w_up, conv_w, conv_b, w_down, final_w, tm, s_len, n_split):
    t, d = h2d.shape
    width = D_FF // n_split
    kern = functools.partial(_conv_ffn_kernel, tiles_per_seq=s_len // tm, n_split=n_split)

    def const_spec(shape):
        return pl.BlockSpec(shape, lambda i: (0, 0))

    return pl.pallas_call(
        kern,
        grid=(t // tm,),
        in_specs=[
            pl.BlockSpec((tm, d), lambda i: (i, 0)),
            const_spec((1, d)),
            const_spec((d, 2 * D_FF)),
            const_spec((FFN_CONV, 2 * D_FF)),
            const_spec((1, 2 * D_FF)),
            const_spec((D_FF, d)),
            const_spec((1, d)),
        ],
        out_specs=pl.BlockSpec((tm, d), lambda i: (i, 0)),
        out_shape=jax.ShapeDtypeStruct((t, d), F32),
        scratch_shapes=[pltpu.VMEM((2 * n_split, tm + SUBLANES, width), F32)],
        compiler_params=pltpu.CompilerParams(
            dimension_semantics=("arbitrary",), vmem_limit_bytes=VMEM_LIMIT),
        name="conv_ffn",
    )(h2d, norm_w, w_up, conv_w, conv_b, w_down, final_w)


def _rope_tables(s_len):
    half = MOBA_HEAD_DIM // 2
    inv_freq = jnp.power(ROPE_THETA, -jnp.arange(half, dtype=F32) / half)
    ang = jnp.arange(s_len).astype(F32)[:, None] * inv_freq[None, :]
    cos, sin = jnp.cos(ang), jnp.sin(ang)
    return jnp.concatenate([cos, cos], axis=-1), jnp.concatenate([-sin, sin], axis=-1)


def _pick_tile(n, want):
    t = min(n, want)
    while n % t:
        t //= 2
    return t


def _layer(x, attn_norm_w, w_in, dn_conv_w, dn_a_log, dn_dt_bias, dn_norm_w, w_dn_out,
           w_moba_out, w_o, ffn_norm_w, w_up, ffn_conv_w, ffn_conv_b, w_down, final_norm_w):
    b_sz, s_len, d = x.shape
    t = b_sz * s_len
    x2d = x.reshape(t, d)

    small_lo = 4 * DN_WIDTH
    small_hi = small_lo + 2 * DN_HEADS
    w_main = jnp.concatenate([w_in[:, :small_lo], w_in[:, small_hi:]], axis=1).astype(BF16)
    w_small_cols = w_in[:, small_lo:small_hi]
    w_small = jnp.pad(w_small_cols, ((0, 0), (0, SMALL_W - 2 * DN_HEADS))).astype(BF16)
    w_small_t = w_small_cols.T.astype(BF16)

    main, small, small_t = _in_proj(x2d, attn_norm_w.reshape(1, d), w_main, w_small, w_small_t,
                                    _pick_tile(t, 1024))
    main3d = main.reshape(b_sz, s_len, main.shape[1])
    small3d = small.reshape(b_sz, s_len, SMALL_W)

    pad_r = ((0, 0), (DN_HEADS, SMALL_W - 2 * DN_HEADS))
    alog_r = jnp.pad(dn_a_log.reshape(1, DN_HEADS), pad_r)
    dtb_r = jnp.pad(dn_dt_bias.reshape(1, DN_HEADS), pad_r)
    alog_c = jnp.pad(dn_a_log.reshape(DN_HEADS, 1), ((DN_HEADS, 0), (0, 0)))
    dtb_c = jnp.pad(dn_dt_bias.reshape(DN_HEADS, 1), ((DN_HEADS, 0), (0, 0)))
    o_dn = _deltanet(main3d, small3d, small_t, dn_conv_w, alog_r, dtb_r, alog_c, dtb_c,
                     dn_norm_w.reshape(1, DN_HEAD_DIM), _pick_tile(s_len, 256))

    cos_t, sin_t = _rope_tables(s_len)
    qat, ka, vt = _moba_prep(main3d, cos_t, sin_t)
    o_mb = _moba_attn(qat, ka, vt, 4)

    h = _out_proj(x2d, o_dn.reshape(t, DN_WIDTH), o_mb.reshape(t, MOBA_WIDTH), main,
                  w_dn_out.astype(BF16), w_moba_out.astype(BF16), w_o.astype(BF16),
                  _pick_tile(t, 512))

    tm_ffn = _pick_tile(s_len, 256)
    out = _conv_ffn(h, ffn_norm_w.reshape(1, d), w_up.astype(BF16), ffn_conv_w,
                    ffn_conv_b.reshape(1, 2 * D_FF), w_down.astype(BF16),
                    final_norm_w.reshape(1, d), tm_ffn, s_len, 2)
    return out.reshape(b_sz, s_len, d)


def kernel(x, attn_norm_w, w_in, dn_conv_w, dn_A_log, dn_dt_bias, dn_norm_w, w_dn_out, w_moba_out,
           w_o, ffn_norm_w, w_up, ffn_conv_w, ffn_conv_b, w_down, final_norm_w):
    depth = w_in.shape[0]
    assert depth == 1, "the final RMSNorm is fused into the layer's conv_ffn call"
    return _layer(x, attn_norm_w[0], w_in[0], dn_conv_w[0], dn_A_log[0], dn_dt_bias[0],
                  dn_norm_w[0], w_dn_out[0], w_moba_out[0], w_o[0], ffn_norm_w[0], w_up[0],
                  ffn_conv_w[0], ffn_conv_b[0], w_down[0], final_norm_w)
```

```python
import functools

import jax
import jax.numpy as jnp
import numpy as np
from jax import lax
from jax.experimental import pallas as pl
from jax.experimental.pallas import tpu as pltpu

D_MODEL = 1024
DN_HEADS = 8
DN_HEAD_DIM = 128
DN_WIDTH = DN_HEADS * DN_HEAD_DIM
DN_CONV = 4
DN_CHUNK = 64
MOBA_HEADS = 8
MOBA_HEAD_DIM = 128
MOBA_WIDTH = MOBA_HEADS * MOBA_HEAD_DIM
MOBA_BLOCK = 256
MOBA_TOPK = 3
ROPE_THETA = 10000.0
D_FF = 2816
FFN_CONV = 3
NORM_EPS = 1e-6

SMALL_W = 128
SUBLANES = 8
AUG_W = 2 * MOBA_HEAD_DIM
MASK_PENALTY = -(2.0 ** 100)
BF16_ROWS = 16
VT_ROWS = MOBA_HEAD_DIM + BF16_ROWS
MOBA_EXP2_SCALE = (MOBA_HEAD_DIM ** -0.5) * float(np.log2(np.e))
VMEM_LIMIT = 52 * 1024 * 1024

F32 = jnp.float32
BF16 = jnp.bfloat16
NEG_INF = float("-inf")


def _dot(a, b):
    return jnp.dot(a.astype(BF16), b.astype(BF16), preferred_element_type=F32)


def _dot_nt(a, b):
    return lax.dot_general(a.astype(BF16), b.astype(BF16), (((1,), (1,)), ((), ())),
                           preferred_element_type=F32)


def _dot_tn(a, b):
    return lax.dot_general(a.astype(BF16), b.astype(BF16), (((0,), (0,)), ((), ())),
                           preferred_element_type=F32)


def _sigmoid(x):
    return 1.0 / (1.0 + jnp.exp(-x))


def _silu(x):
    return x * _sigmoid(x)


def _softplus(x):
    return jnp.maximum(x, 0.0) + jnp.log1p(jnp.exp(-jnp.abs(x)))


def _in_proj_kernel(x_ref, nw_ref, wm_ref, ws_ref, wst_ref, main_ref, small_ref, smallt_ref, n_scr):
    @pl.when(pl.program_id(1) == 0)
    def _():
        x = x_ref[...]
        var = jnp.mean(x * x, axis=-1, keepdims=True)
        n = ((x * lax.rsqrt(var + NORM_EPS)) * nw_ref[...]).astype(BF16)
        n_scr[...] = n
        small_ref[...] = jnp.dot(n, ws_ref[...], preferred_element_type=F32)
        smallt_ref[...] = lax.dot_general(wst_ref[...], n, (((1,), (1,)), ((), ())),
                                          preferred_element_type=F32)

    main_ref[...] = jnp.dot(n_scr[...], wm_ref[...], preferred_element_type=F32)


def _in_proj(x2d, norm_w, w_main, w_small, w_small_t, tm):
    t, d = x2d.shape
    n_main = w_main.shape[1]
    tn = D_MODEL
    grid = (t // tm, n_main // tn)
    return pl.pallas_call(
        _in_proj_kernel,
        grid=grid,
        in_specs=[
            pl.BlockSpec((tm, d), lambda i, j: (i, 0)),
            pl.BlockSpec((1, d), lambda i, j: (0, 0)),
            pl.BlockSpec((d, tn), lambda i, j: (0, j)),
            pl.BlockSpec((d, SMALL_W), lambda i, j: (0, 0)),
            pl.BlockSpec((2 * DN_HEADS, d), lambda i, j: (0, 0)),
        ],
        out_specs=[
            pl.BlockSpec((tm, tn), lambda i, j: (i, j)),
            pl.BlockSpec((tm, SMALL_W), lambda i, j: (i, 0)),
            pl.BlockSpec((2 * DN_HEADS, tm), lambda i, j: (0, i)),
        ],
        out_shape=[
            jax.ShapeDtypeStruct((t, n_main), F32),
            jax.ShapeDtypeStruct((t, SMALL_W), F32),
            jax.ShapeDtypeStruct((2 * DN_HEADS, t), F32),
        ],
        scratch_shapes=[pltpu.VMEM((tm, d), BF16)],
        compiler_params=pltpu.CompilerParams(
            dimension_semantics=("parallel", "arbitrary"), vmem_limit_bytes=VMEM_LIMIT),
        name="in_proj",
    )(x2d, norm_w, w_main, w_small, w_small_t)


def _chunk_cumsum(x, axis):
    idx = lax.broadcasted_iota(jnp.int32, x.shape, axis) % DN_CHUNK
    shift = 1
    while shift < DN_CHUNK:
        x = x + jnp.where(idx >= shift, pltpu.roll(x, shift, axis), 0.0)
        shift *= 2
    return x


def _unit_lower_inverse(mats, row, col):
    c = mats[0].shape[0]
    eye = (row == col).astype(F32)
    blk = 8
    diag_blk = (row // blk) == (col // blk)
    n = [jnp.where(diag_blk, -a, 0.0) for a in mats]
    n2 = [_dot(x, x) for x in n]
    n4 = [_dot(x, x) for x in n2]
    t = [_dot(eye + x, eye + y) for x, y in zip(n, n2)]
    t = [_dot(x, eye + y) for x, y in zip(t, n4)]
    while blk < c:
        off = ((row // (2 * blk)) == (col // (2 * blk))) & ((row // blk) != (col // blk))
        a_t = [_dot(jnp.where(off, a, 0.0), x) for a, x in zip(mats, t)]
        t = [x - _dot(x, y) for x, y in zip(t, a_t)]
        blk *= 2
    return t


def _deltanet_kernel(q_ref, k_ref, v_ref, qh_ref, kh_ref, vh_ref, z_ref, small_ref, smallt_ref,
                     cwq_ref, cwk_ref, cwv_ref, alog_r_ref, dtb_r_ref, alog_c_ref, dtb_c_ref,
                     normw_ref, out_ref,
                     state_scr, conv_scr, qc_scr, kc_scr, vc_scr, u_scr, wq_scr, kd_scr,
                     qk_scr, gl_scr, bcol_scr, gcol_scr, grow_scr):
    tc = q_ref.shape[0]
    n_chunks = tc // DN_CHUNK
    first = pl.program_id(1) == 0
    heads = range(DN_HEADS)
    cols = [slice(h * DN_HEAD_DIM, (h + 1) * DN_HEAD_DIM) for h in heads]

    @pl.when(first)
    def _():
        state_scr[...] = jnp.zeros_like(state_scr)

    def conv_silu(x_ref, halo_ref, cw_ref):
        halo = jnp.where(first, 0.0, halo_ref[...])
        conv_scr[0:SUBLANES, :] = halo
        conv_scr[SUBLANES:SUBLANES + tc, :] = x_ref[...]
        base = SUBLANES - (DN_CONV - 1)
        y = conv_scr[SUBLANES:SUBLANES + tc, :] * cw_ref[DN_CONV - 1:DN_CONV, :]
        for j in range(DN_CONV - 1):
            y = y + conv_scr[base + j:base + j + tc, :] * cw_ref[j:j + 1, :]
        return _silu(y)

    def l2norm_heads(y, dst, scale):
        for h in range(DN_HEADS):
            seg = y[:, h * DN_HEAD_DIM:(h + 1) * DN_HEAD_DIM]
            ss = jnp.sum(seg * seg, axis=-1, keepdims=True)
            seg = seg * lax.rsqrt(ss + NORM_EPS)
            if scale is not None:
                seg = seg * scale
            dst[:, h * DN_HEAD_DIM:(h + 1) * DN_HEAD_DIM] = seg

    l2norm_heads(conv_silu(q_ref, qh_ref, cwq_ref), qc_scr, DN_HEAD_DIM ** -0.5)
    l2norm_heads(conv_silu(k_ref, kh_ref, cwk_ref), kc_scr, None)
    vc_scr[...] = conv_silu(v_ref, vh_ref, cwv_ref)

    small = small_ref[...]
    bcol_scr[...] = _sigmoid(small)
    gcol_scr[...] = _chunk_cumsum(-jnp.exp(alog_r_ref[...]) * _softplus(small + dtb_r_ref[...]), 0)
    small_t = smallt_ref[...]
    g_t = _chunk_cumsum(-jnp.exp(alog_c_ref[...]) * _softplus(small_t + dtb_c_ref[...]), 1)
    for c in range(n_chunks):
        grow_scr[c] = g_t[:, c * DN_CHUNK:(c + 1) * DN_CHUNK]

    row = lax.broadcasted_iota(jnp.int32, (DN_CHUNK, DN_CHUNK), 0)
    col = lax.broadcasted_iota(jnp.int32, (DN_CHUNK, DN_CHUNK), 1)
    lower_incl = row >= col
    lower_strict = row > col

    def prep(c, carry):
        rows = pl.ds(pl.multiple_of(c * DN_CHUNK, DN_CHUNK), DN_CHUNK)
        w_rows = pl.ds(pl.multiple_of(c * 2 * DN_CHUNK, 2 * DN_CHUNK), DN_CHUNK)
        qd_rows = pl.ds(pl.multiple_of(c * 2 * DN_CHUNK + DN_CHUNK, DN_CHUNK), DN_CHUNK)
        g_rows = grow_scr[c]
        q = [qc_scr[rows, cols[h]] for h in heads]
        k = [kc_scr[rows, cols[h]] for h in heads]
        beta = [bcol_scr[rows, h:h + 1] for h in heads]
        gc_b = [jnp.broadcast_to(gcol_scr[rows, DN_HEADS + h:DN_HEADS + h + 1],
                                 (DN_CHUNK, DN_HEAD_DIM)) for h in heads]
        eg_b = [jnp.exp(x) for x in gc_b]
        decay = [jnp.exp(jnp.where(lower_incl,
                                   gc_b[h][:, :DN_CHUNK] - g_rows[DN_HEADS + h:DN_HEADS + h + 1, :],
                                   NEG_INF)) for h in heads]
        kb = [k[h] * beta[h] for h in heads]
        kq = [_dot_nt(jnp.concatenate([kb[h], q[h]], axis=0), k[h]) for h in heads]
        strict = [kq[h][:DN_CHUNK] * jnp.where(lower_strict, decay[h], 0.0) for h in heads]
        for h in heads:
            qk_scr[h, rows, :] = (kq[h][DN_CHUNK:] * decay[h]).astype(BF16)
        t_mat = _unit_lower_inverse(strict, row, col)
        uw = [_dot(t_mat[h], jnp.concatenate([vc_scr[rows, cols[h]] * beta[h], kb[h] * eg_b[h]],
                                             axis=1)) for h in heads]
        for h in heads:
            u_scr[rows, cols[h]] = uw[h][:, :DN_HEAD_DIM]
            wq_scr[w_rows, cols[h]] = uw[h][:, DN_HEAD_DIM:].astype(BF16)
            wq_scr[qd_rows, cols[h]] = (q[h] * eg_b[h]).astype(BF16)
            g_last_b = jnp.broadcast_to(gc_b[h][DN_CHUNK - 1:DN_CHUNK, :], (DN_CHUNK, DN_HEAD_DIM))
            kd_scr[rows, cols[h]] = (k[h] * jnp.exp(g_last_b - gc_b[h])).astype(BF16)
            gl_scr[c, h:h + 1, :] = eg_b[h][DN_CHUNK - 1:DN_CHUNK, :]
        return carry

    lax.fori_loop(0, n_chunks, prep, 0)

    normw = normw_ref[...]

    def scan(c, carry):
        rows = pl.ds(pl.multiple_of(c * DN_CHUNK, DN_CHUNK), DN_CHUNK)
        wq_rows = pl.ds(pl.multiple_of(c * 2 * DN_CHUNK, 2 * DN_CHUNK), 2 * DN_CHUNK)
        gl_all = gl_scr[c]
        state = [state_scr[h] for h in heads]
        state_b = [x.astype(BF16) for x in state]
        ws_qs = [jnp.dot(wq_scr[wq_rows, cols[h]], state_b[h], preferred_element_type=F32)
                 for h in heads]
        v_new_b = [(u_scr[rows, cols[h]] - ws_qs[h][:DN_CHUNK]).astype(BF16) for h in heads]
        intra = [jnp.dot(qk_scr[h, rows, :], v_new_b[h], preferred_element_type=F32) for h in heads]
        d_state = [lax.dot_general(kd_scr[rows, cols[h]], v_new_b[h], (((0,), (0,)), ((), ())),
                                   preferred_element_type=F32) for h in heads]
        for h in heads:
            state_scr[h] = state[h] * gl_all[h:h + 1, :] + d_state[h]
            o = ws_qs[h][DN_CHUNK:] + intra[h]
            var = jnp.mean(o * o, axis=-1, keepdims=True)
            o = (o * lax.rsqrt(var + NORM_EPS)) * normw
            out_ref[rows, cols[h]] = (o * _silu(z_ref[rows, cols[h]])).astype(out_ref.dtype)
        return carry

    lax.fori_loop(0, n_chunks, scan, 0)


def _deltanet(main3d, small3d, small_t3, conv_w, alog_r, dtb_r, alog_c, dtb_c, norm_w, tc):
    b_sz, s_len, _ = main3d.shape
    n_t = s_len // tc
    n_chunks = tc // DN_CHUNK
    halo_blocks = tc // SUBLANES

    def tile_spec(cb):
        return pl.BlockSpec((None, tc, DN_WIDTH), lambda b, i, cb=cb: (b, i, cb))

    def halo_spec(cb):
        return pl.BlockSpec((None, SUBLANES, DN_WIDTH),
                            lambda b, i, cb=cb: (b, jnp.maximum(i * halo_blocks - 1, 0), cb))

    def const_spec(shape):
        return pl.BlockSpec(shape, lambda b, i: tuple(0 for _ in shape))

    in_specs = [
        tile_spec(0), tile_spec(1), tile_spec(2),
        halo_spec(0), halo_spec(1), halo_spec(2),
        tile_spec(3),
        pl.BlockSpec((None, tc, SMALL_W), lambda b, i: (b, i, 0)),
        pl.BlockSpec((2 * DN_HEADS, tc), lambda b, i: (0, b * n_t + i)),
        pl.BlockSpec((DN_CONV, DN_WIDTH), lambda b, i: (0, 0)),
        pl.BlockSpec((DN_CONV, DN_WIDTH), lambda b, i: (0, 1)),
        pl.BlockSpec((DN_CONV, DN_WIDTH), lambda b, i: (0, 2)),
        const_spec((1, SMALL_W)), const_spec((1, SMALL_W)),
        const_spec((2 * DN_HEADS, 1)), const_spec((2 * DN_HEADS, 1)),
        const_spec((1, DN_HEAD_DIM)),
    ]
    scratch = [
        pltpu.VMEM((DN_HEADS, DN_HEAD_DIM, DN_HEAD_DIM), F32),
        pltpu.VMEM((tc + SUBLANES, DN_WIDTH), F32),
        pltpu.VMEM((tc, DN_WIDTH), F32),
        pltpu.VMEM((tc, DN_WIDTH), F32),
        pltpu.VMEM((tc, DN_WIDTH), F32),
        pltpu.VMEM((tc, DN_WIDTH), F32),
        pltpu.VMEM((2 * tc, DN_WIDTH), BF16),
        pltpu.VMEM((tc, DN_WIDTH), BF16),
        pltpu.VMEM((DN_HEADS, tc, DN_CHUNK), BF16),
        pltpu.VMEM((n_chunks, DN_HEADS, DN_HEAD_DIM), F32),
        pltpu.VMEM((tc, SMALL_W), F32),
        pltpu.VMEM((tc, SMALL_W), F32),
        pltpu.VMEM((n_chunks, 2 * DN_HEADS, DN_CHUNK), F32),
    ]
    return pl.pallas_call(
        _deltanet_kernel,
        grid=(b_sz, n_t),
        in_specs=in_specs,
        out_specs=pl.BlockSpec((None, tc, DN_WIDTH), lambda b, i: (b, i, 0)),
        out_shape=jax.ShapeDtypeStruct((b_sz, s_len, DN_WIDTH), BF16),
        scratch_shapes=scratch,
        compiler_params=pltpu.CompilerParams(
            dimension_semantics=("parallel", "arbitrary"), vmem_limit_bytes=VMEM_LIMIT),
        name="deltanet",
    )(main3d, main3d, main3d, main3d, main3d, main3d, main3d, small3d, small_t3,
      conv_w, conv_w, conv_w, alog_r, dtb_r, alog_c, dtb_c, norm_w)


def _moba_prep_kernel(q_ref, k_ref, v_ref, cos_ref, sin_ref, qat_ref, ka_ref, vt_ref, kmean_scr):
    blk = pl.program_id(1)
    n_rows = q_ref.shape[0]

    @pl.when(blk == 0)
    def _():
        kmean_scr[...] = jnp.zeros_like(kmean_scr)

    cos = cos_ref[...]
    sin = sin_ref[...]
    half = MOBA_HEAD_DIM // 2
    lane = lax.broadcasted_iota(jnp.int32, (n_rows, MOBA_HEAD_DIM), 1)
    onehot = (lane == blk).astype(BF16)
    blk_row = lax.broadcasted_iota(jnp.int32, (MOBA_HEAD_DIM, n_rows), 0)
    blk_row_f = blk_row.astype(F32)
    mean_row = lax.broadcasted_iota(jnp.int32, (MOBA_HEAD_DIM, MOBA_HEAD_DIM), 0)
    ones_rows = (lax.broadcasted_iota(jnp.int32, (BF16_ROWS, n_rows), 0) == 0).astype(BF16)

    for h in range(MOBA_HEADS):
        cols = slice(h * MOBA_HEAD_DIM, (h + 1) * MOBA_HEAD_DIM)
        q = q_ref[:, cols]
        k = k_ref[:, cols]
        q = q * cos + pltpu.roll(q, half, 1) * sin
        k = k * cos + pltpu.roll(k, half, 1) * sin
        q_t = q.T

        kmean = kmean_scr[h]
        gate = jnp.dot(kmean, q_t, precision=lax.Precision.HIGHEST, preferred_element_type=F32)
        gate = jnp.where(blk_row < blk, gate, NEG_INF)
        sel = blk_row == blk
        for _ in range(MOBA_TOPK):
            mx = jnp.max(gate, axis=0, keepdims=True)
            first_idx = jnp.min(jnp.where(gate == mx, blk_row_f, float(MOBA_HEAD_DIM)),
                                axis=0, keepdims=True)
            hit = blk_row_f == first_idx
            sel = sel | (hit & (mx > NEG_INF))
            gate = jnp.where(hit, NEG_INF, gate)
        qat_ref[h * AUG_W:h * AUG_W + MOBA_HEAD_DIM, :] = (q_t * MOBA_EXP2_SCALE).astype(BF16)
        qat_ref[h * AUG_W + MOBA_HEAD_DIM:(h + 1) * AUG_W, :] = jnp.where(
            sel, 0.0, MASK_PENALTY).astype(BF16)
        ka_ref[:, h * AUG_W:h * AUG_W + MOBA_HEAD_DIM] = k.astype(BF16)
        ka_ref[:, h * AUG_W + MOBA_HEAD_DIM:(h + 1) * AUG_W] = onehot
        vt_ref[h, 0:MOBA_HEAD_DIM, :] = v_ref[:, cols].T.astype(BF16)
        vt_ref[h, MOBA_HEAD_DIM:VT_ROWS, :] = ones_rows

        k_mean_row = jnp.mean(k, axis=0, keepdims=True)
        kmean_scr[h] = jnp.where(mean_row == blk, k_mean_row, kmean)


def _moba_prep(main3d, cos_t, sin_t):
    b_sz, s_len, _ = main3d.shape
    n_blk = s_len // MOBA_BLOCK
    assert n_blk <= MOBA_HEAD_DIM

    def tile_spec(cb):
        return pl.BlockSpec((None, MOBA_BLOCK, MOBA_WIDTH), lambda b, i, cb=cb: (b, i, cb))

    tab_spec = pl.BlockSpec((MOBA_BLOCK, MOBA_HEAD_DIM), lambda b, i: (i, 0))
    return pl.pallas_call(
        _moba_prep_kernel,
        grid=(b_sz, n_blk),
        in_specs=[tile_spec(4), tile_spec(5), tile_spec(6), tab_spec, tab_spec],
        out_specs=[
            pl.BlockSpec((None, MOBA_HEADS * AUG_W, MOBA_BLOCK), lambda b, i: (b, 0, i)),
            pl.BlockSpec((None, MOBA_BLOCK, MOBA_HEADS * AUG_W), lambda b, i: (b, i, 0)),
            pl.BlockSpec((None, MOBA_HEADS, None, VT_ROWS, MOBA_BLOCK),
                         lambda b, i: (b, 0, i, 0, 0)),
        ],
        out_shape=[
            jax.ShapeDtypeStruct((b_sz, MOBA_HEADS * AUG_W, s_len), BF16),
            jax.ShapeDtypeStruct((b_sz, s_len, MOBA_HEADS * AUG_W), BF16),
            jax.ShapeDtypeStruct((b_sz, MOBA_HEADS, n_blk, VT_ROWS, MOBA_BLOCK), BF16),
        ],
        scratch_shapes=[pltpu.VMEM((MOBA_HEADS, MOBA_HEAD_DIM, MOBA_HEAD_DIM), F32)],
        compiler_params=pltpu.CompilerParams(
            dimension_semantics=("parallel", "arbitrary"), vmem_limit_bytes=VMEM_LIMIT),
        name="moba_prep",
    )(main3d, main3d, main3d, cos_t, sin_t)


def _moba_attn_kernel(qat_ref, ka_ref, vt_ref, out_ref, s_scr, p_scr, acc_scr, *, heads_per_step):
    n_q = qat_ref.shape[1]
    blocks_per_tile = n_q // MOBA_BLOCK
    first_blk = pl.program_id(2) * blocks_per_tile
    heads = range(heads_per_step)
    q_t = [qat_ref[h * AUG_W:(h + 1) * AUG_W, :] for h in heads]

    def scores(j):
        rows = pl.ds(pl.multiple_of(j * MOBA_BLOCK, MOBA_BLOCK), MOBA_BLOCK)
        return [jnp.dot(ka_ref[rows, h * AUG_W:(h + 1) * AUG_W], q_t[h],
                        preferred_element_type=F32) for h in heads]

    key = lax.broadcasted_iota(jnp.int32, (MOBA_BLOCK, n_q), 0)
    qry = lax.broadcasted_iota(jnp.int32, (MOBA_BLOCK, n_q), 1)
    m = None
    for r in range(blocks_per_tile):
        s = [jnp.where(key + r * MOBA_BLOCK <= qry, x, NEG_INF) for x in scores(first_blk + r)]
        if r == 0:
            m = [jnp.max(x, axis=0, keepdims=True) for x in s]
            for h in heads:
                acc_scr[h] = jnp.dot(vt_ref[h, first_blk], jnp.exp2(s[h] - m[h]).astype(BF16),
                                     preferred_element_type=F32)
        else:
            m_new = [jnp.maximum(m[h], jnp.max(s[h], axis=0, keepdims=True)) for h in heads]
            for h in heads:
                acc_scr[h] = (jnp.exp2(m[h] - m_new[h]) * acc_scr[h]
                              + jnp.dot(vt_ref[h, first_blk + r],
                                        jnp.exp2(s[h] - m_new[h]).astype(BF16),
                                        preferred_element_type=F32))
            m = m_new

    s_first = scores(0)
    for h in heads:
        s_scr[0, h] = s_first[h]
        p_scr[0, h] = jnp.zeros(p_scr.shape[2:], BF16)
    alpha = [jnp.ones_like(x) for x in m]

    def half_trip(j, cur, nxt, m, alpha):
        s_next = scores(jnp.minimum(j + 1, first_blk - 1))
        j_prev = jnp.maximum(j - 1, 0)
        for h in heads:
            acc_scr[h] = alpha[h] * acc_scr[h] + jnp.dot(vt_ref[h, j_prev], p_scr[cur, h],
                                                         preferred_element_type=F32)
        s_cur = [s_scr[cur, h] for h in heads]
        m_new = [jnp.maximum(m[h], jnp.max(s_cur[h], axis=0, keepdims=True)) for h in heads]
        for h in heads:
            p_scr[nxt, h] = jnp.exp2(s_cur[h] - m_new[h]).astype(BF16)
            s_scr[nxt, h] = s_next[h]
        alpha = [jnp.exp2(m[h] - m_new[h]) for h in heads]
        return m_new, alpha

    def body(jj, carry):
        m, alpha = carry
        m, alpha = half_trip(2 * jj, 0, 1, m, alpha)
        return half_trip(2 * jj + 1, 1, 0, m, alpha)

    assert blocks_per_tile % 2 == 0
    m, alpha = lax.fori_loop(0, first_blk // 2, body, (m, alpha))
    j_last = jnp.maximum(first_blk - 1, 0)
    for h in heads:
        a = alpha[h] * acc_scr[h] + jnp.dot(vt_ref[h, j_last], p_scr[0, h],
                                            preferred_element_type=F32)
        o_t = a[:MOBA_HEAD_DIM] / a[MOBA_HEAD_DIM:MOBA_HEAD_DIM + 1]
        out_ref[:, h * MOBA_HEAD_DIM:(h + 1) * MOBA_HEAD_DIM] = o_t.T.astype(out_ref.dtype)


def _moba_attn(qat, ka, vt, heads_per_step, q_tile):
    b_sz, s_len, _ = ka.shape
    n_blk = s_len // MOBA_BLOCK
    hb = heads_per_step
    resident = pl.Buffered(1)
    return pl.pallas_call(
        functools.partial(_moba_attn_kernel, heads_per_step=hb),
        grid=(b_sz, MOBA_HEADS // hb, s_len // q_tile),
        in_specs=[
            pl.BlockSpec((None, hb * AUG_W, q_tile), lambda b, g, i: (b, g, i)),
            pl.BlockSpec((None, s_len, hb * AUG_W), lambda b, g, i: (b, 0, g),
                         pipeline_mode=resident),
            pl.BlockSpec((None, hb, n_blk, VT_ROWS, MOBA_BLOCK),
                         lambda b, g, i: (b, g, 0, 0, 0), pipeline_mode=resident),
        ],
        out_specs=pl.BlockSpec((None, q_tile, hb * MOBA_HEAD_DIM), lambda b, g, i: (b, i, g)),
        out_shape=jax.ShapeDtypeStruct((b_sz, s_len, MOBA_WIDTH), BF16),
        scratch_shapes=[pltpu.VMEM((2, hb, MOBA_BLOCK, q_tile), F32),
                        pltpu.VMEM((2, hb, MOBA_BLOCK, q_tile), BF16),
                        pltpu.VMEM((hb, VT_ROWS, q_tile), F32)],
        compiler_params=pltpu.CompilerParams(
            dimension_semantics=("parallel", "parallel", "arbitrary"),
            vmem_limit_bytes=VMEM_LIMIT),
        name="moba_attn",
    )(qat, ka, vt)


def _out_proj_kernel(x_ref, odn_ref, omb_ref, gdn_ref, gmb_ref, wdn_ref, wmb_ref, wo_ref, h_ref):
    y_dn = jnp.dot(odn_ref[...], wdn_ref[...], preferred_element_type=F32)
    y_mb = jnp.dot(omb_ref[...], wmb_ref[...], preferred_element_type=F32)
    merged = _sigmoid(gdn_ref[...]) * y_dn + _sigmoid(gmb_ref[...]) * y_mb
    h_ref[...] = x_ref[...] + jnp.dot(merged.astype(BF16), wo_ref[...], preferred_element_type=F32)


def _out_proj(x2d, o_dn, o_mb, main2d, w_dn, w_mb, w_o, tm):
    t, d = x2d.shape
    row_spec = pl.BlockSpec((tm, d), lambda i: (i, 0))
    w_spec = pl.BlockSpec((d, d), lambda i: (0, 0))
    return pl.pallas_call(
        _out_proj_kernel,
        grid=(t // tm,),
        in_specs=[row_spec, row_spec, row_spec,
                  pl.BlockSpec((tm, d), lambda i: (i, 7)),
                  pl.BlockSpec((tm, d), lambda i: (i, 8)),
                  w_spec, w_spec, w_spec],
        out_specs=row_spec,
        out_shape=jax.ShapeDtypeStruct((t, d), F32),
        compiler_params=pltpu.CompilerParams(
            dimension_semantics=("parallel",), vmem_limit_bytes=VMEM_LIMIT),
        name="out_proj",
    )(x2d, o_dn, o_mb, main2d, main2d, w_dn, w_mb, w_o)


def _conv_ffn_kernel(h_ref, nw_ref, wup_ref, cw_ref, cb_ref, wdown_ref, fw_ref, out_ref,
                     u_scr, *, tiles_per_seq, n_split):
    tm = h_ref.shape[0]
    seq_start = (pl.program_id(0) % tiles_per_seq) == 0
    h = h_ref[...]
    var = jnp.mean(h * h, axis=-1, keepdims=True)
    n = ((h * lax.rsqrt(var + NORM_EPS)) * nw_ref[...]).astype(BF16)

    width = D_FF // n_split
    base = SUBLANES - (FFN_CONV - 1)
    acc = jnp.zeros((tm, D_MODEL), F32)
    for part in range(n_split):
        halves = []
        for half in range(2):
            c0 = half * D_FF + part * width
            slot = 2 * part + half
            @pl.when(seq_start)
            def _(slot=slot):
                u_scr[slot, 0:SUBLANES, :] = jnp.zeros((SUBLANES, width), F32)

            @pl.when(jnp.logical_not(seq_start))
            def _(slot=slot):
                u_scr[slot, 0:SUBLANES, :] = u_scr[slot, tm:tm + SUBLANES, :]

            u_scr[slot, SUBLANES:SUBLANES + tm, :] = jnp.dot(
                n, wup_ref[:, c0:c0 + width], preferred_element_type=F32)
            y = (u_scr[slot, SUBLANES:SUBLANES + tm, :]
                 * cw_ref[FFN_CONV - 1:FFN_CONV, c0:c0 + width])
            for j in range(FFN_CONV - 1):
                y = y + u_scr[slot, base + j:base + j + tm, :] * cw_ref[j:j + 1, c0:c0 + width]
            halves.append(y + cb_ref[:, c0:c0 + width])
        act = (_silu(halves[0]) * halves[1]).astype(BF16)
        acc = acc + jnp.dot(act, wdown_ref[part * width:(part + 1) * width, :],
                            preferred_element_type=F32)
    h2 = h + acc
    var2 = jnp.mean(h2 * h2, axis=-1, keepdims=True)
    out_ref[...] = (h2 * lax.rsqrt(var2 + NORM_EPS)) * fw_ref[...]


def _conv_ffn(h2d, norm_w, w_up, conv_w, conv_b, w_down, final_w, tm, s_len, n_split):
    t, d = h2d.shape
    width = D_FF // n_split
    kern = functools.partial(_conv_ffn_kernel, tiles_per_seq=s_len // tm, n_split=n_split)

    def const_spec(shape):
        return pl.BlockSpec(shape, lambda i: (0, 0))

    return pl.pallas_call(
        kern,
        grid=(t // tm,),
        in_specs=[
            pl.BlockSpec((tm, d), lambda i: (i, 0)),
            const_spec((1, d)),
            const_spec((d, 2 * D_FF)),
            const_spec((FFN_CONV, 2 * D_FF)),
            const_spec((1, 2 * D_FF)),
            const_spec((D_FF, d)),
            const_spec((1, d)),
        ],
        out_specs=pl.BlockSpec((tm, d), lambda i: (i, 0)),
        out_shape=jax.ShapeDtypeStruct((t, d), F32),
        scratch_shapes=[pltpu.VMEM((2 * n_split, tm + SUBLANES, width), F32)],
        compiler_params=pltpu.CompilerParams(
            dimension_semantics=("arbitrary",), vmem_limit_bytes=VMEM_LIMIT),
        name="conv_ffn",
    )(h2d, norm_w, w_up, conv_w, conv_b, w_down, final_w)


def _rope_tables(s_len):
    half = MOBA_HEAD_DIM // 2
    inv_freq = jnp.power(ROPE_THETA, -jnp.arange(half, dtype=F32) / half)
    ang = jnp.arange(s_len).astype(F32)[:, None] * inv_freq[None, :]
    cos, sin = jnp.cos(ang), jnp.sin(ang)
    return jnp.concatenate([cos, cos], axis=-1), jnp.concatenate([-sin, sin], axis=-1)


def _pick_tile(n, want):
    t = min(n, want)
    while n % t:
        t //= 2
    return t


def _layer(x, attn_norm_w, w_in, dn_conv_w, dn_a_log, dn_dt_bias, dn_norm_w, w_dn_out,
           w_moba_out, w_o, ffn_norm_w, w_up, ffn_conv_w, ffn_conv_b, w_down, final_norm_w):
    b_sz, s_len, d = x.shape
    t = b_sz * s_len
    x2d = x.reshape(t, d)

    small_lo = 4 * DN_WIDTH
    small_hi = small_lo + 2 * DN_HEADS
    w_main = jnp.concatenate([w_in[:, :small_lo], w_in[:, small_hi:]], axis=1).astype(BF16)
    w_small_cols = w_in[:, small_lo:small_hi]
    w_small = jnp.pad(w_small_cols, ((0, 0), (0, SMALL_W - 2 * DN_HEADS))).astype(BF16)
    w_small_t = w_small_cols.T.astype(BF16)

    main, small, small_t = _in_proj(x2d, attn_norm_w.reshape(1, d), w_main, w_small, w_small_t,
                                    _pick_tile(t, 1024))
    main3d = main.reshape(b_sz, s_len, main.shape[1])
    small3d = small.reshape(b_sz, s_len, SMALL_W)

    pad_r = ((0, 0), (DN_HEADS, SMALL_W - 2 * DN_HEADS))
    alog_r = jnp.pad(dn_a_log.reshape(1, DN_HEADS), pad_r)
    dtb_r = jnp.pad(dn_dt_bias.reshape(1, DN_HEADS), pad_r)
    alog_c = jnp.pad(dn_a_log.reshape(DN_HEADS, 1), ((DN_HEADS, 0), (0, 0)))
    dtb_c = jnp.pad(dn_dt_bias.reshape(DN_HEADS, 1), ((DN_HEADS, 0), (0, 0)))
    o_dn = _deltanet(main3d, small3d, small_t, dn_conv_w, alog_r, dtb_r, alog_c, dtb_c,
                     dn_norm_w.reshape(1, DN_HEAD_DIM), _pick_tile(s_len, 256))

    cos_t, sin_t = _rope_tables(s_len)
    qat, ka, vt = _moba_prep(main3d, cos_t, sin_t)
    o_mb = _moba_attn(qat, ka, vt, 4, _pick_tile(s_len, 2 * MOBA_BLOCK))

    h = _out_proj(x2d, o_dn.reshape(t, DN_WIDTH), o_mb.reshape(t, MOBA_WIDTH), main,
                  w_dn_out.astype(BF16), w_moba_out.astype(BF16), w_o.astype(BF16),
                  _pick_tile(t, 512))

    tm_ffn = _pick_tile(s_len, 256)
    out = _conv_ffn(h, ffn_norm_w.reshape(1, d), w_up.astype(BF16), ffn_conv_w,
                    ffn_conv_b.reshape(1, 2 * D_FF), w_down.astype(BF16),
                    final_norm_w.reshape(1, d), tm_ffn, s_len, 2)
    return out.reshape(b_sz, s_len, d)


def kernel(x, attn_norm_w, w_in, dn_conv_w, dn_A_log, dn_dt_bias, dn_norm_w, w_dn_out, w_moba_out,
           w_o, ffn_norm_w, w_up, ffn_conv_w, ffn_conv_b, w_down, final_norm_w):
    depth = w_in.shape[0]
    assert depth == 1, "the final RMSNorm is fused into the layer's conv_ffn call"
    return _layer(x, attn_norm_w[0], w_in[0], dn_conv_w[0], dn_A_log[0], dn_dt_bias[0],
                  dn_norm_w[0], w_dn_out[0], w_moba_out[0], w_o[0], ffn_norm_w[0], w_up[0],
                  ffn_conv_w[0], ffn_conv_b[0], w_down[0], final_norm_w)
```

```python
import functools

import jax
import jax.numpy as jnp
import numpy as np
from jax import lax
from jax.experimental import pallas as pl
from jax.experimental.pallas import tpu as pltpu

D_MODEL = 1024
DN_HEADS = 8
DN_HEAD_DIM = 128
DN_WIDTH = DN_HEADS * DN_HEAD_DIM
DN_CONV = 4
DN_CHUNK = 64
MOBA_HEADS = 8
MOBA_HEAD_DIM = 128
MOBA_WIDTH = MOBA_HEADS * MOBA_HEAD_DIM
MOBA_BLOCK = 256
MOBA_TOPK = 3
ROPE_THETA = 10000.0
D_FF = 2816
FFN_CONV = 3
NORM_EPS = 1e-6

SMALL_W = 128
SUBLANES = 8
AUG_W = 2 * MOBA_HEAD_DIM
MASK_PENALTY = -(2.0 ** 100)
BF16_ROWS = 16
VT_ROWS = MOBA_HEAD_DIM + BF16_ROWS
MOBA_EXP2_SCALE = (MOBA_HEAD_DIM ** -0.5) * float(np.log2(np.e))
VMEM_LIMIT = 52 * 1024 * 1024

F32 = jnp.float32
BF16 = jnp.bfloat16
NEG_INF = float("-inf")


def _dot(a, b):
    return jnp.dot(a.astype(BF16), b.astype(BF16), preferred_element_type=F32)


def _dot_nt(a, b):
    return lax.dot_general(a.astype(BF16), b.astype(BF16), (((1,), (1,)), ((), ())),
                           preferred_element_type=F32)


def _dot_tn(a, b):
    return lax.dot_general(a.astype(BF16), b.astype(BF16), (((0,), (0,)), ((), ())),
                           preferred_element_type=F32)


def _sigmoid(x):
    return 1.0 / (1.0 + jnp.exp(-x))


def _silu(x):
    return x * _sigmoid(x)


def _softplus(x):
    return jnp.maximum(x, 0.0) + jnp.log1p(jnp.exp(-jnp.abs(x)))


def _in_proj_kernel(x_ref, nw_ref, wm_ref, ws_ref, wst_ref, main_ref, small_ref, smallt_ref, n_scr):
    @pl.when(pl.program_id(1) == 0)
    def _():
        x = x_ref[...]
        var = jnp.mean(x * x, axis=-1, keepdims=True)
        n = ((x * lax.rsqrt(var + NORM_EPS)) * nw_ref[...]).astype(BF16)
        n_scr[...] = n
        small_ref[...] = jnp.dot(n, ws_ref[...], preferred_element_type=F32)
        smallt_ref[...] = lax.dot_general(wst_ref[...], n, (((1,), (1,)), ((), ())),
                                          preferred_element_type=F32)

    main_ref[...] = jnp.dot(n_scr[...], wm_ref[...], preferred_element_type=F32)


def _in_proj(x2d, norm_w, w_main, w_small, w_small_t, tm):
    t, d = x2d.shape
    n_main = w_main.shape[1]
    tn = D_MODEL
    grid = (t // tm, n_main // tn)
    return pl.pallas_call(
        _in_proj_kernel,
        grid=grid,
        in_specs=[
            pl.BlockSpec((tm, d), lambda i, j: (i, 0)),
            pl.BlockSpec((1, d), lambda i, j: (0, 0)),
            pl.BlockSpec((d, tn), lambda i, j: (0, j)),
            pl.BlockSpec((d, SMALL_W), lambda i, j: (0, 0)),
            pl.BlockSpec((2 * DN_HEADS, d), lambda i, j: (0, 0)),
        ],
        out_specs=[
            pl.BlockSpec((tm, tn), lambda i, j: (i, j)),
            pl.BlockSpec((tm, SMALL_W), lambda i, j: (i, 0)),
            pl.BlockSpec((2 * DN_HEADS, tm), lambda i, j: (0, i)),
        ],
        out_shape=[
            jax.ShapeDtypeStruct((t, n_main), F32),
            jax.ShapeDtypeStruct((t, SMALL_W), F32),
            jax.ShapeDtypeStruct((2 * DN_HEADS, t), F32),
        ],
        scratch_shapes=[pltpu.VMEM((tm, d), BF16)],
        compiler_params=pltpu.CompilerParams(
            dimension_semantics=("parallel", "arbitrary"), vmem_limit_bytes=VMEM_LIMIT),
        name="in_proj",
    )(x2d, norm_w, w_main, w_small, w_small_t)


def _chunk_cumsum(x, axis):
    idx = lax.broadcasted_iota(jnp.int32, x.shape, axis) % DN_CHUNK
    shift = 1
    while shift < DN_CHUNK:
        x = x + jnp.where(idx >= shift, pltpu.roll(x, shift, axis), 0.0)
        shift *= 2
    return x


def _unit_lower_inverse(mats, row, col):
    c = mats[0].shape[0]
    eye = (row == col).astype(F32)
    blk = 8
    diag_blk = (row // blk) == (col // blk)
    n = [jnp.where(diag_blk, -a, 0.0) for a in mats]
    n2 = [_dot(x, x) for x in n]
    n4 = [_dot(x, x) for x in n2]
    t = [_dot(eye + x, eye + y) for x, y in zip(n, n2)]
    t = [_dot(x, eye + y) for x, y in zip(t, n4)]
    while blk < c:
        off = ((row // (2 * blk)) == (col // (2 * blk))) & ((row // blk) != (col // blk))
        a_t = [_dot(jnp.where(off, a, 0.0), x) for a, x in zip(mats, t)]
        t = [x - _dot(x, y) for x, y in zip(t, a_t)]
        blk *= 2
    return t


def _deltanet_kernel(q_ref, k_ref, v_ref, qh_ref, kh_ref, vh_ref, z_ref, small_ref, smallt_ref,
                     cwq_ref, cwk_ref, cwv_ref, alog_r_ref, dtb_r_ref, alog_c_ref, dtb_c_ref,
                     normw_ref, out_ref,
                     state_scr, qc_scr, kc_scr, vc_scr, u_scr, wq_scr, kd_scr,
                     qk_scr, gl_scr, bcol_scr, gcol_scr, grow_scr):
    tc = q_ref.shape[0]
    n_chunks = tc // DN_CHUNK
    first = pl.program_id(1) == 0
    heads = range(DN_HEADS)
    cols = [slice(h * DN_HEAD_DIM, (h + 1) * DN_HEAD_DIM) for h in heads]

    @pl.when(first)
    def _():
        state_scr[...] = jnp.zeros_like(state_scr)

    def conv_silu_norm(x_ref, halo_ref, cw_ref, dst, l2norm, scale):
        halo = jnp.where(first, 0.0, halo_ref[...])
        xp = jnp.concatenate([halo[SUBLANES - (DN_CONV - 1):], x_ref[...]], axis=0)
        y = xp[DN_CONV - 1:DN_CONV - 1 + tc] * cw_ref[DN_CONV - 1]
        for j in range(DN_CONV - 1):
            y = y + xp[j:j + tc] * cw_ref[j]
        y = _silu(y)
        if l2norm:
            y = y * lax.rsqrt(jnp.sum(y * y, axis=-1, keepdims=True) + NORM_EPS)
            if scale is not None:
                y = y * scale
        dst[...] = y.reshape(tc * DN_HEADS, DN_HEAD_DIM)

    conv_silu_norm(q_ref, qh_ref, cwq_ref, qc_scr, True, DN_HEAD_DIM ** -0.5)
    conv_silu_norm(k_ref, kh_ref, cwk_ref, kc_scr, True, None)
    conv_silu_norm(v_ref, vh_ref, cwv_ref, vc_scr, False, None)

    def head_rows(scr, c, h):
        return scr[pl.ds(c * DN_CHUNK * DN_HEADS + h, DN_CHUNK, stride=DN_HEADS), :]

    small = small_ref[...]
    bcol_scr[...] = _sigmoid(small)
    gcol_scr[...] = _chunk_cumsum(-jnp.exp(alog_r_ref[...]) * _softplus(small + dtb_r_ref[...]), 0)
    small_t = smallt_ref[...]
    g_t = _chunk_cumsum(-jnp.exp(alog_c_ref[...]) * _softplus(small_t + dtb_c_ref[...]), 1)
    for c in range(n_chunks):
        grow_scr[c] = g_t[:, c * DN_CHUNK:(c + 1) * DN_CHUNK]

    row = lax.broadcasted_iota(jnp.int32, (DN_CHUNK, DN_CHUNK), 0)
    col = lax.broadcasted_iota(jnp.int32, (DN_CHUNK, DN_CHUNK), 1)
    lower_incl = row >= col
    lower_strict = row > col

    def rows_at(start, size):
        if isinstance(start, int):
            return pl.ds(start, size)
        return pl.ds(pl.multiple_of(start, DN_CHUNK), size)

    def prep(c0, chunks):
        items = [(c0 + i, h) for i in range(chunks) for h in heads]
        rows = [rows_at(c * DN_CHUNK, DN_CHUNK) for c, _ in items]
        g_rows = [grow_scr[c0 + i] for i in range(chunks)]
        q = [head_rows(qc_scr, c, h) for c, h in items]
        k = [head_rows(kc_scr, c, h) for c, h in items]
        beta = [bcol_scr[r, h:h + 1] for r, (_, h) in zip(rows, items)]
        gc_b = [jnp.broadcast_to(gcol_scr[r, DN_HEADS + h:DN_HEADS + h + 1],
                                 (DN_CHUNK, DN_HEAD_DIM)) for r, (_, h) in zip(rows, items)]
        eg_b = [jnp.exp(x) for x in gc_b]
        decay = [jnp.exp(jnp.where(
            lower_incl,
            gc_b[n][:, :DN_CHUNK] - g_rows[n // DN_HEADS][DN_HEADS + h:DN_HEADS + h + 1, :],
            NEG_INF)) for n, (_, h) in enumerate(items)]
        kb = [x * y for x, y in zip(k, beta)]
        kq = [_dot_nt(jnp.concatenate([kb[n], q[n]], axis=0), k[n]) for n in range(len(items))]
        strict = [kq[n][:DN_CHUNK] * jnp.where(lower_strict, decay[n], 0.0)
                  for n in range(len(items))]
        for n, (c, h) in enumerate(items):
            qk_scr[h, rows[n], :] = (kq[n][DN_CHUNK:] * decay[n]).astype(BF16)
        t_mat = _unit_lower_inverse(strict, row, col)
        uw = [_dot(t_mat[n], jnp.concatenate([head_rows(vc_scr, c, h) * beta[n], kb[n] * eg_b[n]],
                                             axis=1)) for n, (c, h) in enumerate(items)]
        for n, (c, h) in enumerate(items):
            u_scr[rows[n], cols[h]] = uw[n][:, :DN_HEAD_DIM]
            wq_scr[rows_at(c * 2 * DN_CHUNK, DN_CHUNK), cols[h]] = uw[n][:, DN_HEAD_DIM:].astype(BF16)
            wq_scr[rows_at(c * 2 * DN_CHUNK + DN_CHUNK, DN_CHUNK), cols[h]] = (
                q[n] * eg_b[n]).astype(BF16)
            g_last_b = jnp.broadcast_to(gc_b[n][DN_CHUNK - 1:DN_CHUNK, :], (DN_CHUNK, DN_HEAD_DIM))
            kd_scr[rows[n], cols[h]] = (k[n] * jnp.exp(g_last_b - gc_b[n])).astype(BF16)
            gl_scr[c, h:h + 1, :] = eg_b[n][DN_CHUNK - 1:DN_CHUNK, :]

    prep(0, n_chunks)

    normw = normw_ref[...]

    def scan(c, carry):
        rows = pl.ds(pl.multiple_of(c * DN_CHUNK, DN_CHUNK), DN_CHUNK)
        wq_rows = pl.ds(pl.multiple_of(c * 2 * DN_CHUNK, 2 * DN_CHUNK), 2 * DN_CHUNK)
        gl_all = gl_scr[c]
        state = [state_scr[h] for h in heads]
        state_b = [x.astype(BF16) for x in state]
        ws_qs = [jnp.dot(wq_scr[wq_rows, cols[h]], state_b[h], preferred_element_type=F32)
                 for h in heads]
        v_new_b = [(u_scr[rows, cols[h]] - ws_qs[h][:DN_CHUNK]).astype(BF16) for h in heads]
        intra = [jnp.dot(qk_scr[h, rows, :], v_new_b[h], preferred_element_type=F32) for h in heads]
        d_state = [lax.dot_general(kd_scr[rows, cols[h]], v_new_b[h], (((0,), (0,)), ((), ())),
                                   preferred_element_type=F32) for h in heads]
        for h in heads:
            state_scr[h] = state[h] * gl_all[h:h + 1, :] + d_state[h]
            o = ws_qs[h][DN_CHUNK:] + intra[h]
            var = jnp.mean(o * o, axis=-1, keepdims=True)
            o = (o * lax.rsqrt(var + NORM_EPS)) * normw
            out_ref[rows, cols[h]] = (o * _silu(z_ref[rows, cols[h]])).astype(out_ref.dtype)
        return carry

    lax.fori_loop(0, n_chunks, scan, 0)


def _deltanet(main3d, small3d, small_t3, conv_w, alog_r, dtb_r, alog_c, dtb_c, norm_w, tc):
    b_sz, s_len, _ = main3d.shape
    n_t = s_len // tc
    n_chunks = tc // DN_CHUNK
    halo_blocks = tc // SUBLANES

    main4d = main3d.reshape(b_sz, s_len, main3d.shape[2] // DN_HEAD_DIM, DN_HEAD_DIM)
    conv_w3 = conv_w.reshape(DN_CONV, conv_w.shape[1] // DN_HEAD_DIM, DN_HEAD_DIM)

    def tile_spec(cb):
        return pl.BlockSpec((None, tc, DN_HEADS, DN_HEAD_DIM), lambda b, i, cb=cb: (b, i, cb, 0))

    def halo_spec(cb):
        return pl.BlockSpec((None, SUBLANES, DN_HEADS, DN_HEAD_DIM),
                            lambda b, i, cb=cb: (b, jnp.maximum(i * halo_blocks - 1, 0), cb, 0))

    def const_spec(shape):
        return pl.BlockSpec(shape, lambda b, i: tuple(0 for _ in shape))

    in_specs = [
        tile_spec(0), tile_spec(1), tile_spec(2),
        halo_spec(0), halo_spec(1), halo_spec(2),
        pl.BlockSpec((None, tc, DN_WIDTH), lambda b, i: (b, i, 3)),
        pl.BlockSpec((None, tc, SMALL_W), lambda b, i: (b, i, 0)),
        pl.BlockSpec((2 * DN_HEADS, tc), lambda b, i: (0, b * n_t + i)),
        pl.BlockSpec((DN_CONV, DN_HEADS, DN_HEAD_DIM), lambda b, i: (0, 0, 0)),
        pl.BlockSpec((DN_CONV, DN_HEADS, DN_HEAD_DIM), lambda b, i: (0, 1, 0)),
        pl.BlockSpec((DN_CONV, DN_HEADS, DN_HEAD_DIM), lambda b, i: (0, 2, 0)),
        const_spec((1, SMALL_W)), const_spec((1, SMALL_W)),
        const_spec((2 * DN_HEADS, 1)), const_spec((2 * DN_HEADS, 1)),
        const_spec((1, DN_HEAD_DIM)),
    ]
    scratch = [
        pltpu.VMEM((DN_HEADS, DN_HEAD_DIM, DN_HEAD_DIM), F32),
        pltpu.VMEM((tc * DN_HEADS, DN_HEAD_DIM), F32),
        pltpu.VMEM((tc * DN_HEADS, DN_HEAD_DIM), F32),
        pltpu.VMEM((tc * DN_HEADS, DN_HEAD_DIM), F32),
        pltpu.VMEM((tc, DN_WIDTH), F32),
        pltpu.VMEM((2 * tc, DN_WIDTH), BF16),
        pltpu.VMEM((tc, DN_WIDTH), BF16),
        pltpu.VMEM((DN_HEADS, tc, DN_CHUNK), BF16),
        pltpu.VMEM((n_chunks, DN_HEADS, DN_HEAD_DIM), F32),
        pltpu.VMEM((tc, SMALL_W), F32),
        pltpu.VMEM((tc, SMALL_W), F32),
        pltpu.VMEM((n_chunks, 2 * DN_HEADS, DN_CHUNK), F32),
    ]
    return pl.pallas_call(
        _deltanet_kernel,
        grid=(b_sz, n_t),
        in_specs=in_specs,
        out_specs=pl.BlockSpec((None, tc, DN_WIDTH), lambda b, i: (b, i, 0)),
        out_shape=jax.ShapeDtypeStruct((b_sz, s_len, DN_WIDTH), BF16),
        scratch_shapes=scratch,
        compiler_params=pltpu.CompilerParams(
            dimension_semantics=("parallel", "arbitrary"), vmem_limit_bytes=VMEM_LIMIT),
        name="deltanet",
    )(main4d, main4d, main4d, main4d, main4d, main4d, main3d, small3d, small_t3,
      conv_w3, conv_w3, conv_w3, alog_r, dtb_r, alog_c, dtb_c, norm_w)


def _moba_prep_kernel(q_ref, k_ref, v_ref, cos_ref, sin_ref, qat_ref, ka_ref, vt_ref, kmean_scr):
    blk = pl.program_id(1)
    n_rows = q_ref.shape[0]

    @pl.when(blk == 0)
    def _():
        kmean_scr[...] = jnp.zeros_like(kmean_scr)

    cos = cos_ref[...]
    sin = sin_ref[...]
    half = MOBA_HEAD_DIM // 2
    lane = lax.broadcasted_iota(jnp.int32, (n_rows, MOBA_HEAD_DIM), 1)
    onehot = (lane == blk).astype(BF16)
    blk_row = lax.broadcasted_iota(jnp.int32, (MOBA_HEAD_DIM, n_rows), 0)
    blk_row_f = blk_row.astype(F32)
    mean_row = lax.broadcasted_iota(jnp.int32, (MOBA_HEAD_DIM, MOBA_HEAD_DIM), 0)
    ones_rows = (lax.broadcasted_iota(jnp.int32, (BF16_ROWS, n_rows), 0) == 0).astype(BF16)

    for h in range(MOBA_HEADS):
        cols = slice(h * MOBA_HEAD_DIM, (h + 1) * MOBA_HEAD_DIM)
        q = q_ref[:, cols]
        k = k_ref[:, cols]
        q = q * cos + pltpu.roll(q, half, 1) * sin
        k = k * cos + pltpu.roll(k, half, 1) * sin
        q_t = q.T

        kmean = kmean_scr[h]
        gate = jnp.dot(kmean, q_t, precision=lax.Precision.HIGHEST, preferred_element_type=F32)
        gate = jnp.where(blk_row < blk, gate, NEG_INF)
        sel = blk_row == blk
        for _ in range(MOBA_TOPK):
            mx = jnp.max(gate, axis=0, keepdims=True)
            first_idx = jnp.min(jnp.where(gate == mx, blk_row_f, float(MOBA_HEAD_DIM)),
                                axis=0, keepdims=True)
            hit = blk_row_f == first_idx
            sel = sel | (hit & (mx > NEG_INF))
            gate = jnp.where(hit, NEG_INF, gate)
        qat_ref[h * AUG_W:h * AUG_W + MOBA_HEAD_DIM, :] = (q_t * MOBA_EXP2_SCALE).astype(BF16)
        qat_ref[h * AUG_W + MOBA_HEAD_DIM:(h + 1) * AUG_W, :] = jnp.where(
            sel, 0.0, MASK_PENALTY).astype(BF16)
        ka_ref[:, h * AUG_W:h * AUG_W + MOBA_HEAD_DIM] = k.astype(BF16)
        ka_ref[:, h * AUG_W + MOBA_HEAD_DIM:(h + 1) * AUG_W] = onehot
        vt_ref[h, 0:MOBA_HEAD_DIM, :] = v_ref[:, cols].T.astype(BF16)
        vt_ref[h, MOBA_HEAD_DIM:VT_ROWS, :] = ones_rows

        k_mean_row = jnp.mean(k, axis=0, keepdims=True)
        kmean_scr[h] = jnp.where(mean_row == blk, k_mean_row, kmean)


def _moba_prep(main3d, cos_t, sin_t):
    b_sz, s_len, _ = main3d.shape
    n_blk = s_len // MOBA_BLOCK
    assert n_blk <= MOBA_HEAD_DIM

    def tile_spec(cb):
        return pl.BlockSpec((None, MOBA_BLOCK, MOBA_WIDTH), lambda b, i, cb=cb: (b, i, cb))

    tab_spec = pl.BlockSpec((MOBA_BLOCK, MOBA_HEAD_DIM), lambda b, i: (i, 0))
    return pl.pallas_call(
        _moba_prep_kernel,
        grid=(b_sz, n_blk),
        in_specs=[tile_spec(4), tile_spec(5), tile_spec(6), tab_spec, tab_spec],
        out_specs=[
            pl.BlockSpec((None, MOBA_HEADS * AUG_W, MOBA_BLOCK), lambda b, i: (b, 0, i)),
            pl.BlockSpec((None, MOBA_BLOCK, MOBA_HEADS * AUG_W), lambda b, i: (b, i, 0)),
            pl.BlockSpec((None, MOBA_HEADS, None, VT_ROWS, MOBA_BLOCK),
                         lambda b, i: (b, 0, i, 0, 0)),
        ],
        out_shape=[
            jax.ShapeDtypeStruct((b_sz, MOBA_HEADS * AUG_W, s_len), BF16),
            jax.ShapeDtypeStruct((b_sz, s_len, MOBA_HEADS * AUG_W), BF16),
            jax.ShapeDtypeStruct((b_sz, MOBA_HEADS, n_blk, VT_ROWS, MOBA_BLOCK), BF16),
        ],
        scratch_shapes=[pltpu.VMEM((MOBA_HEADS, MOBA_HEAD_DIM, MOBA_HEAD_DIM), F32)],
        compiler_params=pltpu.CompilerParams(
            dimension_semantics=("parallel", "arbitrary"), vmem_limit_bytes=VMEM_LIMIT),
        name="moba_prep",
    )(main3d, main3d, main3d, cos_t, sin_t)


def _moba_attn_kernel(qat_ref, ka_ref, vt_ref, out_ref, s_scr, p_scr, acc_scr, *, heads_per_step):
    n_q = qat_ref.shape[1]
    blocks_per_tile = n_q // MOBA_BLOCK
    first_blk = pl.program_id(2) * blocks_per_tile
    heads = range(heads_per_step)
    q_t = [qat_ref[h * AUG_W:(h + 1) * AUG_W, :] for h in heads]

    def scores(j):
        rows = pl.ds(pl.multiple_of(j * MOBA_BLOCK, MOBA_BLOCK), MOBA_BLOCK)
        return [jnp.dot(ka_ref[rows, h * AUG_W:(h + 1) * AUG_W], q_t[h],
                        preferred_element_type=F32) for h in heads]

    key = lax.broadcasted_iota(jnp.int32, (MOBA_BLOCK, n_q), 0)
    qry = lax.broadcasted_iota(jnp.int32, (MOBA_BLOCK, n_q), 1)
    m = None
    for r in range(blocks_per_tile):
        s = [jnp.where(key + r * MOBA_BLOCK <= qry, x, NEG_INF) for x in scores(first_blk + r)]
        if r == 0:
            m = [jnp.max(x, axis=0, keepdims=True) for x in s]
            for h in heads:
                acc_scr[h] = jnp.dot(vt_ref[h, first_blk], jnp.exp2(s[h] - m[h]).astype(BF16),
                                     preferred_element_type=F32)
        else:
            m_new = [jnp.maximum(m[h], jnp.max(s[h], axis=0, keepdims=True)) for h in heads]
            for h in heads:
                acc_scr[h] = (jnp.exp2(m[h] - m_new[h]) * acc_scr[h]
                              + jnp.dot(vt_ref[h, first_blk + r],
                                        jnp.exp2(s[h] - m_new[h]).astype(BF16),
                                        preferred_element_type=F32))
            m = m_new

    s_first = scores(0)
    for h in heads:
        s_scr[0, h] = s_first[h]
        p_scr[0, h] = jnp.zeros(p_scr.shape[2:], BF16)
    alpha = [jnp.ones_like(x) for x in m]

    def half_trip(j, cur, nxt, m, alpha):
        s_next = scores(jnp.minimum(j + 1, first_blk - 1))
        j_prev = jnp.maximum(j - 1, 0)
        for h in heads:
            acc_scr[h] = alpha[h] * acc_scr[h] + jnp.dot(vt_ref[h, j_prev], p_scr[cur, h],
                                                         preferred_element_type=F32)
        s_cur = [s_scr[cur, h] for h in heads]
        m_new = [jnp.maximum(m[h], jnp.max(s_cur[h], axis=0, keepdims=True)) for h in heads]
        for h in heads:
            p_scr[nxt, h] = jnp.exp2(s_cur[h] - m_new[h]).astype(BF16)
            s_scr[nxt, h] = s_next[h]
        alpha = [jnp.exp2(m[h] - m_new[h]) for h in heads]
        return m_new, alpha

    def body(jj, carry):
        m, alpha = carry
        m, alpha = half_trip(2 * jj, 0, 1, m, alpha)
        return half_trip(2 * jj + 1, 1, 0, m, alpha)

    assert blocks_per_tile % 2 == 0
    m, alpha = lax.fori_loop(0, first_blk // 2, body, (m, alpha))
    j_last = jnp.maximum(first_blk - 1, 0)
    for h in heads:
        a = alpha[h] * acc_scr[h] + jnp.dot(vt_ref[h, j_last], p_scr[0, h],
                                            preferred_element_type=F32)
        o_t = a[:MOBA_HEAD_DIM] / a[MOBA_HEAD_DIM:MOBA_HEAD_DIM + 1]
        out_ref[:, h * MOBA_HEAD_DIM:(h + 1) * MOBA_HEAD_DIM] = o_t.T.astype(out_ref.dtype)


def _moba_attn(qat, ka, vt, heads_per_step, q_tile):
    b_sz, s_len, _ = ka.shape
    n_blk = s_len // MOBA_BLOCK
    hb = heads_per_step
    resident = pl.Buffered(1)
    return pl.pallas_call(
        functools.partial(_moba_attn_kernel, heads_per_step=hb),
        grid=(b_sz, MOBA_HEADS // hb, s_len // q_tile),
        in_specs=[
            pl.BlockSpec((None, hb * AUG_W, q_tile), lambda b, g, i: (b, g, i)),
            pl.BlockSpec((None, s_len, hb * AUG_W), lambda b, g, i: (b, 0, g),
                         pipeline_mode=resident),
            pl.BlockSpec((None, hb, n_blk, VT_ROWS, MOBA_BLOCK),
                         lambda b, g, i: (b, g, 0, 0, 0), pipeline_mode=resident),
        ],
        out_specs=pl.BlockSpec((None, q_tile, hb * MOBA_HEAD_DIM), lambda b, g, i: (b, i, g)),
        out_shape=jax.ShapeDtypeStruct((b_sz, s_len, MOBA_WIDTH), BF16),
        scratch_shapes=[pltpu.VMEM((2, hb, MOBA_BLOCK, q_tile), F32),
                        pltpu.VMEM((2, hb, MOBA_BLOCK, q_tile), BF16),
                        pltpu.VMEM((hb, VT_ROWS, q_tile), F32)],
        compiler_params=pltpu.CompilerParams(
            dimension_semantics=("parallel", "parallel", "arbitrary"),
            vmem_limit_bytes=VMEM_LIMIT),
        name="moba_attn",
    )(qat, ka, vt)


def _out_proj_kernel(x_ref, odn_ref, omb_ref, gdn_ref, gmb_ref, wdn_ref, wmb_ref, wo_ref, h_ref):
    y_dn = jnp.dot(odn_ref[...], wdn_ref[...], preferred_element_type=F32)
    y_mb = jnp.dot(omb_ref[...], wmb_ref[...], preferred_element_type=F32)
    merged = _sigmoid(gdn_ref[...]) * y_dn + _sigmoid(gmb_ref[...]) * y_mb
    h_ref[...] = x_ref[...] + jnp.dot(merged.astype(BF16), wo_ref[...], preferred_element_type=F32)


def _out_proj(x2d, o_dn, o_mb, main2d, w_dn, w_mb, w_o, tm):
    t, d = x2d.shape
    row_spec = pl.BlockSpec((tm, d), lambda i: (i, 0))
    w_spec = pl.BlockSpec((d, d), lambda i: (0, 0))
    return pl.pallas_call(
        _out_proj_kernel,
        grid=(t // tm,),
        in_specs=[row_spec, row_spec, row_spec,
                  pl.BlockSpec((tm, d), lambda i: (i, 7)),
                  pl.BlockSpec((tm, d), lambda i: (i, 8)),
                  w_spec, w_spec, w_spec],
        out_specs=row_spec,
        out_shape=jax.ShapeDtypeStruct((t, d), F32),
        compiler_params=pltpu.CompilerParams(
            dimension_semantics=("parallel",), vmem_limit_bytes=VMEM_LIMIT),
        name="out_proj",
    )(x2d, o_dn, o_mb, main2d, main2d, w_dn, w_mb, w_o)


def _conv_ffn_kernel(h_ref, nw_ref, wup_ref, cw_ref, cb_ref, wdown_ref, fw_ref, out_ref,
                     u_scr, *, tiles_per_seq, n_split):
    tm = h_ref.shape[0]
    seq_start = (pl.program_id(0) % tiles_per_seq) == 0
    h = h_ref[...]
    var = jnp.mean(h * h, axis=-1, keepdims=True)
    n = ((h * lax.rsqrt(var + NORM_EPS)) * nw_ref[...]).astype(BF16)

    width = D_FF // n_split
    base = SUBLANES - (FFN_CONV - 1)
    acc = jnp.zeros((tm, D_MODEL), F32)
    for part in range(n_split):
        halves = []
        for half in range(2):
            c0 = half * D_FF + part * width
            slot = 2 * part + half
            @pl.when(seq_start)
            def _(slot=slot):
                u_scr[slot, 0:SUBLANES, :] = jnp.zeros((SUBLANES, width), F32)

            @pl.when(jnp.logical_not(seq_start))
            def _(slot=slot):
                u_scr[slot, 0:SUBLANES, :] = u_scr[slot, tm:tm + SUBLANES, :]

            u_scr[slot, SUBLANES:SUBLANES + tm, :] = jnp.dot(
                n, wup_ref[:, c0:c0 + width], preferred_element_type=F32)
            y = (u_scr[slot, SUBLANES:SUBLANES + tm, :]
                 * cw_ref[FFN_CONV - 1:FFN_CONV, c0:c0 + width])
            for j in range(FFN_CONV - 1):
                y = y + u_scr[slot, base + j:base + j + tm, :] * cw_ref[j:j + 1, c0:c0 + width]
            halves.append(y + cb_ref[:, c0:c0 + width])
        act = (_silu(halves[0]) * halves[1]).astype(BF16)
        acc = acc + jnp.dot(act, wdown_ref[part * width:(part + 1) * width, :],
                            preferred_element_type=F32)
    h2 = h + acc
    var2 = jnp.mean(h2 * h2, axis=-1, keepdims=True)
    out_ref[...] = (h2 * lax.rsqrt(var2 + NORM_EPS)) * fw_ref[...]


def _conv_ffn(h2d, norm_w, w_up, conv_w, conv_b, w_down, final_w, tm, s_len, n_split):
    t, d = h2d.shape
    width = D_FF // n_split
    kern = functools.partial(_conv_ffn_kernel, tiles_per_seq=s_len // tm, n_split=n_split)

    def const_spec(shape):
        return pl.BlockSpec(shape, lambda i: (0, 0))

    return pl.pallas_call(
        kern,
        grid=(t // tm,),
        in_specs=[
            pl.BlockSpec((tm, d), lambda i: (i, 0)),
            const_spec((1, d)),
            const_spec((d, 2 * D_FF)),
            const_spec((FFN_CONV, 2 * D_FF)),
            const_spec((1, 2 * D_FF)),
            const_spec((D_FF, d)),
            const_spec((1, d)),
        ],
        out_specs=pl.BlockSpec((tm, d), lambda i: (i, 0)),
        out_shape=jax.ShapeDtypeStruct((t, d), F32),
        scratch_shapes=[pltpu.VMEM((2 * n_split, tm + SUBLANES, width), F32)],
        compiler_params=pltpu.CompilerParams(
            dimension_semantics=("arbitrary",), vmem_limit_bytes=VMEM_LIMIT),
        name="conv_ffn",
    )(h2d, norm_w, w_up, conv_w, conv_b, w_down, final_w)


def _rope_tables(s_len):
    half = MOBA_HEAD_DIM // 2
    inv_freq = jnp.power(ROPE_THETA, -jnp.arange(half, dtype=F32) / half)
    ang = jnp.arange(s_len).astype(F32)[:, None] * inv_freq[None, :]
    cos, sin = jnp.cos(ang), jnp.sin(ang)
    return jnp.concatenate([cos, cos], axis=-1), jnp.concatenate([-sin, sin], axis=-1)


def _pick_tile(n, want):
    t = min(n, want)
    while n % t:
        t //= 2
    return t


def _layer(x, attn_norm_w, w_in, dn_conv_w, dn_a_log, dn_dt_bias, dn_norm_w, w_dn_out,
           w_moba_out, w_o, ffn_norm_w, w_up, ffn_conv_w, ffn_conv_b, w_down, final_norm_w):
    b_sz, s_len, d = x.shape
    t = b_sz * s_len
    x2d = x.reshape(t, d)

    small_lo = 4 * DN_WIDTH
    small_hi = small_lo + 2 * DN_HEADS
    w_main = jnp.concatenate([w_in[:, :small_lo], w_in[:, small_hi:]], axis=1).astype(BF16)
    w_small_cols = w_in[:, small_lo:small_hi]
    w_small = jnp.pad(w_small_cols, ((0, 0), (0, SMALL_W - 2 * DN_HEADS))).astype(BF16)
    w_small_t = w_small_cols.T.astype(BF16)

    main, small, small_t = _in_proj(x2d, attn_norm_w.reshape(1, d), w_main, w_small, w_small_t,
                                    _pick_tile(t, 1024))
    main3d = main.reshape(b_sz, s_len, main.shape[1])
    small3d = small.reshape(b_sz, s_len, SMALL_W)

    pad_r = ((0, 0), (DN_HEADS, SMALL_W - 2 * DN_HEADS))
    alog_r = jnp.pad(dn_a_log.reshape(1, DN_HEADS), pad_r)
    dtb_r = jnp.pad(dn_dt_bias.reshape(1, DN_HEADS), pad_r)
    alog_c = jnp.pad(dn_a_log.reshape(DN_HEADS, 1), ((DN_HEADS, 0), (0, 0)))
    dtb_c = jnp.pad(dn_dt_bias.reshape(DN_HEADS, 1), ((DN_HEADS, 0), (0, 0)))
    o_dn = _deltanet(main3d, small3d, small_t, dn_conv_w, alog_r, dtb_r, alog_c, dtb_c,
                     dn_norm_w.reshape(1, DN_HEAD_DIM), _pick_tile(s_len, 256))

    cos_t, sin_t = _rope_tables(s_len)
    qat, ka, vt = _moba_prep(main3d, cos_t, sin_t)
    o_mb = _moba_attn(qat, ka, vt, 4, _pick_tile(s_len, 2 * MOBA_BLOCK))

    h = _out_proj(x2d, o_dn.reshape(t, DN_WIDTH), o_mb.reshape(t, MOBA_WIDTH), main,
                  w_dn_out.astype(BF16), w_moba_out.astype(BF16), w_o.astype(BF16),
                  _pick_tile(t, 512))

    tm_ffn = _pick_tile(s_len, 256)
    out = _conv_ffn(h, ffn_norm_w.reshape(1, d), w_up.astype(BF16), ffn_conv_w,
                    ffn_conv_b.reshape(1, 2 * D_FF), w_down.astype(BF16),
                    final_norm_w.reshape(1, d), tm_ffn, s_len, 2)
    return out.reshape(b_sz, s_len, d)


def kernel(x, attn_norm_w, w_in, dn_conv_w, dn_A_log, dn_dt_bias, dn_norm_w, w_dn_out, w_moba_out,
           w_o, ffn_norm_w, w_up, ffn_conv_w, ffn_conv_b, w_down, final_norm_w):
    depth = w_in.shape[0]
    assert depth == 1, "the final RMSNorm is fused into the layer's conv_ffn call"
    return _layer(x, attn_norm_w[0], w_in[0], dn_conv_w[0], dn_A_log[0], dn_dt_bias[0],
                  dn_norm_w[0], w_dn_out[0], w_moba_out[0], w_o[0], ffn_norm_w[0], w_up[0],
                  ffn_conv_w[0], ffn_conv_b[0], w_down[0], final_norm_w)
```

```python
import functools

import jax
import jax.numpy as jnp
import numpy as np
from jax import lax
from jax.experimental import pallas as pl
from jax.experimental.pallas import tpu as pltpu

D_MODEL = 1024
DN_HEADS = 8
DN_HEAD_DIM = 128
DN_WIDTH = DN_HEADS * DN_HEAD_DIM
DN_CONV = 4
DN_CHUNK = 64
MOBA_HEADS = 8
MOBA_HEAD_DIM = 128
MOBA_WIDTH = MOBA_HEADS * MOBA_HEAD_DIM
MOBA_BLOCK = 256
MOBA_TOPK = 3
ROPE_THETA = 10000.0
D_FF = 2816
FFN_CONV = 3
NORM_EPS = 1e-6

SMALL_W = 128
DN_QKV_BLOCKS = 3
SUBLANES = 8
AUG_W = 2 * MOBA_HEAD_DIM
MASK_PENALTY = -(2.0 ** 100)
BF16_ROWS = 16
VT_ROWS = MOBA_HEAD_DIM + BF16_ROWS
MOBA_EXP2_SCALE = (MOBA_HEAD_DIM ** -0.5) * float(np.log2(np.e))
VMEM_LIMIT = 52 * 1024 * 1024

F32 = jnp.float32
BF16 = jnp.bfloat16
NEG_INF = float("-inf")


def _dot(a, b):
    return jnp.dot(a.astype(BF16), b.astype(BF16), preferred_element_type=F32)


def _dot_nt(a, b):
    return lax.dot_general(a.astype(BF16), b.astype(BF16), (((1,), (1,)), ((), ())),
                           preferred_element_type=F32)


def _dot_tn(a, b):
    return lax.dot_general(a.astype(BF16), b.astype(BF16), (((0,), (0,)), ((), ())),
                           preferred_element_type=F32)


def _sigmoid(x):
    return 1.0 / (1.0 + jnp.exp(-x))


def _silu(x):
    return x * _sigmoid(x)


def _softplus(x):
    return jnp.maximum(x, 0.0) + jnp.log1p(jnp.exp(-jnp.abs(x)))


def _in_proj_kernel(x_ref, nw_ref, wm_ref, ws_ref, wst_ref, q_ref, k_ref, v_ref, rest_ref, small_ref,
                    smallt_ref, n_scr):
    j = pl.program_id(1)
    tm = x_ref.shape[0]

    @pl.when(j == 0)
    def _():
        x = x_ref[...]
        var = jnp.mean(x * x, axis=-1, keepdims=True)
        n = ((x * lax.rsqrt(var + NORM_EPS)) * nw_ref[...]).astype(BF16)
        n_scr[...] = n
        small_ref[...] = jnp.dot(n, ws_ref[...], preferred_element_type=F32)
        smallt_ref[...] = lax.dot_general(wst_ref[...], n, (((1,), (1,)), ((), ())),
                                          preferred_element_type=F32)

    for blk, tok_ref in enumerate((q_ref, k_ref, v_ref)):
        @pl.when(j == blk)
        def _(tok_ref=tok_ref):
            res = jnp.dot(n_scr[...], wm_ref[...], preferred_element_type=F32)
            for h in range(DN_HEADS):
                tok_ref[pl.ds(h, tm, stride=DN_HEADS), :] = res[:, h * DN_HEAD_DIM:(h + 1) * DN_HEAD_DIM]

    @pl.when(j >= DN_QKV_BLOCKS)
    def _():
        rest_ref[...] = jnp.dot(n_scr[...], wm_ref[...], preferred_element_type=F32)


def _in_proj(x2d, norm_w, w_main, w_small, w_small_t, tm):
    t, d = x2d.shape
    n_main = w_main.shape[1]
    tn = D_MODEL
    assert tn == DN_WIDTH
    grid = (t // tm, n_main // tn)
    tok_spec = pl.BlockSpec((tm * DN_HEADS, DN_HEAD_DIM), lambda i, j: (i, 0))
    tok_shape = jax.ShapeDtypeStruct((t * DN_HEADS, DN_HEAD_DIM), F32)
    return pl.pallas_call(
        _in_proj_kernel,
        grid=grid,
        in_specs=[
            pl.BlockSpec((tm, d), lambda i, j: (i, 0)),
            pl.BlockSpec((1, d), lambda i, j: (0, 0)),
            pl.BlockSpec((d, tn), lambda i, j: (0, j)),
            pl.BlockSpec((d, SMALL_W), lambda i, j: (0, 0)),
            pl.BlockSpec((2 * DN_HEADS, d), lambda i, j: (0, 0)),
        ],
        out_specs=[
            tok_spec, tok_spec, tok_spec,
            pl.BlockSpec((tm, tn), lambda i, j: (i, jnp.maximum(j - DN_QKV_BLOCKS, 0))),
            pl.BlockSpec((tm, SMALL_W), lambda i, j: (i, 0)),
            pl.BlockSpec((2 * DN_HEADS, tm), lambda i, j: (0, i)),
        ],
        out_shape=[
            tok_shape, tok_shape, tok_shape,
            jax.ShapeDtypeStruct((t, n_main - DN_QKV_BLOCKS * tn), F32),
            jax.ShapeDtypeStruct((t, SMALL_W), F32),
            jax.ShapeDtypeStruct((2 * DN_HEADS, t), F32),
        ],
        scratch_shapes=[pltpu.VMEM((tm, d), BF16)],
        compiler_params=pltpu.CompilerParams(
            dimension_semantics=("parallel", "arbitrary"), vmem_limit_bytes=VMEM_LIMIT),
        name="in_proj",
    )(x2d, norm_w, w_main, w_small, w_small_t)


def _chunk_cumsum(x, axis):
    idx = lax.broadcasted_iota(jnp.int32, x.shape, axis) % DN_CHUNK
    shift = 1
    while shift < DN_CHUNK:
        x = x + jnp.where(idx >= shift, pltpu.roll(x, shift, axis), 0.0)
        shift *= 2
    return x


def _unit_lower_inverse(mats, row, col):
    c = mats[0].shape[0]
    eye = (row == col).astype(F32)
    blk = 8
    diag_blk = (row // blk) == (col // blk)
    n = [jnp.where(diag_blk, -a, 0.0) for a in mats]
    n2 = [_dot(x, x) for x in n]
    n4 = [_dot(x, x) for x in n2]
    t = [_dot(eye + x, eye + y) for x, y in zip(n, n2)]
    t = [_dot(x, eye + y) for x, y in zip(t, n4)]
    while blk < c:
        off = ((row // (2 * blk)) == (col // (2 * blk))) & ((row // blk) != (col // blk))
        a_t = [_dot(jnp.where(off, a, 0.0), x) for a, x in zip(mats, t)]
        t = [x - _dot(x, y) for x, y in zip(t, a_t)]
        blk *= 2
    return t


def _deltanet_kernel(q_ref, k_ref, v_ref, qh_ref, kh_ref, vh_ref, z_ref, small_ref, smallt_ref,
                     cwq_ref, cwk_ref, cwv_ref, alog_r_ref, dtb_r_ref, alog_c_ref, dtb_c_ref,
                     normw_ref, out_ref,
                     state_scr, qc_scr, kc_scr, vc_scr, u_scr, wq_scr, kd_scr,
                     qk_scr, gl_scr, bcol_scr, gcol_scr, grow_scr):
    tc = q_ref.shape[0]
    n_chunks = tc // DN_CHUNK
    first = pl.program_id(1) == 0
    heads = range(DN_HEADS)
    cols = [slice(h * DN_HEAD_DIM, (h + 1) * DN_HEAD_DIM) for h in heads]

    @pl.when(first)
    def _():
        state_scr[...] = jnp.zeros_like(state_scr)

    def conv_silu_norm(x_ref, halo_ref, cw_ref, dst, l2norm, scale):
        halo = jnp.where(first, 0.0, halo_ref[...])
        xp = jnp.concatenate([halo[SUBLANES - (DN_CONV - 1):], x_ref[...]], axis=0)
        y = xp[DN_CONV - 1:DN_CONV - 1 + tc] * cw_ref[DN_CONV - 1]
        for j in range(DN_CONV - 1):
            y = y + xp[j:j + tc] * cw_ref[j]
        y = _silu(y)
        if l2norm:
            y = y * lax.rsqrt(jnp.sum(y * y, axis=-1, keepdims=True) + NORM_EPS)
            if scale is not None:
                y = y * scale
        dst[...] = y.reshape(tc * DN_HEADS, DN_HEAD_DIM)

    conv_silu_norm(q_ref, qh_ref, cwq_ref, qc_scr, True, DN_HEAD_DIM ** -0.5)
    conv_silu_norm(k_ref, kh_ref, cwk_ref, kc_scr, True, None)
    conv_silu_norm(v_ref, vh_ref, cwv_ref, vc_scr, False, None)

    def head_rows(scr, c, h):
        return scr[pl.ds(c * DN_CHUNK * DN_HEADS + h, DN_CHUNK, stride=DN_HEADS), :]

    small = small_ref[...]
    bcol_scr[...] = _sigmoid(small)
    gcol_scr[...] = _chunk_cumsum(-jnp.exp(alog_r_ref[...]) * _softplus(small + dtb_r_ref[...]), 0)
    small_t = smallt_ref[...]
    g_t = _chunk_cumsum(-jnp.exp(alog_c_ref[...]) * _softplus(small_t + dtb_c_ref[...]), 1)
    for c in range(n_chunks):
        grow_scr[c] = g_t[:, c * DN_CHUNK:(c + 1) * DN_CHUNK]

    row = lax.broadcasted_iota(jnp.int32, (DN_CHUNK, DN_CHUNK), 0)
    col = lax.broadcasted_iota(jnp.int32, (DN_CHUNK, DN_CHUNK), 1)
    lower_incl = row >= col
    lower_strict = row > col

    def rows_at(start, size):
        if isinstance(start, int):
            return pl.ds(start, size)
        return pl.ds(pl.multiple_of(start, DN_CHUNK), size)

    def prep(c0, chunks):
        items = [(c0 + i, h) for i in range(chunks) for h in heads]
        rows = [rows_at(c * DN_CHUNK, DN_CHUNK) for c, _ in items]
        g_rows = [grow_scr[c0 + i] for i in range(chunks)]
        q = [head_rows(qc_scr, c, h) for c, h in items]
        k = [head_rows(kc_scr, c, h) for c, h in items]
        beta = [bcol_scr[r, h:h + 1] for r, (_, h) in zip(rows, items)]
        gc_b = [jnp.broadcast_to(gcol_scr[r, DN_HEADS + h:DN_HEADS + h + 1],
                                 (DN_CHUNK, DN_HEAD_DIM)) for r, (_, h) in zip(rows, items)]
        eg_b = [jnp.exp(x) for x in gc_b]
        decay = [jnp.exp(jnp.where(
            lower_incl,
            gc_b[n][:, :DN_CHUNK] - g_rows[n // DN_HEADS][DN_HEADS + h:DN_HEADS + h + 1, :],
            NEG_INF)) for n, (_, h) in enumerate(items)]
        kb = [x * y for x, y in zip(k, beta)]
        kq = [_dot_nt(jnp.concatenate([kb[n], q[n]], axis=0), k[n]) for n in range(len(items))]
        strict = [kq[n][:DN_CHUNK] * jnp.where(lower_strict, decay[n], 0.0)
                  for n in range(len(items))]
        for n, (c, h) in enumerate(items):
            qk_scr[h, rows[n], :] = (kq[n][DN_CHUNK:] * decay[n]).astype(BF16)
        t_mat = _unit_lower_inverse(strict, row, col)
        uw = [_dot(t_mat[n], jnp.concatenate([head_rows(vc_scr, c, h) * beta[n], kb[n] * eg_b[n]],
                                             axis=1)) for n, (c, h) in enumerate(items)]
        for n, (c, h) in enumerate(items):
            u_scr[rows[n], cols[h]] = uw[n][:, :DN_HEAD_DIM]
            wq_scr[rows_at(c * 2 * DN_CHUNK, DN_CHUNK), cols[h]] = uw[n][:, DN_HEAD_DIM:].astype(BF16)
            wq_scr[rows_at(c * 2 * DN_CHUNK + DN_CHUNK, DN_CHUNK), cols[h]] = (
                q[n] * eg_b[n]).astype(BF16)
            g_last_b = jnp.broadcast_to(gc_b[n][DN_CHUNK - 1:DN_CHUNK, :], (DN_CHUNK, DN_HEAD_DIM))
            kd_scr[rows[n], cols[h]] = (k[n] * jnp.exp(g_last_b - gc_b[n])).astype(BF16)
            gl_scr[c, h:h + 1, :] = eg_b[n][DN_CHUNK - 1:DN_CHUNK, :]

    prep(0, n_chunks)

    normw = normw_ref[...]

    def scan(c, carry):
        rows = pl.ds(pl.multiple_of(c * DN_CHUNK, DN_CHUNK), DN_CHUNK)
        wq_rows = pl.ds(pl.multiple_of(c * 2 * DN_CHUNK, 2 * DN_CHUNK), 2 * DN_CHUNK)
        gl_all = gl_scr[c]
        state = [state_scr[h] for h in heads]
        state_b = [x.astype(BF16) for x in state]
        ws_qs = [jnp.dot(wq_scr[wq_rows, cols[h]], state_b[h], preferred_element_type=F32)
                 for h in heads]
        v_new_b = [(u_scr[rows, cols[h]] - ws_qs[h][:DN_CHUNK]).astype(BF16) for h in heads]
        intra = [jnp.dot(qk_scr[h, rows, :], v_new_b[h], preferred_element_type=F32) for h in heads]
        d_state = [lax.dot_general(kd_scr[rows, cols[h]], v_new_b[h], (((0,), (0,)), ((), ())),
                                   preferred_element_type=F32) for h in heads]
        for h in heads:
            state_scr[h] = state[h] * gl_all[h:h + 1, :] + d_state[h]
            o = ws_qs[h][DN_CHUNK:] + intra[h]
            var = jnp.mean(o * o, axis=-1, keepdims=True)
            o = (o * lax.rsqrt(var + NORM_EPS)) * normw
            out_ref[rows, cols[h]] = (o * _silu(z_ref[rows, cols[h]])).astype(out_ref.dtype)
        return carry

    lax.fori_loop(0, n_chunks, scan, 0)


def _deltanet(q4d, k4d, v4d, rest3d, small3d, small_t3, conv_w, alog_r, dtb_r, alog_c, dtb_c, norm_w, tc):
    b_sz, s_len, _ = rest3d.shape
    n_t = s_len // tc
    n_chunks = tc // DN_CHUNK
    halo_blocks = tc // SUBLANES

    conv_w3 = conv_w.reshape(DN_CONV, conv_w.shape[1] // DN_HEAD_DIM, DN_HEAD_DIM)

    tile_spec = pl.BlockSpec((None, tc, DN_HEADS, DN_HEAD_DIM), lambda b, i: (b, i, 0, 0))
    halo_spec = pl.BlockSpec((None, SUBLANES, DN_HEADS, DN_HEAD_DIM),
                             lambda b, i: (b, jnp.maximum(i * halo_blocks - 1, 0), 0, 0))

    def const_spec(shape):
        return pl.BlockSpec(shape, lambda b, i: tuple(0 for _ in shape))

    in_specs = [
        tile_spec, tile_spec, tile_spec,
        halo_spec, halo_spec, halo_spec,
        pl.BlockSpec((None, tc, DN_WIDTH), lambda b, i: (b, i, 0)),
        pl.BlockSpec((None, tc, SMALL_W), lambda b, i: (b, i, 0)),
        pl.BlockSpec((2 * DN_HEADS, tc), lambda b, i: (0, b * n_t + i)),
        pl.BlockSpec((DN_CONV, DN_HEADS, DN_HEAD_DIM), lambda b, i: (0, 0, 0)),
        pl.BlockSpec((DN_CONV, DN_HEADS, DN_HEAD_DIM), lambda b, i: (0, 1, 0)),
        pl.BlockSpec((DN_CONV, DN_HEADS, DN_HEAD_DIM), lambda b, i: (0, 2, 0)),
        const_spec((1, SMALL_W)), const_spec((1, SMALL_W)),
        const_spec((2 * DN_HEADS, 1)), const_spec((2 * DN_HEADS, 1)),
        const_spec((1, DN_HEAD_DIM)),
    ]
    scratch = [
        pltpu.VMEM((DN_HEADS, DN_HEAD_DIM, DN_HEAD_DIM), F32),
        pltpu.VMEM((tc * DN_HEADS, DN_HEAD_DIM), F32),
        pltpu.VMEM((tc * DN_HEADS, DN_HEAD_DIM), F32),
        pltpu.VMEM((tc * DN_HEADS, DN_HEAD_DIM), F32),
        pltpu.VMEM((tc, DN_WIDTH), F32),
        pltpu.VMEM((2 * tc, DN_WIDTH), BF16),
        pltpu.VMEM((tc, DN_WIDTH), BF16),
        pltpu.VMEM((DN_HEADS, tc, DN_CHUNK), BF16),
        pltpu.VMEM((n_chunks, DN_HEADS, DN_HEAD_DIM), F32),
        pltpu.VMEM((tc, SMALL_W), F32),
        pltpu.VMEM((tc, SMALL_W), F32),
        pltpu.VMEM((n_chunks, 2 * DN_HEADS, DN_CHUNK), F32),
    ]
    return pl.pallas_call(
        _deltanet_kernel,
        grid=(b_sz, n_t),
        in_specs=in_specs,
        out_specs=pl.BlockSpec((None, tc, DN_WIDTH), lambda b, i: (b, i, 0)),
        out_shape=jax.ShapeDtypeStruct((b_sz, s_len, DN_WIDTH), BF16),
        scratch_shapes=scratch,
        compiler_params=pltpu.CompilerParams(
            dimension_semantics=("parallel", "arbitrary"), vmem_limit_bytes=VMEM_LIMIT),
        name="deltanet",
    )(q4d, k4d, v4d, q4d, k4d, v4d, rest3d, small3d, small_t3,
      conv_w3, conv_w3, conv_w3, alog_r, dtb_r, alog_c, dtb_c, norm_w)


def _moba_prep_kernel(q_ref, k_ref, v_ref, cos_ref, sin_ref, qat_ref, ka_ref, vt_ref, kmean_scr):
    blk = pl.program_id(1)
    n_rows = q_ref.shape[0]

    @pl.when(blk == 0)
    def _():
        kmean_scr[...] = jnp.zeros_like(kmean_scr)

    cos = cos_ref[...]
    sin = sin_ref[...]
    half = MOBA_HEAD_DIM // 2
    lane = lax.broadcasted_iota(jnp.int32, (n_rows, MOBA_HEAD_DIM), 1)
    onehot = (lane == blk).astype(BF16)
    blk_row = lax.broadcasted_iota(jnp.int32, (MOBA_HEAD_DIM, n_rows), 0)
    blk_row_f = blk_row.astype(F32)
    mean_row = lax.broadcasted_iota(jnp.int32, (MOBA_HEAD_DIM, MOBA_HEAD_DIM), 0)
    ones_rows = (lax.broadcasted_iota(jnp.int32, (BF16_ROWS, n_rows), 0) == 0).astype(BF16)

    for h in range(MOBA_HEADS):
        cols = slice(h * MOBA_HEAD_DIM, (h + 1) * MOBA_HEAD_DIM)
        q = q_ref[:, cols]
        k = k_ref[:, cols]
        q = q * cos + pltpu.roll(q, half, 1) * sin
        k = k * cos + pltpu.roll(k, half, 1) * sin
        q_t = q.T

        kmean = kmean_scr[h]
        gate = jnp.dot(kmean, q_t, precision=lax.Precision.HIGHEST, preferred_element_type=F32)
        gate = jnp.where(blk_row < blk, gate, NEG_INF)
        sel = blk_row == blk
        for _ in range(MOBA_TOPK):
            mx = jnp.max(gate, axis=0, keepdims=True)
            first_idx = jnp.min(jnp.where(gate == mx, blk_row_f, float(MOBA_HEAD_DIM)),
                                axis=0, keepdims=True)
            hit = blk_row_f == first_idx
            sel = sel | (hit & (mx > NEG_INF))
            gate = jnp.where(hit, NEG_INF, gate)
        qat_ref[h * AUG_W:h * AUG_W + MOBA_HEAD_DIM, :] = (q_t * MOBA_EXP2_SCALE).astype(BF16)
        qat_ref[h * AUG_W + MOBA_HEAD_DIM:(h + 1) * AUG_W, :] = jnp.where(
            sel, 0.0, MASK_PENALTY).astype(BF16)
        ka_ref[:, h * AUG_W:h * AUG_W + MOBA_HEAD_DIM] = k.astype(BF16)
        ka_ref[:, h * AUG_W + MOBA_HEAD_DIM:(h + 1) * AUG_W] = onehot
        vt_ref[h, 0:MOBA_HEAD_DIM, :] = v_ref[:, cols].T.astype(BF16)
        vt_ref[h, MOBA_HEAD_DIM:VT_ROWS, :] = ones_rows

        k_mean_row = jnp.mean(k, axis=0, keepdims=True)
        kmean_scr[h] = jnp.where(mean_row == blk, k_mean_row, kmean)


def _moba_prep(main3d, cos_t, sin_t):
    b_sz, s_len, _ = main3d.shape
    n_blk = s_len // MOBA_BLOCK
    assert n_blk <= MOBA_HEAD_DIM

    def tile_spec(cb):
        return pl.BlockSpec((None, MOBA_BLOCK, MOBA_WIDTH), lambda b, i, cb=cb: (b, i, cb))

    tab_spec = pl.BlockSpec((MOBA_BLOCK, MOBA_HEAD_DIM), lambda b, i: (i, 0))
    return pl.pallas_call(
        _moba_prep_kernel,
        grid=(b_sz, n_blk),
        in_specs=[tile_spec(1), tile_spec(2), tile_spec(3), tab_spec, tab_spec],
        out_specs=[
            pl.BlockSpec((None, MOBA_HEADS * AUG_W, MOBA_BLOCK), lambda b, i: (b, 0, i)),
            pl.BlockSpec((None, MOBA_BLOCK, MOBA_HEADS * AUG_W), lambda b, i: (b, i, 0)),
            pl.BlockSpec((None, MOBA_HEADS, None, VT_ROWS, MOBA_BLOCK),
                         lambda b, i: (b, 0, i, 0, 0)),
        ],
        out_shape=[
            jax.ShapeDtypeStruct((b_sz, MOBA_HEADS * AUG_W, s_len), BF16),
            jax.ShapeDtypeStruct((b_sz, s_len, MOBA_HEADS * AUG_W), BF16),
            jax.ShapeDtypeStruct((b_sz, MOBA_HEADS, n_blk, VT_ROWS, MOBA_BLOCK), BF16),
        ],
        scratch_shapes=[pltpu.VMEM((MOBA_HEADS, MOBA_HEAD_DIM, MOBA_HEAD_DIM), F32)],
        compiler_params=pltpu.CompilerParams(
            dimension_semantics=("parallel", "arbitrary"), vmem_limit_bytes=VMEM_LIMIT),
        name="moba_prep",
    )(main3d, main3d, main3d, cos_t, sin_t)


def _moba_attn_kernel(qat_ref, ka_ref, vt_ref, out_ref, s_scr, p_scr, acc_scr, *, heads_per_step):
    n_q = qat_ref.shape[1]
    blocks_per_tile = n_q // MOBA_BLOCK
    first_blk = pl.program_id(2) * blocks_per_tile
    heads = range(heads_per_step)
    q_t = [qat_ref[h * AUG_W:(h + 1) * AUG_W, :] for h in heads]

    def scores(j):
        rows = pl.ds(pl.multiple_of(j * MOBA_BLOCK, MOBA_BLOCK), MOBA_BLOCK)
        return [jnp.dot(ka_ref[rows, h * AUG_W:(h + 1) * AUG_W], q_t[h],
                        preferred_element_type=F32) for h in heads]

    key = lax.broadcasted_iota(jnp.int32, (MOBA_BLOCK, n_q), 0)
    qry = lax.broadcasted_iota(jnp.int32, (MOBA_BLOCK, n_q), 1)
    m = None
    for r in range(blocks_per_tile):
        s = [jnp.where(key + r * MOBA_BLOCK <= qry, x, NEG_INF) for x in scores(first_blk + r)]
        if r == 0:
            m = [jnp.max(x, axis=0, keepdims=True) for x in s]
            for h in heads:
                acc_scr[h] = jnp.dot(vt_ref[h, first_blk], jnp.exp2(s[h] - m[h]).astype(BF16),
                                     preferred_element_type=F32)
        else:
            m_new = [jnp.maximum(m[h], jnp.max(s[h], axis=0, keepdims=True)) for h in heads]
            for h in heads:
                acc_scr[h] = (jnp.exp2(m[h] - m_new[h]) * acc_scr[h]
                              + jnp.dot(vt_ref[h, first_blk + r],
                                        jnp.exp2(s[h] - m_new[h]).astype(BF16),
                                        preferred_element_type=F32))
            m = m_new

    s_first = scores(0)
    for h in heads:
        s_scr[0, h] = s_first[h]
        p_scr[0, h] = jnp.zeros(p_scr.shape[2:], BF16)
    alpha = [jnp.ones_like(x) for x in m]

    def half_trip(j, cur, nxt, m, alpha):
        s_next = scores(jnp.minimum(j + 1, first_blk - 1))
        j_prev = jnp.maximum(j - 1, 0)
        for h in heads:
            acc_scr[h] = alpha[h] * acc_scr[h] + jnp.dot(vt_ref[h, j_prev], p_scr[cur, h],
                                                         preferred_element_type=F32)
        s_cur = [s_scr[cur, h] for h in heads]
        m_new = [jnp.maximum(m[h], jnp.max(s_cur[h], axis=0, keepdims=True)) for h in heads]
        for h in heads:
            p_scr[nxt, h] = jnp.exp2(s_cur[h] - m_new[h]).astype(BF16)
            s_scr[nxt, h] = s_next[h]
        alpha = [jnp.exp2(m[h] - m_new[h]) for h in heads]
        return m_new, alpha

    def body(jj, carry):
        m, alpha = carry
        m, alpha = half_trip(2 * jj, 0, 1, m, alpha)
        return half_trip(2 * jj + 1, 1, 0, m, alpha)

    assert blocks_per_tile % 2 == 0
    m, alpha = lax.fori_loop(0, first_blk // 2, body, (m, alpha))
    j_last = jnp.maximum(first_blk - 1, 0)
    for h in heads:
        a = alpha[h] * acc_scr[h] + jnp.dot(vt_ref[h, j_last], p_scr[0, h],
                                            preferred_element_type=F32)
        o_t = a[:MOBA_HEAD_DIM] / a[MOBA_HEAD_DIM:MOBA_HEAD_DIM + 1]
        out_ref[:, h * MOBA_HEAD_DIM:(h + 1) * MOBA_HEAD_DIM] = o_t.T.astype(out_ref.dtype)


def _moba_attn(qat, ka, vt, heads_per_step, q_tile):
    b_sz, s_len, _ = ka.shape
    n_blk = s_len // MOBA_BLOCK
    hb = heads_per_step
    resident = pl.Buffered(1)
    return pl.pallas_call(
        functools.partial(_moba_attn_kernel, heads_per_step=hb),
        grid=(b_sz, MOBA_HEADS // hb, s_len // q_tile),
        in_specs=[
            pl.BlockSpec((None, hb * AUG_W, q_tile), lambda b, g, i: (b, g, i)),
            pl.BlockSpec((None, s_len, hb * AUG_W), lambda b, g, i: (b, 0, g),
                         pipeline_mode=resident),
            pl.BlockSpec((None, hb, n_blk, VT_ROWS, MOBA_BLOCK),
                         lambda b, g, i: (b, g, 0, 0, 0), pipeline_mode=resident),
        ],
        out_specs=pl.BlockSpec((None, q_tile, hb * MOBA_HEAD_DIM), lambda b, g, i: (b, i, g)),
        out_shape=jax.ShapeDtypeStruct((b_sz, s_len, MOBA_WIDTH), BF16),
        scratch_shapes=[pltpu.VMEM((2, hb, MOBA_BLOCK, q_tile), F32),
                        pltpu.VMEM((2, hb, MOBA_BLOCK, q_tile), BF16),
                        pltpu.VMEM((hb, VT_ROWS, q_tile), F32)],
        compiler_params=pltpu.CompilerParams(
            dimension_semantics=("parallel", "parallel", "arbitrary"),
            vmem_limit_bytes=VMEM_LIMIT),
        name="moba_attn",
    )(qat, ka, vt)


def _out_proj_kernel(x_ref, odn_ref, omb_ref, gdn_ref, gmb_ref, wdn_ref, wmb_ref, wo_ref, h_ref):
    y_dn = jnp.dot(odn_ref[...], wdn_ref[...], preferred_element_type=F32)
    y_mb = jnp.dot(omb_ref[...], wmb_ref[...], preferred_element_type=F32)
    merged = _sigmoid(gdn_ref[...]) * y_dn + _sigmoid(gmb_ref[...]) * y_mb
    h_ref[...] = x_ref[...] + jnp.dot(merged.astype(BF16), wo_ref[...], preferred_element_type=F32)


def _out_proj(x2d, o_dn, o_mb, main2d, w_dn, w_mb, w_o, tm):
    t, d = x2d.shape
    row_spec = pl.BlockSpec((tm, d), lambda i: (i, 0))
    w_spec = pl.BlockSpec((d, d), lambda i: (0, 0))
    return pl.pallas_call(
        _out_proj_kernel,
        grid=(t // tm,),
        in_specs=[row_spec, row_spec, row_spec,
                  pl.BlockSpec((tm, d), lambda i: (i, 4)),
                  pl.BlockSpec((tm, d), lambda i: (i, 5)),
                  w_spec, w_spec, w_spec],
        out_specs=row_spec,
        out_shape=jax.ShapeDtypeStruct((t, d), F32),
        compiler_params=pltpu.CompilerParams(
            dimension_semantics=("parallel",), vmem_limit_bytes=VMEM_LIMIT),
        name="out_proj",
    )(x2d, o_dn, o_mb, main2d, main2d, w_dn, w_mb, w_o)


def _conv_ffn_kernel(h_ref, nw_ref, wup_ref, cw_ref, cb_ref, wdown_ref, fw_ref, out_ref,
                     u_scr, *, tiles_per_seq, n_split):
    tm = h_ref.shape[0]
    seq_start = (pl.program_id(0) % tiles_per_seq) == 0
    h = h_ref[...]
    var = jnp.mean(h * h, axis=-1, keepdims=True)
    n = ((h * lax.rsqrt(var + NORM_EPS)) * nw_ref[...]).astype(BF16)

    width = D_FF // n_split
    base = SUBLANES - (FFN_CONV - 1)
    acc = jnp.zeros((tm, D_MODEL), F32)
    for part in range(n_split):
        halves = []
        for half in range(2):
            c0 = half * D_FF + part * width
            slot = 2 * part + half
            @pl.when(seq_start)
            def _(slot=slot):
                u_scr[slot, 0:SUBLANES, :] = jnp.zeros((SUBLANES, width), F32)

            @pl.when(jnp.logical_not(seq_start))
            def _(slot=slot):
                u_scr[slot, 0:SUBLANES, :] = u_scr[slot, tm:tm + SUBLANES, :]

            u_scr[slot, SUBLANES:SUBLANES + tm, :] = jnp.dot(
                n, wup_ref[:, c0:c0 + width], preferred_element_type=F32)
            y = (u_scr[slot, SUBLANES:SUBLANES + tm, :]
                 * cw_ref[FFN_CONV - 1:FFN_CONV, c0:c0 + width])
            for j in range(FFN_CONV - 1):
                y = y + u_scr[slot, base + j:base + j + tm, :] * cw_ref[j:j + 1, c0:c0 + width]
            halves.append(y + cb_ref[:, c0:c0 + width])
        act = (_silu(halves[0]) * halves[1]).astype(BF16)
        acc = acc + jnp.dot(act, wdown_ref[part * width:(part + 1) * width, :],
                            preferred_element_type=F32)
    h2 = h + acc
    var2 = jnp.mean(h2 * h2, axis=-1, keepdims=True)
    out_ref[...] = (h2 * lax.rsqrt(var2 + NORM_EPS)) * fw_ref[...]


def _conv_ffn(h2d, norm_w, w_up, conv_w, conv_b, w_down, final_w, tm, s_len, n_split):
    t, d = h2d.shape
    width = D_FF // n_split
    kern = functools.partial(_conv_ffn_kernel, tiles_per_seq=s_len // tm, n_split=n_split)

    def const_spec(shape):
        return pl.BlockSpec(shape, lambda i: (0, 0))

    return pl.pallas_call(
        kern,
        grid=(t // tm,),
        in_specs=[
            pl.BlockSpec((tm, d), lambda i: (i, 0)),
            const_spec((1, d)),
            const_spec((d, 2 * D_FF)),
            const_spec((FFN_CONV, 2 * D_FF)),
            const_spec((1, 2 * D_FF)),
            const_spec((D_FF, d)),
            const_spec((1, d)),
        ],
        out_specs=pl.BlockSpec((tm, d), lambda i: (i, 0)),
        out_shape=jax.ShapeDtypeStruct((t, d), F32),
        scratch_shapes=[pltpu.VMEM((2 * n_split, tm + SUBLANES, width), F32)],
        compiler_params=pltpu.CompilerParams(
            dimension_semantics=("arbitrary",), vmem_limit_bytes=VMEM_LIMIT),
        name="conv_ffn",
    )(h2d, norm_w, w_up, conv_w, conv_b, w_down, final_w)


def _rope_tables(s_len):
    half = MOBA_HEAD_DIM // 2
    inv_freq = jnp.power(ROPE_THETA, -jnp.arange(half, dtype=F32) / half)
    ang = jnp.arange(s_len).astype(F32)[:, None] * inv_freq[None, :]
    cos, sin = jnp.cos(ang), jnp.sin(ang)
    return jnp.concatenate([cos, cos], axis=-1), jnp.concatenate([-sin, sin], axis=-1)


def _pick_tile(n, want):
    t = min(n, want)
    while n % t:
        t //= 2
    return t


def _layer(x, attn_norm_w, w_in, dn_conv_w, dn_a_log, dn_dt_bias, dn_norm_w, w_dn_out,
           w_moba_out, w_o, ffn_norm_w, w_up, ffn_conv_w, ffn_conv_b, w_down, final_norm_w):
    b_sz, s_len, d = x.shape
    t = b_sz * s_len
    x2d = x.reshape(t, d)

    small_lo = 4 * DN_WIDTH
    small_hi = small_lo + 2 * DN_HEADS
    w_main = jnp.concatenate([w_in[:, :small_lo], w_in[:, small_hi:]], axis=1).astype(BF16)
    w_small_cols = w_in[:, small_lo:small_hi]
    w_small = jnp.pad(w_small_cols, ((0, 0), (0, SMALL_W - 2 * DN_HEADS))).astype(BF16)
    w_small_t = w_small_cols.T.astype(BF16)

    q_tok, k_tok, v_tok, rest, small, small_t = _in_proj(
        x2d, attn_norm_w.reshape(1, d), w_main, w_small, w_small_t, _pick_tile(t, 1024))
    tok4d = (b_sz, s_len, DN_HEADS, DN_HEAD_DIM)
    main3d = rest.reshape(b_sz, s_len, rest.shape[1])
    small3d = small.reshape(b_sz, s_len, SMALL_W)

    pad_r = ((0, 0), (DN_HEADS, SMALL_W - 2 * DN_HEADS))
    alog_r = jnp.pad(dn_a_log.reshape(1, DN_HEADS), pad_r)
    dtb_r = jnp.pad(dn_dt_bias.reshape(1, DN_HEADS), pad_r)
    alog_c = jnp.pad(dn_a_log.reshape(DN_HEADS, 1), ((DN_HEADS, 0), (0, 0)))
    dtb_c = jnp.pad(dn_dt_bias.reshape(DN_HEADS, 1), ((DN_HEADS, 0), (0, 0)))
    o_dn = _deltanet(q_tok.reshape(tok4d), k_tok.reshape(tok4d), v_tok.reshape(tok4d), main3d,
                     small3d, small_t, dn_conv_w, alog_r, dtb_r, alog_c, dtb_c,
                     dn_norm_w.reshape(1, DN_HEAD_DIM), _pick_tile(s_len, 256))

    cos_t, sin_t = _rope_tables(s_len)
    qat, ka, vt = _moba_prep(main3d, cos_t, sin_t)
    o_mb = _moba_attn(qat, ka, vt, 4, _pick_tile(s_len, 2 * MOBA_BLOCK))

    h = _out_proj(x2d, o_dn.reshape(t, DN_WIDTH), o_mb.reshape(t, MOBA_WIDTH), rest,
                  w_dn_out.astype(BF16), w_moba_out.astype(BF16), w_o.astype(BF16),
                  _pick_tile(t, 512))

    tm_ffn = _pick_tile(s_len, 256)
    out = _conv_ffn(h, ffn_norm_w.reshape(1, d), w_up.astype(BF16), ffn_conv_w,
                    ffn_conv_b.reshape(1, 2 * D_FF), w_down.astype(BF16),
                    final_norm_w.reshape(1, d), tm_ffn, s_len, 2)
    return out.reshape(b_sz, s_len, d)


def kernel(x, attn_norm_w, w_in, dn_conv_w, dn_A_log, dn_dt_bias, dn_norm_w, w_dn_out, w_moba_out,
           w_o, ffn_norm_w, w_up, ffn_conv_w, ffn_conv_b, w_down, final_norm_w):
    depth = w_in.shape[0]
    assert depth == 1, "the final RMSNorm is fused into the layer's conv_ffn call"
    return _layer(x, attn_norm_w[0], w_in[0], dn_conv_w[0], dn_A_log[0], dn_dt_bias[0],
                  dn_norm_w[0], w_dn_out[0], w_moba_out[0], w_o[0], ffn_norm_w[0], w_up[0],
                  ffn_conv_w[0], ffn_conv_b[0], w_down[0], final_norm_w)
```

```python
import functools

import jax
import jax.numpy as jnp
import numpy as np
from jax import lax
from jax.experimental import pallas as pl
from jax.experimental.pallas import tpu as pltpu

D_MODEL = 1024
DN_HEADS = 8
DN_HEAD_DIM = 128
DN_WIDTH = DN_HEADS * DN_HEAD_DIM
DN_CONV = 4
DN_CHUNK = 64
MOBA_HEADS = 8
MOBA_HEAD_DIM = 128
MOBA_WIDTH = MOBA_HEADS * MOBA_HEAD_DIM
MOBA_BLOCK = 256
MOBA_TOPK = 3
ROPE_THETA = 10000.0
D_FF = 2816
FFN_CONV = 3
NORM_EPS = 1e-6

SMALL_W = 128
DN_QKV_BLOCKS = 3
SUBLANES = 8
AUG_W = 2 * MOBA_HEAD_DIM
MASK_PENALTY = -(2.0 ** 100)
BF16_ROWS = 16
VT_ROWS = MOBA_HEAD_DIM + BF16_ROWS
MOBA_EXP2_SCALE = (MOBA_HEAD_DIM ** -0.5) * float(np.log2(np.e))
VMEM_LIMIT = 52 * 1024 * 1024

F32 = jnp.float32
BF16 = jnp.bfloat16
NEG_INF = float("-inf")


def _dot(a, b):
    return jnp.dot(a.astype(BF16), b.astype(BF16), preferred_element_type=F32)


def _dot_nt(a, b):
    return lax.dot_general(a.astype(BF16), b.astype(BF16), (((1,), (1,)), ((), ())),
                           preferred_element_type=F32)


def _dot_tn(a, b):
    return lax.dot_general(a.astype(BF16), b.astype(BF16), (((0,), (0,)), ((), ())),
                           preferred_element_type=F32)


def _sigmoid(x):
    return 1.0 / (1.0 + jnp.exp(-x))


def _silu(x):
    return x * _sigmoid(x)


def _softplus(x):
    return jnp.maximum(x, 0.0) + jnp.log1p(jnp.exp(-jnp.abs(x)))


def _in_proj_kernel(x_ref, nw_ref, wm_ref, ws_ref, wst_ref, q_ref, k_ref, v_ref, rest_ref, small_ref,
                    smallt_ref, n_scr):
    j = pl.program_id(1)
    tm = x_ref.shape[0]

    @pl.when(j == 0)
    def _():
        x = x_ref[...]
        var = jnp.mean(x * x, axis=-1, keepdims=True)
        n = ((x * lax.rsqrt(var + NORM_EPS)) * nw_ref[...]).astype(BF16)
        n_scr[...] = n
        small_ref[...] = jnp.dot(n, ws_ref[...], preferred_element_type=F32)
        smallt_ref[...] = lax.dot_general(wst_ref[...], n, (((1,), (1,)), ((), ())),
                                          preferred_element_type=F32)

    for blk, tok_ref in enumerate((q_ref, k_ref, v_ref)):
        @pl.when(j == blk)
        def _(tok_ref=tok_ref):
            res = jnp.dot(n_scr[...], wm_ref[...], preferred_element_type=F32)
            for h in range(DN_HEADS):
                tok_ref[pl.ds(h, tm, stride=DN_HEADS), :] = res[:, h * DN_HEAD_DIM:(h + 1) * DN_HEAD_DIM]

    @pl.when(j >= DN_QKV_BLOCKS)
    def _():
        rest_ref[...] = jnp.dot(n_scr[...], wm_ref[...], preferred_element_type=F32)


def _in_proj(x2d, norm_w, w_main, w_small, w_small_t, tm):
    t, d = x2d.shape
    n_main = w_main.shape[1]
    tn = D_MODEL
    assert tn == DN_WIDTH
    grid = (t // tm, n_main // tn)
    tok_spec = pl.BlockSpec((tm * DN_HEADS, DN_HEAD_DIM), lambda i, j: (i, 0))
    tok_shape = jax.ShapeDtypeStruct((t * DN_HEADS, DN_HEAD_DIM), F32)
    return pl.pallas_call(
        _in_proj_kernel,
        grid=grid,
        in_specs=[
            pl.BlockSpec((tm, d), lambda i, j: (i, 0)),
            pl.BlockSpec((1, d), lambda i, j: (0, 0)),
            pl.BlockSpec((d, tn), lambda i, j: (0, j)),
            pl.BlockSpec((d, SMALL_W), lambda i, j: (0, 0)),
            pl.BlockSpec((2 * DN_HEADS, d), lambda i, j: (0, 0)),
        ],
        out_specs=[
            tok_spec, tok_spec, tok_spec,
            pl.BlockSpec((tm, tn), lambda i, j: (i, jnp.maximum(j - DN_QKV_BLOCKS, 0))),
            pl.BlockSpec((tm, SMALL_W), lambda i, j: (i, 0)),
            pl.BlockSpec((2 * DN_HEADS, tm), lambda i, j: (0, i)),
        ],
        out_shape=[
            tok_shape, tok_shape, tok_shape,
            jax.ShapeDtypeStruct((t, n_main - DN_QKV_BLOCKS * tn), F32),
            jax.ShapeDtypeStruct((t, SMALL_W), F32),
            jax.ShapeDtypeStruct((2 * DN_HEADS, t), F32),
        ],
        scratch_shapes=[pltpu.VMEM((tm, d), BF16)],
        compiler_params=pltpu.CompilerParams(
            dimension_semantics=("parallel", "arbitrary"), vmem_limit_bytes=VMEM_LIMIT),
        name="in_proj",
    )(x2d, norm_w, w_main, w_small, w_small_t)


def _chunk_cumsum(x, axis):
    idx = lax.broadcasted_iota(jnp.int32, x.shape, axis) % DN_CHUNK
    shift = 1
    while shift < DN_CHUNK:
        x = x + jnp.where(idx >= shift, pltpu.roll(x, shift, axis), 0.0)
        shift *= 2
    return x


def _unit_lower_inverse(mats, row, col):
    c = mats[0].shape[0]
    eye = (row == col).astype(F32)
    blk = 8
    diag_blk = (row // blk) == (col // blk)
    n = [jnp.where(diag_blk, -a, 0.0) for a in mats]
    n2 = [_dot(x, x) for x in n]
    n4 = [_dot(x, x) for x in n2]
    t = [_dot(eye + x, eye + y) for x, y in zip(n, n2)]
    t = [_dot(x, eye + y) for x, y in zip(t, n4)]
    while blk < c:
        off = ((row // (2 * blk)) == (col // (2 * blk))) & ((row // blk) != (col // blk))
        a_t = [_dot(jnp.where(off, a, 0.0), x) for a, x in zip(mats, t)]
        t = [x - _dot(x, y) for x, y in zip(t, a_t)]
        blk *= 2
    return t


def _deltanet_kernel(q_ref, k_ref, v_ref, qh_ref, kh_ref, vh_ref, z_ref, small_ref, smallt_ref,
                     cwq_ref, cwk_ref, cwv_ref, alog_r_ref, dtb_r_ref, alog_c_ref, dtb_c_ref,
                     normw_ref, out_ref,
                     state_scr, qc_scr, kc_scr, vc_scr, u_scr, wq_scr, kd_scr,
                     qk_scr, gl_scr, bcol_scr, gcol_scr, grow_scr):
    tc = q_ref.shape[0]
    n_chunks = tc // DN_CHUNK
    first = pl.program_id(1) == 0
    heads = range(DN_HEADS)
    cols = [slice(h * DN_HEAD_DIM, (h + 1) * DN_HEAD_DIM) for h in heads]

    @pl.when(first)
    def _():
        state_scr[...] = jnp.zeros_like(state_scr)

    def conv_silu_norm(x_ref, halo_ref, cw_ref, dst, l2norm, scale):
        halo = jnp.where(first, 0.0, halo_ref[...])
        xp = jnp.concatenate([halo[SUBLANES - (DN_CONV - 1):], x_ref[...]], axis=0)
        y = xp[DN_CONV - 1:DN_CONV - 1 + tc] * cw_ref[DN_CONV - 1]
        for j in range(DN_CONV - 1):
            y = y + xp[j:j + tc] * cw_ref[j]
        y = _silu(y)
        if l2norm:
            y = y * lax.rsqrt(jnp.sum(y * y, axis=-1, keepdims=True) + NORM_EPS)
            if scale is not None:
                y = y * scale
        dst[...] = y.reshape(tc * DN_HEADS, DN_HEAD_DIM)

    conv_silu_norm(q_ref, qh_ref, cwq_ref, qc_scr, True, DN_HEAD_DIM ** -0.5)
    conv_silu_norm(k_ref, kh_ref, cwk_ref, kc_scr, True, None)
    conv_silu_norm(v_ref, vh_ref, cwv_ref, vc_scr, False, None)

    def head_rows(scr, c, h):
        return scr[pl.ds(c * DN_CHUNK * DN_HEADS + h, DN_CHUNK, stride=DN_HEADS), :]

    small = small_ref[...]
    bcol_scr[...] = _sigmoid(small)
    gcol_scr[...] = _chunk_cumsum(-jnp.exp(alog_r_ref[...]) * _softplus(small + dtb_r_ref[...]), 0)
    small_t = smallt_ref[...]
    g_t = _chunk_cumsum(-jnp.exp(alog_c_ref[...]) * _softplus(small_t + dtb_c_ref[...]), 1)
    for c in range(n_chunks):
        grow_scr[c] = g_t[:, c * DN_CHUNK:(c + 1) * DN_CHUNK]

    row = lax.broadcasted_iota(jnp.int32, (DN_CHUNK, DN_CHUNK), 0)
    col = lax.broadcasted_iota(jnp.int32, (DN_CHUNK, DN_CHUNK), 1)
    lower_incl = row >= col
    lower_strict = row > col

    def rows_at(start, size):
        if isinstance(start, int):
            return pl.ds(start, size)
        return pl.ds(pl.multiple_of(start, DN_CHUNK), size)

    def prep(c0, chunks):
        items = [(c0 + i, h) for i in range(chunks) for h in heads]
        rows = [rows_at(c * DN_CHUNK, DN_CHUNK) for c, _ in items]
        g_rows = [grow_scr[c0 + i] for i in range(chunks)]
        q = [head_rows(qc_scr, c, h) for c, h in items]
        k = [head_rows(kc_scr, c, h) for c, h in items]
        beta = [bcol_scr[r, h:h + 1] for r, (_, h) in zip(rows, items)]
        gc_b = [jnp.broadcast_to(gcol_scr[r, DN_HEADS + h:DN_HEADS + h + 1],
                                 (DN_CHUNK, DN_HEAD_DIM)) for r, (_, h) in zip(rows, items)]
        eg_b = [jnp.exp(x) for x in gc_b]
        decay = [jnp.exp(jnp.where(
            lower_incl,
            gc_b[n][:, :DN_CHUNK] - g_rows[n // DN_HEADS][DN_HEADS + h:DN_HEADS + h + 1, :],
            NEG_INF)) for n, (_, h) in enumerate(items)]
        kb = [x * y for x, y in zip(k, beta)]
        kq = [_dot_nt(jnp.concatenate([kb[n], q[n]], axis=0), k[n]) for n in range(len(items))]
        strict = [kq[n][:DN_CHUNK] * jnp.where(lower_strict, decay[n], 0.0)
                  for n in range(len(items))]
        for n, (c, h) in enumerate(items):
            qk_scr[h, rows[n], :] = (kq[n][DN_CHUNK:] * decay[n]).astype(BF16)
        t_mat = _unit_lower_inverse(strict, row, col)
        uw = [_dot(t_mat[n], jnp.concatenate([head_rows(vc_scr, c, h) * beta[n], kb[n] * eg_b[n]],
                                             axis=1)) for n, (c, h) in enumerate(items)]
        for n, (c, h) in enumerate(items):
            u_scr[rows[n], cols[h]] = uw[n][:, :DN_HEAD_DIM]
            wq_scr[rows_at(c * 2 * DN_CHUNK, DN_CHUNK), cols[h]] = uw[n][:, DN_HEAD_DIM:].astype(BF16)
            wq_scr[rows_at(c * 2 * DN_CHUNK + DN_CHUNK, DN_CHUNK), cols[h]] = (
                q[n] * eg_b[n]).astype(BF16)
            g_last_b = jnp.broadcast_to(gc_b[n][DN_CHUNK - 1:DN_CHUNK, :], (DN_CHUNK, DN_HEAD_DIM))
            kd_scr[rows[n], cols[h]] = (k[n] * jnp.exp(g_last_b - gc_b[n])).astype(BF16)
            gl_scr[c, h:h + 1, :] = eg_b[n][DN_CHUNK - 1:DN_CHUNK, :]

    prep(0, n_chunks)

    normw = normw_ref[...]

    def scan(c, carry):
        rows = pl.ds(pl.multiple_of(c * DN_CHUNK, DN_CHUNK), DN_CHUNK)
        wq_rows = pl.ds(pl.multiple_of(c * 2 * DN_CHUNK, 2 * DN_CHUNK), 2 * DN_CHUNK)
        gl_all = gl_scr[c]
        state = [state_scr[h] for h in heads]
        state_b = [x.astype(BF16) for x in state]
        ws_qs = [jnp.dot(wq_scr[wq_rows, cols[h]], state_b[h], preferred_element_type=F32)
                 for h in heads]
        v_new_b = [(u_scr[rows, cols[h]] - ws_qs[h][:DN_CHUNK]).astype(BF16) for h in heads]
        intra = [jnp.dot(qk_scr[h, rows, :], v_new_b[h], preferred_element_type=F32) for h in heads]
        d_state = [lax.dot_general(kd_scr[rows, cols[h]], v_new_b[h], (((0,), (0,)), ((), ())),
                                   preferred_element_type=F32) for h in heads]
        for h in heads:
            state_scr[h] = state[h] * gl_all[h:h + 1, :] + d_state[h]
            o = ws_qs[h][DN_CHUNK:] + intra[h]
            var = jnp.mean(o * o, axis=-1, keepdims=True)
            o = (o * lax.rsqrt(var + NORM_EPS)) * normw
            out_ref[rows, cols[h]] = (o * _silu(z_ref[rows, cols[h]])).astype(out_ref.dtype)
        return carry

    lax.fori_loop(0, n_chunks, scan, 0)


def _deltanet(q4d, k4d, v4d, rest3d, small3d, small_t3, conv_w, alog_r, dtb_r, alog_c, dtb_c, norm_w, tc):
    b_sz, s_len, _ = rest3d.shape
    n_t = s_len // tc
    n_chunks = tc // DN_CHUNK
    halo_blocks = tc // SUBLANES

    conv_w3 = conv_w.reshape(DN_CONV, conv_w.shape[1] // DN_HEAD_DIM, DN_HEAD_DIM)

    tile_spec = pl.BlockSpec((None, tc, DN_HEADS, DN_HEAD_DIM), lambda b, i: (b, i, 0, 0))
    halo_spec = pl.BlockSpec((None, SUBLANES, DN_HEADS, DN_HEAD_DIM),
                             lambda b, i: (b, jnp.maximum(i * halo_blocks - 1, 0), 0, 0))

    def const_spec(shape):
        return pl.BlockSpec(shape, lambda b, i: tuple(0 for _ in shape))

    in_specs = [
        tile_spec, tile_spec, tile_spec,
        halo_spec, halo_spec, halo_spec,
        pl.BlockSpec((None, tc, DN_WIDTH), lambda b, i: (b, i, 0)),
        pl.BlockSpec((None, tc, SMALL_W), lambda b, i: (b, i, 0)),
        pl.BlockSpec((2 * DN_HEADS, tc), lambda b, i: (0, b * n_t + i)),
        pl.BlockSpec((DN_CONV, DN_HEADS, DN_HEAD_DIM), lambda b, i: (0, 0, 0)),
        pl.BlockSpec((DN_CONV, DN_HEADS, DN_HEAD_DIM), lambda b, i: (0, 1, 0)),
        pl.BlockSpec((DN_CONV, DN_HEADS, DN_HEAD_DIM), lambda b, i: (0, 2, 0)),
        const_spec((1, SMALL_W)), const_spec((1, SMALL_W)),
        const_spec((2 * DN_HEADS, 1)), const_spec((2 * DN_HEADS, 1)),
        const_spec((1, DN_HEAD_DIM)),
    ]
    scratch = [
        pltpu.VMEM((DN_HEADS, DN_HEAD_DIM, DN_HEAD_DIM), F32),
        pltpu.VMEM((tc * DN_HEADS, DN_HEAD_DIM), F32),
        pltpu.VMEM((tc * DN_HEADS, DN_HEAD_DIM), F32),
        pltpu.VMEM((tc * DN_HEADS, DN_HEAD_DIM), F32),
        pltpu.VMEM((tc, DN_WIDTH), F32),
        pltpu.VMEM((2 * tc, DN_WIDTH), BF16),
        pltpu.VMEM((tc, DN_WIDTH), BF16),
        pltpu.VMEM((DN_HEADS, tc, DN_CHUNK), BF16),
        pltpu.VMEM((n_chunks, DN_HEADS, DN_HEAD_DIM), F32),
        pltpu.VMEM((tc, SMALL_W), F32),
        pltpu.VMEM((tc, SMALL_W), F32),
        pltpu.VMEM((n_chunks, 2 * DN_HEADS, DN_CHUNK), F32),
    ]
    return pl.pallas_call(
        _deltanet_kernel,
        grid=(b_sz, n_t),
        in_specs=in_specs,
        out_specs=pl.BlockSpec((None, tc, DN_WIDTH), lambda b, i: (b, i, 0)),
        out_shape=jax.ShapeDtypeStruct((b_sz, s_len, DN_WIDTH), BF16),
        scratch_shapes=scratch,
        compiler_params=pltpu.CompilerParams(
            dimension_semantics=("parallel", "arbitrary"), vmem_limit_bytes=VMEM_LIMIT),
        name="deltanet",
    )(q4d, k4d, v4d, q4d, k4d, v4d, rest3d, small3d, small_t3,
      conv_w3, conv_w3, conv_w3, alog_r, dtb_r, alog_c, dtb_c, norm_w)


def _moba_prep_kernel(q_ref, k_ref, v_ref, cos_ref, sin_ref, qat_ref, ka_ref, vt_ref, kmean_scr):
    blk = pl.program_id(1)
    n_rows = q_ref.shape[0]

    @pl.when(blk == 0)
    def _():
        kmean_scr[...] = jnp.zeros_like(kmean_scr)

    cos = cos_ref[...]
    sin = sin_ref[...]
    half = MOBA_HEAD_DIM // 2
    lane = lax.broadcasted_iota(jnp.int32, (n_rows, MOBA_HEAD_DIM), 1)
    onehot = (lane == blk).astype(BF16)
    blk_row = lax.broadcasted_iota(jnp.int32, (MOBA_HEAD_DIM, n_rows), 0)
    blk_row_f = blk_row.astype(F32)
    mean_row = lax.broadcasted_iota(jnp.int32, (MOBA_HEAD_DIM, MOBA_HEAD_DIM), 0)
    ones_rows = (lax.broadcasted_iota(jnp.int32, (BF16_ROWS, n_rows), 0) == 0).astype(BF16)

    for h in range(MOBA_HEADS):
        cols = slice(h * MOBA_HEAD_DIM, (h + 1) * MOBA_HEAD_DIM)
        q = q_ref[:, cols]
        k = k_ref[:, cols]
        q = q * cos + pltpu.roll(q, half, 1) * sin
        k = k * cos + pltpu.roll(k, half, 1) * sin
        q_t = q.T

        kmean = kmean_scr[h]
        gate = jnp.dot(kmean, q_t, precision=lax.Precision.HIGHEST, preferred_element_type=F32)
        gate = jnp.where(blk_row < blk, gate, NEG_INF)
        sel = blk_row == blk
        for _ in range(MOBA_TOPK):
            mx = jnp.max(gate, axis=0, keepdims=True)
            first_idx = jnp.min(jnp.where(gate == mx, blk_row_f, float(MOBA_HEAD_DIM)),
                                axis=0, keepdims=True)
            hit = blk_row_f == first_idx
            sel = sel | (hit & (mx > NEG_INF))
            gate = jnp.where(hit, NEG_INF, gate)
        qat_ref[h * AUG_W:h * AUG_W + MOBA_HEAD_DIM, :] = (q_t * MOBA_EXP2_SCALE).astype(BF16)
        qat_ref[h * AUG_W + MOBA_HEAD_DIM:(h + 1) * AUG_W, :] = jnp.where(
            sel, 0.0, MASK_PENALTY).astype(BF16)
        ka_ref[:, h * AUG_W:h * AUG_W + MOBA_HEAD_DIM] = k.astype(BF16)
        ka_ref[:, h * AUG_W + MOBA_HEAD_DIM:(h + 1) * AUG_W] = onehot
        vt_ref[h, 0:MOBA_HEAD_DIM, :] = v_ref[:, cols].T.astype(BF16)
        vt_ref[h, MOBA_HEAD_DIM:VT_ROWS, :] = ones_rows

        k_mean_row = jnp.mean(k, axis=0, keepdims=True)
        kmean_scr[h] = jnp.where(mean_row == blk, k_mean_row, kmean)


def _moba_prep(main3d, cos_t, sin_t):
    b_sz, s_len, _ = main3d.shape
    n_blk = s_len // MOBA_BLOCK
    assert n_blk <= MOBA_HEAD_DIM

    def tile_spec(cb):
        return pl.BlockSpec((None, MOBA_BLOCK, MOBA_WIDTH), lambda b, i, cb=cb: (b, i, cb))

    tab_spec = pl.BlockSpec((MOBA_BLOCK, MOBA_HEAD_DIM), lambda b, i: (i, 0))
    return pl.pallas_call(
        _moba_prep_kernel,
        grid=(b_sz, n_blk),
        in_specs=[tile_spec(1), tile_spec(2), tile_spec(3), tab_spec, tab_spec],
        out_specs=[
            pl.BlockSpec((None, MOBA_HEADS * AUG_W, MOBA_BLOCK), lambda b, i: (b, 0, i)),
            pl.BlockSpec((None, MOBA_BLOCK, MOBA_HEADS * AUG_W), lambda b, i: (b, i, 0)),
            pl.BlockSpec((None, MOBA_HEADS, None, VT_ROWS, MOBA_BLOCK),
                         lambda b, i: (b, 0, i, 0, 0)),
        ],
        out_shape=[
            jax.ShapeDtypeStruct((b_sz, MOBA_HEADS * AUG_W, s_len), BF16),
            jax.ShapeDtypeStruct((b_sz, s_len, MOBA_HEADS * AUG_W), BF16),
            jax.ShapeDtypeStruct((b_sz, MOBA_HEADS, n_blk, VT_ROWS, MOBA_BLOCK), BF16),
        ],
        scratch_shapes=[pltpu.VMEM((MOBA_HEADS, MOBA_HEAD_DIM, MOBA_HEAD_DIM), F32)],
        compiler_params=pltpu.CompilerParams(
            dimension_semantics=("parallel", "arbitrary"), vmem_limit_bytes=VMEM_LIMIT),
        name="moba_prep",
    )(main3d, main3d, main3d, cos_t, sin_t)


def _moba_attn_kernel(qat_ref, ka_ref, vt_ref, out_ref, s_scr, p_scr, acc_scr, *, heads_per_step):
    n_q = qat_ref.shape[1]
    blocks_per_tile = n_q // MOBA_BLOCK
    first_blk = pl.program_id(2) * blocks_per_tile
    heads = range(heads_per_step)
    q_t = [qat_ref[h * AUG_W:(h + 1) * AUG_W, :] for h in heads]

    def scores(j):
        rows = pl.ds(pl.multiple_of(j * MOBA_BLOCK, MOBA_BLOCK), MOBA_BLOCK)
        return [jnp.dot(ka_ref[rows, h * AUG_W:(h + 1) * AUG_W], q_t[h],
                        preferred_element_type=F32) for h in heads]

    key = lax.broadcasted_iota(jnp.int32, (MOBA_BLOCK, n_q), 0)
    qry = lax.broadcasted_iota(jnp.int32, (MOBA_BLOCK, n_q), 1)
    m = None
    for r in range(blocks_per_tile):
        s = [jnp.where(key + r * MOBA_BLOCK <= qry, x, NEG_INF) for x in scores(first_blk + r)]
        if r == 0:
            m = [jnp.max(x, axis=0, keepdims=True) for x in s]
            for h in heads:
                acc_scr[h] = jnp.dot(vt_ref[h, first_blk], jnp.exp2(s[h] - m[h]).astype(BF16),
                                     preferred_element_type=F32)
        else:
            m_new = [jnp.maximum(m[h], jnp.max(s[h], axis=0, keepdims=True)) for h in heads]
            for h in heads:
                acc_scr[h] = (jnp.exp2(m[h] - m_new[h]) * acc_scr[h]
                              + jnp.dot(vt_ref[h, first_blk + r],
                                        jnp.exp2(s[h] - m_new[h]).astype(BF16),
                                        preferred_element_type=F32))
            m = m_new

    s_first = scores(0)
    for h in heads:
        s_scr[0, h] = s_first[h]
        p_scr[0, h] = jnp.zeros(p_scr.shape[2:], BF16)
    alpha = [jnp.ones_like(x) for x in m]

    def half_trip(j, cur, nxt, m, alpha):
        s_next = scores(jnp.minimum(j + 1, first_blk - 1))
        j_prev = jnp.maximum(j - 1, 0)
        for h in heads:
            acc_scr[h] = alpha[h] * acc_scr[h] + jnp.dot(vt_ref[h, j_prev], p_scr[cur, h],
                                                         preferred_element_type=F32)
        s_cur = [s_scr[cur, h] for h in heads]
        m_new = [jnp.maximum(m[h], jnp.max(s_cur[h], axis=0, keepdims=True)) for h in heads]
        for h in heads:
            p_scr[nxt, h] = jnp.exp2(s_cur[h] - m_new[h]).astype(BF16)
            s_scr[nxt, h] = s_next[h]
        alpha = [jnp.exp2(m[h] - m_new[h]) for h in heads]
        return m_new, alpha

    def body(jj, carry):
        m, alpha = carry
        m, alpha = half_trip(2 * jj, 0, 1, m, alpha)
        return half_trip(2 * jj + 1, 1, 0, m, alpha)

    assert blocks_per_tile % 2 == 0
    m, alpha = lax.fori_loop(0, first_blk // 2, body, (m, alpha))
    j_last = jnp.maximum(first_blk - 1, 0)
    for h in heads:
        a = alpha[h] * acc_scr[h] + jnp.dot(vt_ref[h, j_last], p_scr[0, h],
                                            preferred_element_type=F32)
        o_t = a[:MOBA_HEAD_DIM] / a[MOBA_HEAD_DIM:MOBA_HEAD_DIM + 1]
        out_ref[:, h * MOBA_HEAD_DIM:(h + 1) * MOBA_HEAD_DIM] = o_t.T.astype(out_ref.dtype)


def _moba_attn(qat, ka, vt, heads_per_step, q_tile):
    b_sz, s_len, _ = ka.shape
    n_blk = s_len // MOBA_BLOCK
    hb = heads_per_step
    resident = pl.Buffered(1)
    return pl.pallas_call(
        functools.partial(_moba_attn_kernel, heads_per_step=hb),
        grid=(b_sz, MOBA_HEADS // hb, s_len // q_tile),
        in_specs=[
            pl.BlockSpec((None, hb * AUG_W, q_tile), lambda b, g, i: (b, g, i)),
            pl.BlockSpec((None, s_len, hb * AUG_W), lambda b, g, i: (b, 0, g),
                         pipeline_mode=resident),
            pl.BlockSpec((None, hb, n_blk, VT_ROWS, MOBA_BLOCK),
                         lambda b, g, i: (b, g, 0, 0, 0), pipeline_mode=resident),
        ],
        out_specs=pl.BlockSpec((None, q_tile, hb * MOBA_HEAD_DIM), lambda b, g, i: (b, i, g)),
        out_shape=jax.ShapeDtypeStruct((b_sz, s_len, MOBA_WIDTH), BF16),
        scratch_shapes=[pltpu.VMEM((2, hb, MOBA_BLOCK, q_tile), F32),
                        pltpu.VMEM((2, hb, MOBA_BLOCK, q_tile), BF16),
                        pltpu.VMEM((hb, VT_ROWS, q_tile), F32)],
        compiler_params=pltpu.CompilerParams(
            dimension_semantics=("parallel", "parallel", "arbitrary"),
            vmem_limit_bytes=VMEM_LIMIT),
        name="moba_attn",
    )(qat, ka, vt)


def _out_proj_kernel(x_ref, odn_ref, omb_ref, gdn_ref, gmb_ref, wdn_ref, wmb_ref, wo_ref, h_ref):
    y_dn = jnp.dot(odn_ref[...], wdn_ref[...], preferred_element_type=F32)
    y_mb = jnp.dot(omb_ref[...], wmb_ref[...], preferred_element_type=F32)
    merged = _sigmoid(gdn_ref[...]) * y_dn + _sigmoid(gmb_ref[...]) * y_mb
    h_ref[...] = x_ref[...] + jnp.dot(merged.astype(BF16), wo_ref[...], preferred_element_type=F32)


def _out_proj(x2d, o_dn, o_mb, main2d, w_dn, w_mb, w_o, tm):
    t, d = x2d.shape
    row_spec = pl.BlockSpec((tm, d), lambda i: (i, 0))
    w_spec = pl.BlockSpec((d, d), lambda i: (0, 0))
    return pl.pallas_call(
        _out_proj_kernel,
        grid=(t // tm,),
        in_specs=[row_spec, row_spec, row_spec,
                  pl.BlockSpec((tm, d), lambda i: (i, 4)),
                  pl.BlockSpec((tm, d), lambda i: (i, 5)),
                  w_spec, w_spec, w_spec],
        out_specs=row_spec,
        out_shape=jax.ShapeDtypeStruct((t, d), F32),
        compiler_params=pltpu.CompilerParams(
            dimension_semantics=("parallel",), vmem_limit_bytes=VMEM_LIMIT),
        name="out_proj",
    )(x2d, o_dn, o_mb, main2d, main2d, w_dn, w_mb, w_o)


def _conv_ffn_kernel(h_ref, nw_ref, wup_ref, cw_ref, cb_ref, wdown_ref, fw_ref, out_ref,
                     u_scr, *, tiles_per_seq, n_split):
    tm = h_ref.shape[0]
    seq_start = (pl.program_id(0) % tiles_per_seq) == 0
    h = h_ref[...]
    var = jnp.mean(h * h, axis=-1, keepdims=True)
    n = ((h * lax.rsqrt(var + NORM_EPS)) * nw_ref[...]).astype(BF16)

    width = D_FF // n_split
    base = SUBLANES - (FFN_CONV - 1)
    acc = jnp.zeros((tm, D_MODEL), F32)
    for part in range(n_split):
        halves = []
        for half in range(2):
            c0 = half * D_FF + part * width
            slot = 2 * part + half
            @pl.when(seq_start)
            def _(slot=slot):
                u_scr[slot, 0:SUBLANES, :] = jnp.zeros((SUBLANES, width), F32)

            @pl.when(jnp.logical_not(seq_start))
            def _(slot=slot):
                u_scr[slot, 0:SUBLANES, :] = u_scr[slot, tm:tm + SUBLANES, :]

            u_scr[slot, SUBLANES:SUBLANES + tm, :] = jnp.dot(
                n, wup_ref[:, c0:c0 + width], preferred_element_type=F32)
            y = (u_scr[slot, SUBLANES:SUBLANES + tm, :]
                 * cw_ref[FFN_CONV - 1:FFN_CONV, c0:c0 + width])
            for j in range(FFN_CONV - 1):
                y = y + u_scr[slot, base + j:base + j + tm, :] * cw_ref[j:j + 1, c0:c0 + width]
            halves.append(y + cb_ref[:, c0:c0 + width])
        act = (_silu(halves[0]) * halves[1]).astype(BF16)
        acc = acc + jnp.dot(act, wdown_ref[part * width:(part + 1) * width, :],
                            preferred_element_type=F32)
    h2 = h + acc
    var2 = jnp.mean(h2 * h2, axis=-1, keepdims=True)
    out_ref[...] = (h2 * lax.rsqrt(var2 + NORM_EPS)) * fw_ref[...]


def _conv_ffn(h2d, norm_w, w_up, conv_w, conv_b, w_down, final_w, tm, s_len, n_split):
    t, d = h2d.shape
    width = D_FF // n_split
    kern = functools.partial(_conv_ffn_kernel, tiles_per_seq=s_len // tm, n_split=n_split)

    def const_spec(shape):
        return pl.BlockSpec(shape, lambda i: (0, 0), pipeline_mode=pl.Buffered(1))

    return pl.pallas_call(
        kern,
        grid=(t // tm,),
        in_specs=[
            pl.BlockSpec((tm, d), lambda i: (i, 0)),
            const_spec((1, d)),
            const_spec((d, 2 * D_FF)),
            const_spec((FFN_CONV, 2 * D_FF)),
            const_spec((1, 2 * D_FF)),
            const_spec((D_FF, d)),
            const_spec((1, d)),
        ],
        out_specs=pl.BlockSpec((tm, d), lambda i: (i, 0)),
        out_shape=jax.ShapeDtypeStruct((t, d), F32),
        scratch_shapes=[pltpu.VMEM((2 * n_split, tm + SUBLANES, width), F32)],
        compiler_params=pltpu.CompilerParams(
            dimension_semantics=("arbitrary",), vmem_limit_bytes=VMEM_LIMIT),
        name="conv_ffn",
    )(h2d, norm_w, w_up, conv_w, conv_b, w_down, final_w)


def _rope_tables(s_len):
    half = MOBA_HEAD_DIM // 2
    inv_freq = np.power(ROPE_THETA, -np.arange(half, dtype=np.float64) / half)
    ang = np.arange(s_len, dtype=np.float64)[:, None] * inv_freq[None, :]
    cos, sin = np.cos(ang), np.sin(ang)
    return (jnp.asarray(np.concatenate([cos, cos], axis=-1), F32),
            jnp.asarray(np.concatenate([-sin, sin], axis=-1), F32))


def _pick_tile(n, want):
    t = min(n, want)
    while n % t:
        t //= 2
    return t


def _layer(x, attn_norm_w, w_in, dn_conv_w, dn_a_log, dn_dt_bias, dn_norm_w, w_dn_out,
           w_moba_out, w_o, ffn_norm_w, w_up, ffn_conv_w, ffn_conv_b, w_down, final_norm_w):
    b_sz, s_len, d = x.shape
    t = b_sz * s_len
    x2d = x.reshape(t, d)

    small_lo = 4 * DN_WIDTH
    small_hi = small_lo + 2 * DN_HEADS
    w_main = jnp.concatenate([w_in[:, :small_lo], w_in[:, small_hi:]], axis=1).astype(BF16)
    w_small_cols = w_in[:, small_lo:small_hi]
    w_small = jnp.pad(w_small_cols, ((0, 0), (0, SMALL_W - 2 * DN_HEADS))).astype(BF16)
    w_small_t = w_small_cols.T.astype(BF16)

    q_tok, k_tok, v_tok, rest, small, small_t = _in_proj(
        x2d, attn_norm_w.reshape(1, d), w_main, w_small, w_small_t, _pick_tile(t, 1024))
    tok4d = (b_sz, s_len, DN_HEADS, DN_HEAD_DIM)
    main3d = rest.reshape(b_sz, s_len, rest.shape[1])
    small3d = small.reshape(b_sz, s_len, SMALL_W)

    pad_r = ((0, 0), (DN_HEADS, SMALL_W - 2 * DN_HEADS))
    alog_r = jnp.pad(dn_a_log.reshape(1, DN_HEADS), pad_r)
    dtb_r = jnp.pad(dn_dt_bias.reshape(1, DN_HEADS), pad_r)
    alog_c = jnp.pad(dn_a_log.reshape(DN_HEADS, 1), ((DN_HEADS, 0), (0, 0)))
    dtb_c = jnp.pad(dn_dt_bias.reshape(DN_HEADS, 1), ((DN_HEADS, 0), (0, 0)))
    o_dn = _deltanet(q_tok.reshape(tok4d), k_tok.reshape(tok4d), v_tok.reshape(tok4d), main3d,
                     small3d, small_t, dn_conv_w, alog_r, dtb_r, alog_c, dtb_c,
                     dn_norm_w.reshape(1, DN_HEAD_DIM), _pick_tile(s_len, 256))

    cos_t, sin_t = _rope_tables(s_len)
    qat, ka, vt = _moba_prep(main3d, cos_t, sin_t)
    o_mb = _moba_attn(qat, ka, vt, 4, _pick_tile(s_len, 2 * MOBA_BLOCK))

    h = _out_proj(x2d, o_dn.reshape(t, DN_WIDTH), o_mb.reshape(t, MOBA_WIDTH), rest,
                  w_dn_out.astype(BF16), w_moba_out.astype(BF16), w_o.astype(BF16),
                  _pick_tile(t, 1024))

    tm_ffn = _pick_tile(s_len, 512)
    out = _conv_ffn(h, ffn_norm_w.reshape(1, d), w_up.astype(BF16), ffn_conv_w,
                    ffn_conv_b.reshape(1, 2 * D_FF), w_down.astype(BF16),
                    final_norm_w.reshape(1, d), tm_ffn, s_len, 1)
    return out.reshape(b_sz, s_len, d)


def kernel(x, attn_norm_w, w_in, dn_conv_w, dn_A_log, dn_dt_bias, dn_norm_w, w_dn_out, w_moba_out,
           w_o, ffn_norm_w, w_up, ffn_conv_w, ffn_conv_b, w_down, final_norm_w):
    depth = w_in.shape[0]
    assert depth == 1, "the final RMSNorm is fused into the layer's conv_ffn call"
    return _layer(x, attn_norm_w[0], w_in[0], dn_conv_w[0], dn_A_log[0], dn_dt_bias[0],
                  dn_norm_w[0], w_dn_out[0], w_moba_out[0], w_o[0], ffn_norm_w[0], w_up[0],
                  ffn_conv_w[0], ffn_conv_b[0], w_down[0], final_norm_w)
```

```python
import functools

import jax
import jax.numpy as jnp
import numpy as np
from jax import lax
from jax.experimental import pallas as pl
from jax.experimental.pallas import tpu as pltpu

D_MODEL = 1024
DN_HEADS = 8
DN_HEAD_DIM = 128
DN_WIDTH = DN_HEADS * DN_HEAD_DIM
DN_CONV = 4
DN_CHUNK = 64
MOBA_HEADS = 8
MOBA_HEAD_DIM = 128
MOBA_WIDTH = MOBA_HEADS * MOBA_HEAD_DIM
MOBA_BLOCK = 256
MOBA_TOPK = 3
ROPE_THETA = 10000.0
D_FF = 2816
FFN_CONV = 3
NORM_EPS = 1e-6

SMALL_W = 128
DN_QKV_BLOCKS = 3
SUBLANES = 8
AUG_W = 2 * MOBA_HEAD_DIM
MASK_PENALTY = -(2.0 ** 100)
BF16_ROWS = 16
VT_ROWS = MOBA_HEAD_DIM + BF16_ROWS
MOBA_EXP2_SCALE = (MOBA_HEAD_DIM ** -0.5) * float(np.log2(np.e))
VMEM_LIMIT = 52 * 1024 * 1024

F32 = jnp.float32
BF16 = jnp.bfloat16
NEG_INF = float("-inf")


def _dot(a, b):
    return jnp.dot(a.astype(BF16), b.astype(BF16), preferred_element_type=F32)


def _dot_nt(a, b):
    return lax.dot_general(a.astype(BF16), b.astype(BF16), (((1,), (1,)), ((), ())),
                           preferred_element_type=F32)


def _dot_tn(a, b):
    return lax.dot_general(a.astype(BF16), b.astype(BF16), (((0,), (0,)), ((), ())),
                           preferred_element_type=F32)


def _sigmoid(x):
    return 1.0 / (1.0 + jnp.exp(-x))


def _silu(x):
    return x * _sigmoid(x)


def _softplus(x):
    return jnp.maximum(x, 0.0) + jnp.log1p(jnp.exp(-jnp.abs(x)))


def _in_proj_kernel_body(x_ref, nw_ref, wlo_ref, whi_ref, ws_ref, wst_ref, q_ref, k_ref, v_ref,
                         rest_ref, small_ref, smallt_ref, n_scr, *, n_lo):
    j = pl.program_id(1)
    tm = x_ref.shape[0]

    @pl.when(j == 0)
    def _():
        x = x_ref[...]
        var = jnp.mean(x * x, axis=-1, keepdims=True)
        n = ((x * lax.rsqrt(var + NORM_EPS)) * nw_ref[...]).astype(BF16)
        n_scr[...] = n
        small_ref[...] = jnp.dot(n, ws_ref[...], preferred_element_type=F32)
        smallt_ref[...] = lax.dot_general(wst_ref[...], n, (((1,), (1,)), ((), ())),
                                          preferred_element_type=F32)

    for blk, tok_ref in enumerate((q_ref, k_ref, v_ref)):
        @pl.when(j == blk)
        def _(tok_ref=tok_ref):
            res = jnp.dot(n_scr[...], wlo_ref[...], preferred_element_type=F32)
            for h in range(DN_HEADS):
                tok_ref[pl.ds(h, tm, stride=DN_HEADS), :] = res[:, h * DN_HEAD_DIM:(h + 1) * DN_HEAD_DIM]

    @pl.when(j == n_lo - 1)
    def _():
        rest_ref[...] = jnp.dot(n_scr[...], wlo_ref[...],
                                preferred_element_type=F32).astype(rest_ref.dtype)

    @pl.when(j >= n_lo)
    def _():
        rest_ref[...] = jnp.dot(n_scr[...], whi_ref[...],
                                preferred_element_type=F32).astype(rest_ref.dtype)


def _in_proj(x2d, norm_w, w_lo, w_hi, w_small, w_small_t, tm):
    t, d = x2d.shape
    tn = D_MODEL
    assert tn == DN_WIDTH
    n_lo = w_lo.shape[1] // tn
    assert n_lo == DN_QKV_BLOCKS + 1
    n_blocks = n_lo + w_hi.shape[1] // tn
    grid = (t // tm, n_blocks)
    tok_spec = pl.BlockSpec((tm * DN_HEADS, DN_HEAD_DIM), lambda i, j: (i, 0))
    tok_shape = jax.ShapeDtypeStruct((t * DN_HEADS, DN_HEAD_DIM), F32)
    return pl.pallas_call(
        functools.partial(_in_proj_kernel_body, n_lo=n_lo),
        grid=grid,
        in_specs=[
            pl.BlockSpec((tm, d), lambda i, j: (i, 0)),
            pl.BlockSpec((1, d), lambda i, j: (0, 0)),
            pl.BlockSpec((d, tn), lambda i, j: (0, jnp.minimum(j, n_lo - 1))),
            pl.BlockSpec((d, tn), lambda i, j: (0, jnp.maximum(j - n_lo, 0))),
            pl.BlockSpec((d, SMALL_W), lambda i, j: (0, 0)),
            pl.BlockSpec((2 * DN_HEADS, d), lambda i, j: (0, 0)),
        ],
        out_specs=[
            tok_spec, tok_spec, tok_spec,
            pl.BlockSpec((tm, tn), lambda i, j: (i, jnp.maximum(j - DN_QKV_BLOCKS, 0))),
            pl.BlockSpec((tm, SMALL_W), lambda i, j: (i, 0)),
            pl.BlockSpec((2 * DN_HEADS, tm), lambda i, j: (0, i)),
        ],
        out_shape=[
            tok_shape, tok_shape, tok_shape,
            jax.ShapeDtypeStruct((t, (n_blocks - DN_QKV_BLOCKS) * tn), BF16),
            jax.ShapeDtypeStruct((t, SMALL_W), F32),
            jax.ShapeDtypeStruct((2 * DN_HEADS, t), F32),
        ],
        scratch_shapes=[pltpu.VMEM((tm, d), BF16)],
        compiler_params=pltpu.CompilerParams(
            dimension_semantics=("parallel", "arbitrary"), vmem_limit_bytes=VMEM_LIMIT),
        name="in_proj",
    )(x2d, norm_w, w_lo, w_hi, w_small, w_small_t)


def _chunk_cumsum(x, axis):
    idx = lax.broadcasted_iota(jnp.int32, x.shape, axis) % DN_CHUNK
    shift = 1
    while shift < DN_CHUNK:
        x = x + jnp.where(idx >= shift, pltpu.roll(x, shift, axis), 0.0)
        shift *= 2
    return x


def _unit_lower_inverse(mats, row, col):
    c = mats[0].shape[0]
    eye = (row == col).astype(F32)
    blk = 8
    diag_blk = (row // blk) == (col // blk)
    n = [jnp.where(diag_blk, -a, 0.0) for a in mats]
    n2 = [_dot(x, x) for x in n]
    n4 = [_dot(x, x) for x in n2]
    t = [_dot(eye + x, eye + y) for x, y in zip(n, n2)]
    t = [_dot(x, eye + y) for x, y in zip(t, n4)]
    while blk < c:
        off = ((row // (2 * blk)) == (col // (2 * blk))) & ((row // blk) != (col // blk))
        a_t = [_dot(jnp.where(off, a, 0.0), x) for a, x in zip(mats, t)]
        t = [x - _dot(x, y) for x, y in zip(t, a_t)]
        blk *= 2
    return t


def _deltanet_kernel(q_ref, k_ref, v_ref, qh_ref, kh_ref, vh_ref, z_ref, small_ref, smallt_ref,
                     cwq_ref, cwk_ref, cwv_ref, alog_r_ref, dtb_r_ref, alog_c_ref, dtb_c_ref,
                     normw_ref, out_ref,
                     state_scr, qc_scr, kc_scr, vc_scr, u_scr, wq_scr, kd_scr,
                     qk_scr, gl_scr, bcol_scr, gcol_scr, grow_scr):
    tc = q_ref.shape[0]
    n_chunks = tc // DN_CHUNK
    first = pl.program_id(1) == 0
    heads = range(DN_HEADS)
    cols = [slice(h * DN_HEAD_DIM, (h + 1) * DN_HEAD_DIM) for h in heads]

    @pl.when(first)
    def _():
        state_scr[...] = jnp.zeros_like(state_scr)

    def conv_silu_norm(x_ref, halo_ref, cw_ref, dst, l2norm, scale):
        halo = jnp.where(first, 0.0, halo_ref[...])
        xp = jnp.concatenate([halo[SUBLANES - (DN_CONV - 1):], x_ref[...]], axis=0)
        y = xp[DN_CONV - 1:DN_CONV - 1 + tc] * cw_ref[DN_CONV - 1]
        for j in range(DN_CONV - 1):
            y = y + xp[j:j + tc] * cw_ref[j]
        y = _silu(y)
        if l2norm:
            y = y * lax.rsqrt(jnp.sum(y * y, axis=-1, keepdims=True) + NORM_EPS)
            if scale is not None:
                y = y * scale
        dst[...] = y.reshape(tc * DN_HEADS, DN_HEAD_DIM)

    conv_silu_norm(q_ref, qh_ref, cwq_ref, qc_scr, True, DN_HEAD_DIM ** -0.5)
    conv_silu_norm(k_ref, kh_ref, cwk_ref, kc_scr, True, None)
    conv_silu_norm(v_ref, vh_ref, cwv_ref, vc_scr, False, None)

    def head_rows(scr, c, h):
        return scr[pl.ds(c * DN_CHUNK * DN_HEADS + h, DN_CHUNK, stride=DN_HEADS), :]

    small = small_ref[...]
    bcol_scr[...] = _sigmoid(small)
    gcol_scr[...] = _chunk_cumsum(-jnp.exp(alog_r_ref[...]) * _softplus(small + dtb_r_ref[...]), 0)
    small_t = smallt_ref[...]
    g_t = _chunk_cumsum(-jnp.exp(alog_c_ref[...]) * _softplus(small_t + dtb_c_ref[...]), 1)
    for c in range(n_chunks):
        grow_scr[c] = g_t[:, c * DN_CHUNK:(c + 1) * DN_CHUNK]

    row = lax.broadcasted_iota(jnp.int32, (DN_CHUNK, DN_CHUNK), 0)
    col = lax.broadcasted_iota(jnp.int32, (DN_CHUNK, DN_CHUNK), 1)
    lower_incl = row >= col
    lower_strict = row > col

    def rows_at(start, size):
        if isinstance(start, int):
            return pl.ds(start, size)
        return pl.ds(pl.multiple_of(start, DN_CHUNK), size)

    def prep(c0, chunks):
        items = [(c0 + i, h) for i in range(chunks) for h in heads]
        rows = [rows_at(c * DN_CHUNK, DN_CHUNK) for c, _ in items]
        g_rows = [grow_scr[c0 + i] for i in range(chunks)]
        q = [head_rows(qc_scr, c, h) for c, h in items]
        k = [head_rows(kc_scr, c, h) for c, h in items]
        beta = [bcol_scr[r, h:h + 1] for r, (_, h) in zip(rows, items)]
        gc_b = [jnp.broadcast_to(gcol_scr[r, DN_HEADS + h:DN_HEADS + h + 1],
                                 (DN_CHUNK, DN_HEAD_DIM)) for r, (_, h) in zip(rows, items)]
        eg_b = [jnp.exp(x) for x in gc_b]
        decay = [jnp.exp(jnp.where(
            lower_incl,
            gc_b[n][:, :DN_CHUNK] - g_rows[n // DN_HEADS][DN_HEADS + h:DN_HEADS + h + 1, :],
            NEG_INF)) for n, (_, h) in enumerate(items)]
        kb = [x * y for x, y in zip(k, beta)]
        kq = [_dot_nt(jnp.concatenate([kb[n], q[n]], axis=0), k[n]) for n in range(len(items))]
        strict = [kq[n][:DN_CHUNK] * jnp.where(lower_strict, decay[n], 0.0)
                  for n in range(len(items))]
        for n, (c, h) in enumerate(items):
            qk_scr[h, rows[n], :] = (kq[n][DN_CHUNK:] * decay[n]).astype(BF16)
        t_mat = _unit_lower_inverse(strict, row, col)
        uw = [_dot(t_mat[n], jnp.concatenate([head_rows(vc_scr, c, h) * beta[n], kb[n] * eg_b[n]],
                                             axis=1)) for n, (c, h) in enumerate(items)]
        for n, (c, h) in enumerate(items):
            u_scr[rows[n], cols[h]] = uw[n][:, :DN_HEAD_DIM]
            wq_scr[rows_at(c * 2 * DN_CHUNK, DN_CHUNK), cols[h]] = uw[n][:, DN_HEAD_DIM:].astype(BF16)
            wq_scr[rows_at(c * 2 * DN_CHUNK + DN_CHUNK, DN_CHUNK), cols[h]] = (
                q[n] * eg_b[n]).astype(BF16)
            g_last_b = jnp.broadcast_to(gc_b[n][DN_CHUNK - 1:DN_CHUNK, :], (DN_CHUNK, DN_HEAD_DIM))
            kd_scr[rows[n], cols[h]] = (k[n] * jnp.exp(g_last_b - gc_b[n])).astype(BF16)
            gl_scr[c, h:h + 1, :] = eg_b[n][DN_CHUNK - 1:DN_CHUNK, :]

    prep(0, n_chunks)

    normw = normw_ref[...]

    def scan(c, carry):
        rows = pl.ds(pl.multiple_of(c * DN_CHUNK, DN_CHUNK), DN_CHUNK)
        wq_rows = pl.ds(pl.multiple_of(c * 2 * DN_CHUNK, 2 * DN_CHUNK), 2 * DN_CHUNK)
        gl_all = gl_scr[c]
        state = [state_scr[h] for h in heads]
        state_b = [x.astype(BF16) for x in state]
        ws_qs = [jnp.dot(wq_scr[wq_rows, cols[h]], state_b[h], preferred_element_type=F32)
                 for h in heads]
        v_new_b = [(u_scr[rows, cols[h]] - ws_qs[h][:DN_CHUNK]).astype(BF16) for h in heads]
        intra = [jnp.dot(qk_scr[h, rows, :], v_new_b[h], preferred_element_type=F32) for h in heads]
        d_state = [lax.dot_general(kd_scr[rows, cols[h]], v_new_b[h], (((0,), (0,)), ((), ())),
                                   preferred_element_type=F32) for h in heads]
        for h in heads:
            state_scr[h] = state[h] * gl_all[h:h + 1, :] + d_state[h]
            o = ws_qs[h][DN_CHUNK:] + intra[h]
            var = jnp.mean(o * o, axis=-1, keepdims=True)
            o = (o * lax.rsqrt(var + NORM_EPS)) * normw
            out_ref[rows, cols[h]] = (o * _silu(z_ref[rows, cols[h]].astype(F32))).astype(out_ref.dtype)
        return carry

    lax.fori_loop(0, n_chunks, scan, 0)


def _deltanet(q4d, k4d, v4d, rest3d, small3d, small_t3, conv_w, alog_r, dtb_r, alog_c, dtb_c, norm_w, tc):
    b_sz, s_len, _ = rest3d.shape
    n_t = s_len // tc
    n_chunks = tc // DN_CHUNK
    halo_blocks = tc // SUBLANES

    conv_w3 = conv_w.reshape(DN_CONV, conv_w.shape[1] // DN_HEAD_DIM, DN_HEAD_DIM)

    tile_spec = pl.BlockSpec((None, tc, DN_HEADS, DN_HEAD_DIM), lambda b, i: (b, i, 0, 0))
    halo_spec = pl.BlockSpec((None, SUBLANES, DN_HEADS, DN_HEAD_DIM),
                             lambda b, i: (b, jnp.maximum(i * halo_blocks - 1, 0), 0, 0))

    def const_spec(shape):
        return pl.BlockSpec(shape, lambda b, i: tuple(0 for _ in shape))

    in_specs = [
        tile_spec, tile_spec, tile_spec,
        halo_spec, halo_spec, halo_spec,
        pl.BlockSpec((None, tc, DN_WIDTH), lambda b, i: (b, i, 0)),
        pl.BlockSpec((None, tc, SMALL_W), lambda b, i: (b, i, 0)),
        pl.BlockSpec((2 * DN_HEADS, tc), lambda b, i: (0, b * n_t + i)),
        pl.BlockSpec((DN_CONV, DN_HEADS, DN_HEAD_DIM), lambda b, i: (0, 0, 0)),
        pl.BlockSpec((DN_CONV, DN_HEADS, DN_HEAD_DIM), lambda b, i: (0, 1, 0)),
        pl.BlockSpec((DN_CONV, DN_HEADS, DN_HEAD_DIM), lambda b, i: (0, 2, 0)),
        const_spec((1, SMALL_W)), const_spec((1, SMALL_W)),
        const_spec((2 * DN_HEADS, 1)), const_spec((2 * DN_HEADS, 1)),
        const_spec((1, DN_HEAD_DIM)),
    ]
    scratch = [
        pltpu.VMEM((DN_HEADS, DN_HEAD_DIM, DN_HEAD_DIM), F32),
        pltpu.VMEM((tc * DN_HEADS, DN_HEAD_DIM), F32),
        pltpu.VMEM((tc * DN_HEADS, DN_HEAD_DIM), F32),
        pltpu.VMEM((tc * DN_HEADS, DN_HEAD_DIM), F32),
        pltpu.VMEM((tc, DN_WIDTH), F32),
        pltpu.VMEM((2 * tc, DN_WIDTH), BF16),
        pltpu.VMEM((tc, DN_WIDTH), BF16),
        pltpu.VMEM((DN_HEADS, tc, DN_CHUNK), BF16),
        pltpu.VMEM((n_chunks, DN_HEADS, DN_HEAD_DIM), F32),
        pltpu.VMEM((tc, SMALL_W), F32),
        pltpu.VMEM((tc, SMALL_W), F32),
        pltpu.VMEM((n_chunks, 2 * DN_HEADS, DN_CHUNK), F32),
    ]
    return pl.pallas_call(
        _deltanet_kernel,
        grid=(b_sz, n_t),
        in_specs=in_specs,
        out_specs=pl.BlockSpec((None, tc, DN_WIDTH), lambda b, i: (b, i, 0)),
        out_shape=jax.ShapeDtypeStruct((b_sz, s_len, DN_WIDTH), BF16),
        scratch_shapes=scratch,
        compiler_params=pltpu.CompilerParams(
            dimension_semantics=("parallel", "arbitrary"), vmem_limit_bytes=VMEM_LIMIT),
        name="deltanet",
    )(q4d, k4d, v4d, q4d, k4d, v4d, rest3d, small3d, small_t3,
      conv_w3, conv_w3, conv_w3, alog_r, dtb_r, alog_c, dtb_c, norm_w)


def _moba_prep_kernel(q_ref, k_ref, v_ref, cos_ref, sin_ref, qat_ref, ka_ref, vt_ref, kmean_scr):
    blk = pl.program_id(1)
    n_rows = q_ref.shape[0]

    @pl.when(blk == 0)
    def _():
        kmean_scr[...] = jnp.zeros_like(kmean_scr)

    cos = cos_ref[...]
    sin = sin_ref[...]
    half = MOBA_HEAD_DIM // 2
    lane = lax.broadcasted_iota(jnp.int32, (n_rows, MOBA_HEAD_DIM), 1)
    onehot = (lane == blk).astype(BF16)
    blk_row = lax.broadcasted_iota(jnp.int32, (MOBA_HEAD_DIM, n_rows), 0)
    blk_row_f = blk_row.astype(F32)
    mean_row = lax.broadcasted_iota(jnp.int32, (MOBA_HEAD_DIM, MOBA_HEAD_DIM), 0)
    ones_rows = (lax.broadcasted_iota(jnp.int32, (BF16_ROWS, n_rows), 0) == 0).astype(BF16)

    for h in range(MOBA_HEADS):
        cols = slice(h * MOBA_HEAD_DIM, (h + 1) * MOBA_HEAD_DIM)
        q = q_ref[:, cols].astype(F32)
        k = k_ref[:, cols].astype(F32)
        q = q * cos + pltpu.roll(q, half, 1) * sin
        k = k * cos + pltpu.roll(k, half, 1) * sin
        q_t = q.T

        kmean = kmean_scr[h]
        gate = jnp.dot(kmean, q_t, precision=lax.Precision.HIGHEST, preferred_element_type=F32)
        gate = jnp.where(blk_row < blk, gate, NEG_INF)
        sel = blk_row == blk
        for _ in range(MOBA_TOPK):
            mx = jnp.max(gate, axis=0, keepdims=True)
            first_idx = jnp.min(jnp.where(gate == mx, blk_row_f, float(MOBA_HEAD_DIM)),
                                axis=0, keepdims=True)
            hit = blk_row_f == first_idx
            sel = sel | (hit & (mx > NEG_INF))
            gate = jnp.where(hit, NEG_INF, gate)
        qat_ref[h * AUG_W:h * AUG_W + MOBA_HEAD_DIM, :] = (q_t * MOBA_EXP2_SCALE).astype(BF16)
        qat_ref[h * AUG_W + MOBA_HEAD_DIM:(h + 1) * AUG_W, :] = jnp.where(
            sel, 0.0, MASK_PENALTY).astype(BF16)
        ka_ref[:, h * AUG_W:h * AUG_W + MOBA_HEAD_DIM] = k.astype(BF16)
        ka_ref[:, h * AUG_W + MOBA_HEAD_DIM:(h + 1) * AUG_W] = onehot
        vt_ref[h, 0:MOBA_HEAD_DIM, :] = v_ref[:, cols].astype(F32).T.astype(BF16)
        vt_ref[h, MOBA_HEAD_DIM:VT_ROWS, :] = ones_rows

        k_mean_row = jnp.mean(k, axis=0, keepdims=True)
        kmean_scr[h] = jnp.where(mean_row == blk, k_mean_row, kmean)


def _moba_prep(main3d, cos_t, sin_t):
    b_sz, s_len, _ = main3d.shape
    n_blk = s_len // MOBA_BLOCK
    assert n_blk <= MOBA_HEAD_DIM

    def tile_spec(cb):
        return pl.BlockSpec((None, MOBA_BLOCK, MOBA_WIDTH), lambda b, i, cb=cb: (b, i, cb))

    tab_spec = pl.BlockSpec((MOBA_BLOCK, MOBA_HEAD_DIM), lambda b, i: (i, 0))
    return pl.pallas_call(
        _moba_prep_kernel,
        grid=(b_sz, n_blk),
        in_specs=[tile_spec(1), tile_spec(2), tile_spec(3), tab_spec, tab_spec],
        out_specs=[
            pl.BlockSpec((None, MOBA_HEADS * AUG_W, MOBA_BLOCK), lambda b, i: (b, 0, i)),
            pl.BlockSpec((None, MOBA_BLOCK, MOBA_HEADS * AUG_W), lambda b, i: (b, i, 0)),
            pl.BlockSpec((None, MOBA_HEADS, None, VT_ROWS, MOBA_BLOCK),
                         lambda b, i: (b, 0, i, 0, 0)),
        ],
        out_shape=[
            jax.ShapeDtypeStruct((b_sz, MOBA_HEADS * AUG_W, s_len), BF16),
            jax.ShapeDtypeStruct((b_sz, s_len, MOBA_HEADS * AUG_W), BF16),
            jax.ShapeDtypeStruct((b_sz, MOBA_HEADS, n_blk, VT_ROWS, MOBA_BLOCK), BF16),
        ],
        scratch_shapes=[pltpu.VMEM((MOBA_HEADS, MOBA_HEAD_DIM, MOBA_HEAD_DIM), F32)],
        compiler_params=pltpu.CompilerParams(
            dimension_semantics=("parallel", "arbitrary"), vmem_limit_bytes=VMEM_LIMIT),
        name="moba_prep",
    )(main3d, main3d, main3d, cos_t, sin_t)


def _moba_attn_kernel(qat_ref, ka_ref, vt_ref, out_ref, s_scr, p_scr, acc_scr, *, heads_per_step):
    n_q = qat_ref.shape[1]
    blocks_per_tile = n_q // MOBA_BLOCK
    first_blk = pl.program_id(2) * blocks_per_tile
    heads = range(heads_per_step)
    q_t = [qat_ref[h * AUG_W:(h + 1) * AUG_W, :] for h in heads]

    def scores(j):
        rows = pl.ds(pl.multiple_of(j * MOBA_BLOCK, MOBA_BLOCK), MOBA_BLOCK)
        return [jnp.dot(ka_ref[rows, h * AUG_W:(h + 1) * AUG_W], q_t[h],
                        preferred_element_type=F32) for h in heads]

    key = lax.broadcasted_iota(jnp.int32, (MOBA_BLOCK, n_q), 0)
    qry = lax.broadcasted_iota(jnp.int32, (MOBA_BLOCK, n_q), 1)
    m = None
    for r in range(blocks_per_tile):
        s = [jnp.where(key + r * MOBA_BLOCK <= qry, x, NEG_INF) for x in scores(first_blk + r)]
        if r == 0:
            m = [jnp.max(x, axis=0, keepdims=True) for x in s]
            for h in heads:
                acc_scr[h] = jnp.dot(vt_ref[h, first_blk], jnp.exp2(s[h] - m[h]).astype(BF16),
                                     preferred_element_type=F32)
        else:
            m_new = [jnp.maximum(m[h], jnp.max(s[h], axis=0, keepdims=True)) for h in heads]
            for h in heads:
                acc_scr[h] = (jnp.exp2(m[h] - m_new[h]) * acc_scr[h]
                              + jnp.dot(vt_ref[h, first_blk + r],
                                        jnp.exp2(s[h] - m_new[h]).astype(BF16),
                                        preferred_element_type=F32))
            m = m_new

    s_first = scores(0)
    for h in heads:
        s_scr[0, h] = s_first[h]
        p_scr[0, h] = jnp.zeros(p_scr.shape[2:], BF16)
    alpha = [jnp.ones_like(x) for x in m]

    def half_trip(j, cur, nxt, m, alpha):
        s_next = scores(jnp.minimum(j + 1, first_blk - 1))
        j_prev = jnp.maximum(j - 1, 0)
        for h in heads:
            acc_scr[h] = alpha[h] * acc_scr[h] + jnp.dot(vt_ref[h, j_prev], p_scr[cur, h],
                                                         preferred_element_type=F32)
        s_cur = [s_scr[cur, h] for h in heads]
        m_new = [jnp.maximum(m[h], jnp.max(s_cur[h], axis=0, keepdims=True)) for h in heads]
        for h in heads:
            p_scr[nxt, h] = jnp.exp2(s_cur[h] - m_new[h]).astype(BF16)
            s_scr[nxt, h] = s_next[h]
        alpha = [jnp.exp2(m[h] - m_new[h]) for h in heads]
        return m_new, alpha

    def body(jj, carry):
        m, alpha = carry
        m, alpha = half_trip(2 * jj, 0, 1, m, alpha)
        return half_trip(2 * jj + 1, 1, 0, m, alpha)

    assert blocks_per_tile % 2 == 0
    m, alpha = lax.fori_loop(0, first_blk // 2, body, (m, alpha))
    j_last = jnp.maximum(first_blk - 1, 0)
    for h in heads:
        a = alpha[h] * acc_scr[h] + jnp.dot(vt_ref[h, j_last], p_scr[0, h],
                                            preferred_element_type=F32)
        o_t = a[:MOBA_HEAD_DIM] / a[MOBA_HEAD_DIM:MOBA_HEAD_DIM + 1]
        out_ref[:, h * MOBA_HEAD_DIM:(h + 1) * MOBA_HEAD_DIM] = o_t.T.astype(out_ref.dtype)


def _moba_attn(qat, ka, vt, heads_per_step, q_tile):
    b_sz, s_len, _ = ka.shape
    n_blk = s_len // MOBA_BLOCK
    hb = heads_per_step
    resident = pl.Buffered(1)
    return pl.pallas_call(
        functools.partial(_moba_attn_kernel, heads_per_step=hb),
        grid=(b_sz, MOBA_HEADS // hb, s_len // q_tile),
        in_specs=[
            pl.BlockSpec((None, hb * AUG_W, q_tile), lambda b, g, i: (b, g, i)),
            pl.BlockSpec((None, s_len, hb * AUG_W), lambda b, g, i: (b, 0, g),
                         pipeline_mode=resident),
            pl.BlockSpec((None, hb, n_blk, VT_ROWS, MOBA_BLOCK),
                         lambda b, g, i: (b, g, 0, 0, 0), pipeline_mode=resident),
        ],
        out_specs=pl.BlockSpec((None, q_tile, hb * MOBA_HEAD_DIM), lambda b, g, i: (b, i, g)),
        out_shape=jax.ShapeDtypeStruct((b_sz, s_len, MOBA_WIDTH), BF16),
        scratch_shapes=[pltpu.VMEM((2, hb, MOBA_BLOCK, q_tile), F32),
                        pltpu.VMEM((2, hb, MOBA_BLOCK, q_tile), BF16),
                        pltpu.VMEM((hb, VT_ROWS, q_tile), F32)],
        compiler_params=pltpu.CompilerParams(
            dimension_semantics=("parallel", "parallel", "arbitrary"),
            vmem_limit_bytes=VMEM_LIMIT),
        name="moba_attn",
    )(qat, ka, vt)


def _out_proj_kernel(x_ref, odn_ref, omb_ref, gdn_ref, gmb_ref, wdn_ref, wmb_ref, wo_ref, h_ref):
    y_dn = jnp.dot(odn_ref[...], wdn_ref[...], preferred_element_type=F32)
    y_mb = jnp.dot(omb_ref[...], wmb_ref[...], preferred_element_type=F32)
    merged = (_sigmoid(gdn_ref[...].astype(F32)) * y_dn
              + _sigmoid(gmb_ref[...].astype(F32)) * y_mb)
    h_ref[...] = x_ref[...] + jnp.dot(merged.astype(BF16), wo_ref[...], preferred_element_type=F32)


def _out_proj(x2d, o_dn, o_mb, main2d, w_dn, w_mb, w_o, tm):
    t, d = x2d.shape
    row_spec = pl.BlockSpec((tm, d), lambda i: (i, 0))
    w_spec = pl.BlockSpec((d, d), lambda i: (0, 0))
    return pl.pallas_call(
        _out_proj_kernel,
        grid=(t // tm,),
        in_specs=[row_spec, row_spec, row_spec,
                  pl.BlockSpec((tm, d), lambda i: (i, 4)),
                  pl.BlockSpec((tm, d), lambda i: (i, 5)),
                  w_spec, w_spec, w_spec],
        out_specs=row_spec,
        out_shape=jax.ShapeDtypeStruct((t, d), F32),
        compiler_params=pltpu.CompilerParams(
            dimension_semantics=("parallel",), vmem_limit_bytes=VMEM_LIMIT),
        name="out_proj",
    )(x2d, o_dn, o_mb, main2d, main2d, w_dn, w_mb, w_o)


def _conv_ffn_kernel(h_ref, nw_ref, wup_ref, cw_ref, cb_ref, wdown_ref, fw_ref, out_ref,
                     u_scr, *, tiles_per_seq, n_split):
    tm = h_ref.shape[0]
    seq_start = (pl.program_id(0) % tiles_per_seq) == 0
    h = h_ref[...]
    var = jnp.mean(h * h, axis=-1, keepdims=True)
    n = ((h * lax.rsqrt(var + NORM_EPS)) * nw_ref[...]).astype(BF16)

    width = D_FF // n_split
    base = SUBLANES - (FFN_CONV - 1)
    acc = jnp.zeros((tm, D_MODEL), F32)
    for part in range(n_split):
        halves = []
        for half in range(2):
            c0 = half * D_FF + part * width
            slot = 2 * part + half
            @pl.when(seq_start)
            def _(slot=slot):
                u_scr[slot, 0:SUBLANES, :] = jnp.zeros((SUBLANES, width), F32)

            @pl.when(jnp.logical_not(seq_start))
            def _(slot=slot):
                u_scr[slot, 0:SUBLANES, :] = u_scr[slot, tm:tm + SUBLANES, :]

            u_scr[slot, SUBLANES:SUBLANES + tm, :] = jnp.dot(
                n, wup_ref[:, c0:c0 + width], preferred_element_type=F32)
            y = (u_scr[slot, SUBLANES:SUBLANES + tm, :]
                 * cw_ref[FFN_CONV - 1:FFN_CONV, c0:c0 + width])
            for j in range(FFN_CONV - 1):
                y = y + u_scr[slot, base + j:base + j + tm, :] * cw_ref[j:j + 1, c0:c0 + width]
            halves.append(y + cb_ref[:, c0:c0 + width])
        act = (_silu(halves[0]) * halves[1]).astype(BF16)
        acc = acc + jnp.dot(act, wdown_ref[part * width:(part + 1) * width, :],
                            preferred_element_type=F32)
    h2 = h + acc
    var2 = jnp.mean(h2 * h2, axis=-1, keepdims=True)
    out_ref[...] = (h2 * lax.rsqrt(var2 + NORM_EPS)) * fw_ref[...]


def _conv_ffn(h2d, norm_w, w_up, conv_w, conv_b, w_down, final_w, tm, s_len, n_split):
    t, d = h2d.shape
    width = D_FF // n_split
    kern = functools.partial(_conv_ffn_kernel, tiles_per_seq=s_len // tm, n_split=n_split)

    def const_spec(shape):
        return pl.BlockSpec(shape, lambda i: (0, 0), pipeline_mode=pl.Buffered(1))

    return pl.pallas_call(
        kern,
        grid=(t // tm,),
        in_specs=[
            pl.BlockSpec((tm, d), lambda i: (i, 0)),
            const_spec((1, d)),
            const_spec((d, 2 * D_FF)),
            const_spec((FFN_CONV, 2 * D_FF)),
            const_spec((1, 2 * D_FF)),
            const_spec((D_FF, d)),
            const_spec((1, d)),
        ],
        out_specs=pl.BlockSpec((tm, d), lambda i: (i, 0)),
        out_shape=jax.ShapeDtypeStruct((t, d), F32),
        scratch_shapes=[pltpu.VMEM((2 * n_split, tm + SUBLANES, width), F32)],
        compiler_params=pltpu.CompilerParams(
            dimension_semantics=("arbitrary",), vmem_limit_bytes=VMEM_LIMIT),
        name="conv_ffn",
    )(h2d, norm_w, w_up, conv_w, conv_b, w_down, final_w)


def _rope_tables(s_len):
    half = MOBA_HEAD_DIM // 2
    inv_freq = np.power(ROPE_THETA, -np.arange(half, dtype=np.float64) / half)
    ang = np.arange(s_len, dtype=np.float64)[:, None] * inv_freq[None, :]
    cos, sin = np.cos(ang), np.sin(ang)
    return (jnp.asarray(np.concatenate([cos, cos], axis=-1), F32),
            jnp.asarray(np.concatenate([-sin, sin], axis=-1), F32))


def _pick_tile(n, want):
    t = min(n, want)
    while n % t:
        t //= 2
    return t


def _layer(x, attn_norm_w, w_in, dn_conv_w, dn_a_log, dn_dt_bias, dn_norm_w, w_dn_out,
           w_moba_out, w_o, ffn_norm_w, w_up, ffn_conv_w, ffn_conv_b, w_down, final_norm_w):
    b_sz, s_len, d = x.shape
    t = b_sz * s_len
    x2d = x.reshape(t, d)

    small_lo = 4 * DN_WIDTH
    small_hi = small_lo + 2 * DN_HEADS
    w_lo = w_in[:, :small_lo].astype(BF16)
    w_hi = w_in[:, small_hi:].astype(BF16)
    w_small_cols = w_in[:, small_lo:small_hi]
    w_small = jnp.pad(w_small_cols, ((0, 0), (0, SMALL_W - 2 * DN_HEADS))).astype(BF16)
    w_small_t = w_small_cols.T.astype(BF16)

    q_tok, k_tok, v_tok, rest, small, small_t = _in_proj(
        x2d, attn_norm_w.reshape(1, d), w_lo, w_hi, w_small, w_small_t, _pick_tile(t, 1024))
    tok4d = (b_sz, s_len, DN_HEADS, DN_HEAD_DIM)
    main3d = rest.reshape(b_sz, s_len, rest.shape[1])
    small3d = small.reshape(b_sz, s_len, SMALL_W)

    pad_r = ((0, 0), (DN_HEADS, SMALL_W - 2 * DN_HEADS))
    alog_r = jnp.pad(dn_a_log.reshape(1, DN_HEADS), pad_r)
    dtb_r = jnp.pad(dn_dt_bias.reshape(1, DN_HEADS), pad_r)
    alog_c = jnp.pad(dn_a_log.reshape(DN_HEADS, 1), ((DN_HEADS, 0), (0, 0)))
    dtb_c = jnp.pad(dn_dt_bias.reshape(DN_HEADS, 1), ((DN_HEADS, 0), (0, 0)))
    o_dn = _deltanet(q_tok.reshape(tok4d), k_tok.reshape(tok4d), v_tok.reshape(tok4d), main3d,
                     small3d, small_t, dn_conv_w, alog_r, dtb_r, alog_c, dtb_c,
                     dn_norm_w.reshape(1, DN_HEAD_DIM), _pick_tile(s_len, 256))

    cos_t, sin_t = _rope_tables(s_len)
    qat, ka, vt = _moba_prep(main3d, cos_t, sin_t)
    o_mb = _moba_attn(qat, ka, vt, 4, _pick_tile(s_len, 2 * MOBA_BLOCK))

    h = _out_proj(x2d, o_dn.reshape(t, DN_WIDTH), o_mb.reshape(t, MOBA_WIDTH), rest,
                  w_dn_out.astype(BF16), w_moba_out.astype(BF16), w_o.astype(BF16),
                  _pick_tile(t, 1024))

    tm_ffn = _pick_tile(s_len, 512)
    out = _conv_ffn(h, ffn_norm_w.reshape(1, d), w_up.astype(BF16), ffn_conv_w,
                    ffn_conv_b.reshape(1, 2 * D_FF), w_down.astype(BF16),
                    final_norm_w.reshape(1, d), tm_ffn, s_len, 1)
    return out.reshape(b_sz, s_len, d)


def kernel(x, attn_norm_w, w_in, dn_conv_w, dn_A_log, dn_dt_bias, dn_norm_w, w_dn_out, w_moba_out,
           w_o, ffn_norm_w, w_up, ffn_conv_w, ffn_conv_b, w_down, final_norm_w):
    depth = w_in.shape[0]
    assert depth == 1, "the final RMSNorm is fused into the layer's conv_ffn call"
    return _layer(x, attn_norm_w[0], w_in[0], dn_conv_w[0], dn_A_log[0], dn_dt_bias[0],
                  dn_norm_w[0], w_dn_out[0], w_moba_out[0], w_o[0], ffn_norm_w[0], w_up[0],
                  ffn_conv_w[0], ffn_conv_b[0], w_down[0], final_norm_w)
```

```python
import functools

import jax
import jax.numpy as jnp
import numpy as np
from jax import lax
from jax.experimental import pallas as pl
from jax.experimental.pallas import tpu as pltpu

D_MODEL = 1024
DN_HEADS = 8
DN_HEAD_DIM = 128
DN_WIDTH = DN_HEADS * DN_HEAD_DIM
DN_CONV = 4
DN_CHUNK = 64
MOBA_HEADS = 8
MOBA_HEAD_DIM = 128
MOBA_WIDTH = MOBA_HEADS * MOBA_HEAD_DIM
MOBA_BLOCK = 256
MOBA_TOPK = 3
ROPE_THETA = 10000.0
D_FF = 2816
FFN_CONV = 3
NORM_EPS = 1e-6

SMALL_W = 128
DN_QKV_BLOCKS = 3
SUBLANES = 8
AUG_W = 2 * MOBA_HEAD_DIM
MASK_PENALTY = -(2.0 ** 100)
BF16_ROWS = 16
VT_ROWS = MOBA_HEAD_DIM + BF16_ROWS
MOBA_EXP2_SCALE = (MOBA_HEAD_DIM ** -0.5) * float(np.log2(np.e))
VMEM_LIMIT = 52 * 1024 * 1024

F32 = jnp.float32
BF16 = jnp.bfloat16
NEG_INF = float("-inf")


def _dot(a, b):
    return jnp.dot(a.astype(BF16), b.astype(BF16), preferred_element_type=F32)


def _dot_nt(a, b):
    return lax.dot_general(a.astype(BF16), b.astype(BF16), (((1,), (1,)), ((), ())),
                           preferred_element_type=F32)


def _dot_tn(a, b):
    return lax.dot_general(a.astype(BF16), b.astype(BF16), (((0,), (0,)), ((), ())),
                           preferred_element_type=F32)


def _dot_split(a, b):
    a_hi = a.astype(BF16)
    b_hi = b.astype(BF16)
    a_lo = (a - a_hi.astype(F32)).astype(BF16)
    b_lo = (b - b_hi.astype(F32)).astype(BF16)
    return (jnp.dot(a_hi, b_hi, preferred_element_type=F32)
            + (jnp.dot(a_hi, b_lo, preferred_element_type=F32)
               + jnp.dot(a_lo, b_hi, preferred_element_type=F32)))


def _sigmoid(x):
    return 1.0 / (1.0 + jnp.exp(-x))


def _silu(x):
    return x * _sigmoid(x)


def _softplus(x):
    return jnp.maximum(x, 0.0) + jnp.log1p(jnp.exp(-jnp.abs(x)))


def _in_proj_kernel_body(x_ref, nw_ref, wlo_ref, whi_ref, ws_ref, wst_ref, q_ref, k_ref, v_ref,
                         rest_ref, small_ref, smallt_ref, n_scr, *, n_lo):
    j = pl.program_id(1)
    tm = x_ref.shape[0]

    @pl.when(j == 0)
    def _():
        x = x_ref[...]
        var = jnp.mean(x * x, axis=-1, keepdims=True)
        n = ((x * lax.rsqrt(var + NORM_EPS)) * nw_ref[...]).astype(BF16)
        n_scr[...] = n
        small_ref[...] = jnp.dot(n, ws_ref[...], preferred_element_type=F32)
        smallt_ref[...] = lax.dot_general(wst_ref[...], n, (((1,), (1,)), ((), ())),
                                          preferred_element_type=F32)

    for blk, tok_ref in enumerate((q_ref, k_ref, v_ref)):
        @pl.when(j == blk)
        def _(tok_ref=tok_ref):
            res = jnp.dot(n_scr[...], wlo_ref[...], preferred_element_type=F32)
            for h in range(DN_HEADS):
                tok_ref[pl.ds(h, tm, stride=DN_HEADS), :] = res[:, h * DN_HEAD_DIM:(h + 1) * DN_HEAD_DIM]

    @pl.when(j == n_lo - 1)
    def _():
        rest_ref[...] = jnp.dot(n_scr[...], wlo_ref[...],
                                preferred_element_type=F32).astype(rest_ref.dtype)

    @pl.when(j >= n_lo)
    def _():
        rest_ref[...] = jnp.dot(n_scr[...], whi_ref[...],
                                preferred_element_type=F32).astype(rest_ref.dtype)


def _in_proj(x2d, norm_w, w_lo, w_hi, w_small, w_small_t, tm):
    t, d = x2d.shape
    tn = D_MODEL
    assert tn == DN_WIDTH
    n_lo = w_lo.shape[1] // tn
    assert n_lo == DN_QKV_BLOCKS + 1
    n_blocks = n_lo + w_hi.shape[1] // tn
    grid = (t // tm, n_blocks)
    tok_spec = pl.BlockSpec((tm * DN_HEADS, DN_HEAD_DIM), lambda i, j: (i, 0))
    tok_shape = jax.ShapeDtypeStruct((t * DN_HEADS, DN_HEAD_DIM), F32)
    return pl.pallas_call(
        functools.partial(_in_proj_kernel_body, n_lo=n_lo),
        grid=grid,
        in_specs=[
            pl.BlockSpec((tm, d), lambda i, j: (i, 0)),
            pl.BlockSpec((1, d), lambda i, j: (0, 0)),
            pl.BlockSpec((d, tn), lambda i, j: (0, jnp.minimum(j, n_lo - 1))),
            pl.BlockSpec((d, tn), lambda i, j: (0, jnp.maximum(j - n_lo, 0))),
            pl.BlockSpec((d, SMALL_W), lambda i, j: (0, 0)),
            pl.BlockSpec((2 * DN_HEADS, d), lambda i, j: (0, 0)),
        ],
        out_specs=[
            tok_spec, tok_spec, tok_spec,
            pl.BlockSpec((tm, tn), lambda i, j: (i, jnp.maximum(j - DN_QKV_BLOCKS, 0))),
            pl.BlockSpec((tm, SMALL_W), lambda i, j: (i, 0)),
            pl.BlockSpec((2 * DN_HEADS, tm), lambda i, j: (0, i)),
        ],
        out_shape=[
            tok_shape, tok_shape, tok_shape,
            jax.ShapeDtypeStruct((t, (n_blocks - DN_QKV_BLOCKS) * tn), BF16),
            jax.ShapeDtypeStruct((t, SMALL_W), F32),
            jax.ShapeDtypeStruct((2 * DN_HEADS, t), F32),
        ],
        scratch_shapes=[pltpu.VMEM((tm, d), BF16)],
        compiler_params=pltpu.CompilerParams(
            dimension_semantics=("parallel", "arbitrary"), vmem_limit_bytes=VMEM_LIMIT),
        name="in_proj",
    )(x2d, norm_w, w_lo, w_hi, w_small, w_small_t)


def _chunk_cumsum(x, axis):
    idx = lax.broadcasted_iota(jnp.int32, x.shape, axis) % DN_CHUNK
    shift = 1
    while shift < DN_CHUNK:
        x = x + jnp.where(idx >= shift, pltpu.roll(x, shift, axis), 0.0)
        shift *= 2
    return x


def _unit_lower_inverse(mats, row, col):
    c = mats[0].shape[0]
    eye = (row == col).astype(F32)
    blk = 8
    diag_blk = (row // blk) == (col // blk)
    n = [jnp.where(diag_blk, -a, 0.0) for a in mats]
    n2 = [_dot(x, x) for x in n]
    n4 = [_dot(x, x) for x in n2]
    t = [_dot(eye + x, eye + y) for x, y in zip(n, n2)]
    t = [_dot(x, eye + y) for x, y in zip(t, n4)]
    while blk < c:
        off = ((row // (2 * blk)) == (col // (2 * blk))) & ((row // blk) != (col // blk))
        a_t = [_dot(jnp.where(off, a, 0.0), x) for a, x in zip(mats, t)]
        t = [x - _dot(x, y) for x, y in zip(t, a_t)]
        blk *= 2
    return t


def _deltanet_kernel(q_ref, k_ref, v_ref, qh_ref, kh_ref, vh_ref, z_ref, small_ref, smallt_ref,
                     cwq_ref, cwk_ref, cwv_ref, alog_r_ref, dtb_r_ref, alog_c_ref, dtb_c_ref,
                     normw_ref, out_ref,
                     state_scr, qc_scr, kc_scr, vc_scr, u_scr, wq_scr, kd_scr,
                     qk_scr, gl_scr, bcol_scr, gcol_scr, grow_scr):
    tc = q_ref.shape[0]
    n_chunks = tc // DN_CHUNK
    first = pl.program_id(1) == 0
    heads = range(DN_HEADS)
    cols = [slice(h * DN_HEAD_DIM, (h + 1) * DN_HEAD_DIM) for h in heads]

    @pl.when(first)
    def _():
        state_scr[...] = jnp.zeros_like(state_scr)

    def conv_silu_norm(x_ref, halo_ref, cw_ref, dst, l2norm, scale):
        halo = jnp.where(first, 0.0, halo_ref[...])
        xp = jnp.concatenate([halo[SUBLANES - (DN_CONV - 1):], x_ref[...]], axis=0)
        y = xp[DN_CONV - 1:DN_CONV - 1 + tc] * cw_ref[DN_CONV - 1]
        for j in range(DN_CONV - 1):
            y = y + xp[j:j + tc] * cw_ref[j]
        y = _silu(y)
        if l2norm:
            y = y * lax.rsqrt(jnp.sum(y * y, axis=-1, keepdims=True) + NORM_EPS)
            if scale is not None:
                y = y * scale
        dst[...] = y.reshape(tc * DN_HEADS, DN_HEAD_DIM)

    conv_silu_norm(q_ref, qh_ref, cwq_ref, qc_scr, True, DN_HEAD_DIM ** -0.5)
    conv_silu_norm(k_ref, kh_ref, cwk_ref, kc_scr, True, None)
    conv_silu_norm(v_ref, vh_ref, cwv_ref, vc_scr, False, None)

    def head_rows(scr, c, h):
        return scr[pl.ds(c * DN_CHUNK * DN_HEADS + h, DN_CHUNK, stride=DN_HEADS), :]

    small = small_ref[...]
    bcol_scr[...] = _sigmoid(small)
    gcol_scr[...] = _chunk_cumsum(-jnp.exp(alog_r_ref[...]) * _softplus(small + dtb_r_ref[...]), 0)
    small_t = smallt_ref[...]
    g_t = _chunk_cumsum(-jnp.exp(alog_c_ref[...]) * _softplus(small_t + dtb_c_ref[...]), 1)
    for c in range(n_chunks):
        grow_scr[c] = g_t[:, c * DN_CHUNK:(c + 1) * DN_CHUNK]

    row = lax.broadcasted_iota(jnp.int32, (DN_CHUNK, DN_CHUNK), 0)
    col = lax.broadcasted_iota(jnp.int32, (DN_CHUNK, DN_CHUNK), 1)
    lower_incl = row >= col
    lower_strict = row > col

    def rows_at(start, size):
        if isinstance(start, int):
            return pl.ds(start, size)
        return pl.ds(pl.multiple_of(start, DN_CHUNK), size)

    def prep(c0, chunks):
        items = [(c0 + i, h) for i in range(chunks) for h in heads]
        rows = [rows_at(c * DN_CHUNK, DN_CHUNK) for c, _ in items]
        g_rows = [grow_scr[c0 + i] for i in range(chunks)]
        q = [head_rows(qc_scr, c, h) for c, h in items]
        k = [head_rows(kc_scr, c, h) for c, h in items]
        beta = [bcol_scr[r, h:h + 1] for r, (_, h) in zip(rows, items)]
        gc_b = [jnp.broadcast_to(gcol_scr[r, DN_HEADS + h:DN_HEADS + h + 1],
                                 (DN_CHUNK, DN_HEAD_DIM)) for r, (_, h) in zip(rows, items)]
        eg_b = [jnp.exp(x) for x in gc_b]
        decay = [jnp.exp(jnp.where(
            lower_incl,
            gc_b[n][:, :DN_CHUNK] - g_rows[n // DN_HEADS][DN_HEADS + h:DN_HEADS + h + 1, :],
            NEG_INF)) for n, (_, h) in enumerate(items)]
        kb = [x * y for x, y in zip(k, beta)]
        kq = [_dot_nt(jnp.concatenate([kb[n], q[n]], axis=0), k[n]) for n in range(len(items))]
        strict = [kq[n][:DN_CHUNK] * jnp.where(lower_strict, decay[n], 0.0)
                  for n in range(len(items))]
        for n, (c, h) in enumerate(items):
            qk_scr[h, rows[n], :] = (kq[n][DN_CHUNK:] * decay[n]).astype(BF16)
        t_mat = _unit_lower_inverse(strict, row, col)
        uw = [_dot(t_mat[n], jnp.concatenate([head_rows(vc_scr, c, h) * beta[n], kb[n] * eg_b[n]],
                                             axis=1)) for n, (c, h) in enumerate(items)]
        for n, (c, h) in enumerate(items):
            u_scr[rows[n], cols[h]] = uw[n][:, :DN_HEAD_DIM]
            wq_scr[rows_at(c * 2 * DN_CHUNK, DN_CHUNK), cols[h]] = uw[n][:, DN_HEAD_DIM:].astype(BF16)
            wq_scr[rows_at(c * 2 * DN_CHUNK + DN_CHUNK, DN_CHUNK), cols[h]] = (
                q[n] * eg_b[n]).astype(BF16)
            g_last_b = jnp.broadcast_to(gc_b[n][DN_CHUNK - 1:DN_CHUNK, :], (DN_CHUNK, DN_HEAD_DIM))
            kd_scr[rows[n], cols[h]] = (k[n] * jnp.exp(g_last_b - gc_b[n])).astype(BF16)
            gl_scr[c, h:h + 1, :] = eg_b[n][DN_CHUNK - 1:DN_CHUNK, :]

    prep(0, n_chunks)

    normw = normw_ref[...]

    def scan(c, carry):
        rows = pl.ds(pl.multiple_of(c * DN_CHUNK, DN_CHUNK), DN_CHUNK)
        wq_rows = pl.ds(pl.multiple_of(c * 2 * DN_CHUNK, 2 * DN_CHUNK), 2 * DN_CHUNK)
        gl_all = gl_scr[c]
        state = [state_scr[h] for h in heads]
        state_b = [x.astype(BF16) for x in state]
        ws_qs = [jnp.dot(wq_scr[wq_rows, cols[h]], state_b[h], preferred_element_type=F32)
                 for h in heads]
        v_new_b = [(u_scr[rows, cols[h]] - ws_qs[h][:DN_CHUNK]).astype(BF16) for h in heads]
        intra = [jnp.dot(qk_scr[h, rows, :], v_new_b[h], preferred_element_type=F32) for h in heads]
        d_state = [lax.dot_general(kd_scr[rows, cols[h]], v_new_b[h], (((0,), (0,)), ((), ())),
                                   preferred_element_type=F32) for h in heads]
        for h in heads:
            state_scr[h] = state[h] * gl_all[h:h + 1, :] + d_state[h]
            o = ws_qs[h][DN_CHUNK:] + intra[h]
            var = jnp.mean(o * o, axis=-1, keepdims=True)
            o = (o * lax.rsqrt(var + NORM_EPS)) * normw
            out_ref[rows, cols[h]] = (o * _silu(z_ref[rows, cols[h]].astype(F32))).astype(out_ref.dtype)
        return carry

    lax.fori_loop(0, n_chunks, scan, 0)


def _deltanet(q4d, k4d, v4d, rest3d, small3d, small_t3, conv_w, alog_r, dtb_r, alog_c, dtb_c, norm_w, tc):
    b_sz, s_len, _ = rest3d.shape
    n_t = s_len // tc
    n_chunks = tc // DN_CHUNK
    halo_blocks = tc // SUBLANES

    conv_w3 = conv_w.reshape(DN_CONV, conv_w.shape[1] // DN_HEAD_DIM, DN_HEAD_DIM)

    tile_spec = pl.BlockSpec((None, tc, DN_HEADS, DN_HEAD_DIM), lambda b, i: (b, i, 0, 0))
    halo_spec = pl.BlockSpec((None, SUBLANES, DN_HEADS, DN_HEAD_DIM),
                             lambda b, i: (b, jnp.maximum(i * halo_blocks - 1, 0), 0, 0))

    def const_spec(shape):
        return pl.BlockSpec(shape, lambda b, i: tuple(0 for _ in shape))

    in_specs = [
        tile_spec, tile_spec, tile_spec,
        halo_spec, halo_spec, halo_spec,
        pl.BlockSpec((None, tc, DN_WIDTH), lambda b, i: (b, i, 0)),
        pl.BlockSpec((None, tc, SMALL_W), lambda b, i: (b, i, 0)),
        pl.BlockSpec((2 * DN_HEADS, tc), lambda b, i: (0, b * n_t + i)),
        pl.BlockSpec((DN_CONV, DN_HEADS, DN_HEAD_DIM), lambda b, i: (0, 0, 0)),
        pl.BlockSpec((DN_CONV, DN_HEADS, DN_HEAD_DIM), lambda b, i: (0, 1, 0)),
        pl.BlockSpec((DN_CONV, DN_HEADS, DN_HEAD_DIM), lambda b, i: (0, 2, 0)),
        const_spec((1, SMALL_W)), const_spec((1, SMALL_W)),
        const_spec((2 * DN_HEADS, 1)), const_spec((2 * DN_HEADS, 1)),
        const_spec((1, DN_HEAD_DIM)),
    ]
    scratch = [
        pltpu.VMEM((DN_HEADS, DN_HEAD_DIM, DN_HEAD_DIM), F32),
        pltpu.VMEM((tc * DN_HEADS, DN_HEAD_DIM), F32),
        pltpu.VMEM((tc * DN_HEADS, DN_HEAD_DIM), F32),
        pltpu.VMEM((tc * DN_HEADS, DN_HEAD_DIM), F32),
        pltpu.VMEM((tc, DN_WIDTH), F32),
        pltpu.VMEM((2 * tc, DN_WIDTH), BF16),
        pltpu.VMEM((tc, DN_WIDTH), BF16),
        pltpu.VMEM((DN_HEADS, tc, DN_CHUNK), BF16),
        pltpu.VMEM((n_chunks, DN_HEADS, DN_HEAD_DIM), F32),
        pltpu.VMEM((tc, SMALL_W), F32),
        pltpu.VMEM((tc, SMALL_W), F32),
        pltpu.VMEM((n_chunks, 2 * DN_HEADS, DN_CHUNK), F32),
    ]
    return pl.pallas_call(
        _deltanet_kernel,
        grid=(b_sz, n_t),
        in_specs=in_specs,
        out_specs=pl.BlockSpec((None, tc, DN_WIDTH), lambda b, i: (b, i, 0)),
        out_shape=jax.ShapeDtypeStruct((b_sz, s_len, DN_WIDTH), BF16),
        scratch_shapes=scratch,
        compiler_params=pltpu.CompilerParams(
            dimension_semantics=("parallel", "arbitrary"), vmem_limit_bytes=VMEM_LIMIT),
        name="deltanet",
    )(q4d, k4d, v4d, q4d, k4d, v4d, rest3d, small3d, small_t3,
      conv_w3, conv_w3, conv_w3, alog_r, dtb_r, alog_c, dtb_c, norm_w)


def _moba_prep_kernel(q_ref, k_ref, v_ref, cos_ref, sin_ref, qat_ref, ka_ref, vt_ref, kmean_scr):
    blk = pl.program_id(1)
    n_rows = q_ref.shape[0]

    @pl.when(blk == 0)
    def _():
        kmean_scr[...] = jnp.zeros_like(kmean_scr)

    cos = cos_ref[...]
    sin = sin_ref[...]
    half = MOBA_HEAD_DIM // 2
    lane = lax.broadcasted_iota(jnp.int32, (n_rows, MOBA_HEAD_DIM), 1)
    onehot = (lane == blk).astype(BF16)
    km_rows = kmean_scr.shape[1]
    blk_row = lax.broadcasted_iota(jnp.int32, (km_rows, n_rows), 0)
    blk_row_f = blk_row.astype(F32)
    mean_row = lax.broadcasted_iota(jnp.int32, (km_rows, MOBA_HEAD_DIM), 0)
    pen_pad = jnp.zeros((MOBA_HEAD_DIM - km_rows, n_rows), BF16)
    ones_rows = (lax.broadcasted_iota(jnp.int32, (BF16_ROWS, n_rows), 0) == 0).astype(BF16)

    for h in range(MOBA_HEADS):
        cols = slice(h * MOBA_HEAD_DIM, (h + 1) * MOBA_HEAD_DIM)
        q = q_ref[:, cols].astype(F32)
        k = k_ref[:, cols].astype(F32)
        q = q * cos + pltpu.roll(q, half, 1) * sin
        k = k * cos + pltpu.roll(k, half, 1) * sin
        q_t = q.T

        kmean = kmean_scr[h]
        gate = _dot_split(kmean, q_t)
        gate = jnp.where(blk_row < blk, gate, NEG_INF)
        sel = blk_row == blk
        for _ in range(MOBA_TOPK):
            mx = jnp.max(gate, axis=0, keepdims=True)
            first_idx = jnp.min(jnp.where(gate == mx, blk_row_f, float(km_rows)),
                                axis=0, keepdims=True)
            hit = blk_row_f == first_idx
            sel = sel | (hit & (mx > NEG_INF))
            gate = jnp.where(hit, NEG_INF, gate)
        qat_ref[h * AUG_W:h * AUG_W + MOBA_HEAD_DIM, :] = (q_t * MOBA_EXP2_SCALE).astype(BF16)
        pen0 = h * AUG_W + MOBA_HEAD_DIM
        qat_ref[pen0:pen0 + km_rows, :] = jnp.where(sel, 0.0, MASK_PENALTY).astype(BF16)
        qat_ref[pen0 + km_rows:(h + 1) * AUG_W, :] = pen_pad
        ka_ref[:, h * AUG_W:h * AUG_W + MOBA_HEAD_DIM] = k.astype(BF16)
        ka_ref[:, h * AUG_W + MOBA_HEAD_DIM:(h + 1) * AUG_W] = onehot
        vt_ref[h, 0:MOBA_HEAD_DIM, :] = v_ref[:, cols].astype(F32).T.astype(BF16)
        vt_ref[h, MOBA_HEAD_DIM:VT_ROWS, :] = ones_rows

        k_mean_row = jnp.mean(k, axis=0, keepdims=True)
        kmean_scr[h] = jnp.where(mean_row == blk, k_mean_row, kmean)


def _moba_prep(main3d, cos_t, sin_t):
    b_sz, s_len, _ = main3d.shape
    n_blk = s_len // MOBA_BLOCK
    km_rows = -(-n_blk // BF16_ROWS) * BF16_ROWS
    assert km_rows <= MOBA_HEAD_DIM

    def tile_spec(cb):
        return pl.BlockSpec((None, MOBA_BLOCK, MOBA_WIDTH), lambda b, i, cb=cb: (b, i, cb))

    tab_spec = pl.BlockSpec((MOBA_BLOCK, MOBA_HEAD_DIM), lambda b, i: (i, 0))
    return pl.pallas_call(
        _moba_prep_kernel,
        grid=(b_sz, n_blk),
        in_specs=[tile_spec(1), tile_spec(2), tile_spec(3), tab_spec, tab_spec],
        out_specs=[
            pl.BlockSpec((None, MOBA_HEADS * AUG_W, MOBA_BLOCK), lambda b, i: (b, 0, i)),
            pl.BlockSpec((None, MOBA_BLOCK, MOBA_HEADS * AUG_W), lambda b, i: (b, i, 0)),
            pl.BlockSpec((None, MOBA_HEADS, None, VT_ROWS, MOBA_BLOCK),
                         lambda b, i: (b, 0, i, 0, 0)),
        ],
        out_shape=[
            jax.ShapeDtypeStruct((b_sz, MOBA_HEADS * AUG_W, s_len), BF16),
            jax.ShapeDtypeStruct((b_sz, s_len, MOBA_HEADS * AUG_W), BF16),
            jax.ShapeDtypeStruct((b_sz, MOBA_HEADS, n_blk, VT_ROWS, MOBA_BLOCK), BF16),
        ],
        scratch_shapes=[pltpu.VMEM((MOBA_HEADS, km_rows, MOBA_HEAD_DIM), F32)],
        compiler_params=pltpu.CompilerParams(
            dimension_semantics=("parallel", "arbitrary"), vmem_limit_bytes=VMEM_LIMIT),
        name="moba_prep",
    )(main3d, main3d, main3d, cos_t, sin_t)


def _moba_attn_kernel(qat_ref, ka_ref, vt_ref, out_ref, s_scr, p_scr, acc_scr, *, heads_per_step):
    n_q = qat_ref.shape[1]
    blocks_per_tile = n_q // MOBA_BLOCK
    first_blk = pl.program_id(2) * blocks_per_tile
    heads = range(heads_per_step)
    q_t = [qat_ref[h * AUG_W:(h + 1) * AUG_W, :] for h in heads]

    def scores(j):
        rows = pl.ds(pl.multiple_of(j * MOBA_BLOCK, MOBA_BLOCK), MOBA_BLOCK)
        return [jnp.dot(ka_ref[rows, h * AUG_W:(h + 1) * AUG_W], q_t[h],
                        preferred_element_type=F32) for h in heads]

    key = lax.broadcasted_iota(jnp.int32, (MOBA_BLOCK, n_q), 0)
    qry = lax.broadcasted_iota(jnp.int32, (MOBA_BLOCK, n_q), 1)
    m = None
    for r in range(blocks_per_tile):
        s = [jnp.where(key + r * MOBA_BLOCK <= qry, x, NEG_INF) for x in scores(first_blk + r)]
        if r == 0:
            m = [jnp.max(x, axis=0, keepdims=True) for x in s]
            for h in heads:
                acc_scr[h] = jnp.dot(vt_ref[h, first_blk], jnp.exp2(s[h] - m[h]).astype(BF16),
                                     preferred_element_type=F32)
        else:
            m_new = [jnp.maximum(m[h], jnp.max(s[h], axis=0, keepdims=True)) for h in heads]
            for h in heads:
                acc_scr[h] = (jnp.exp2(m[h] - m_new[h]) * acc_scr[h]
                              + jnp.dot(vt_ref[h, first_blk + r],
                                        jnp.exp2(s[h] - m_new[h]).astype(BF16),
                                        preferred_element_type=F32))
            m = m_new

    s_first = scores(0)
    for h in heads:
        s_scr[0, h] = s_first[h]
        p_scr[0, h] = jnp.zeros(p_scr.shape[2:], BF16)
    alpha = [jnp.ones_like(x) for x in m]

    def half_trip(j, cur, nxt, m, alpha):
        s_cur = [s_scr[cur, h] for h in heads]
        m_new = [jnp.maximum(m[h], jnp.max(s_cur[h], axis=0, keepdims=True)) for h in heads]
        p_new = [jnp.exp2(s_cur[h] - m_new[h]).astype(BF16) for h in heads]
        s_next = scores(jnp.minimum(j + 1, first_blk - 1))
        j_prev = jnp.maximum(j - 1, 0)
        for h in heads:
            acc_scr[h] = alpha[h] * acc_scr[h] + jnp.dot(vt_ref[h, j_prev], p_scr[cur, h],
                                                         preferred_element_type=F32)
        for h in heads:
            p_scr[nxt, h] = p_new[h]
            s_scr[nxt, h] = s_next[h]
        alpha = [jnp.exp2(m[h] - m_new[h]) for h in heads]
        return m_new, alpha

    def body(jj, carry):
        m, alpha = carry
        m, alpha = half_trip(2 * jj, 0, 1, m, alpha)
        return half_trip(2 * jj + 1, 1, 0, m, alpha)

    assert blocks_per_tile % 2 == 0
    m, alpha = lax.fori_loop(0, first_blk // 2, body, (m, alpha))
    j_last = jnp.maximum(first_blk - 1, 0)
    for h in heads:
        a = alpha[h] * acc_scr[h] + jnp.dot(vt_ref[h, j_last], p_scr[0, h],
                                            preferred_element_type=F32)
        o_t = a[:MOBA_HEAD_DIM] / a[MOBA_HEAD_DIM:MOBA_HEAD_DIM + 1]
        out_ref[:, h * MOBA_HEAD_DIM:(h + 1) * MOBA_HEAD_DIM] = o_t.T.astype(out_ref.dtype)


def _moba_attn(qat, ka, vt, heads_per_step, q_tile):
    b_sz, s_len, _ = ka.shape
    n_blk = s_len // MOBA_BLOCK
    hb = heads_per_step
    resident = pl.Buffered(1)
    return pl.pallas_call(
        functools.partial(_moba_attn_kernel, heads_per_step=hb),
        grid=(b_sz, MOBA_HEADS // hb, s_len // q_tile),
        in_specs=[
            pl.BlockSpec((None, hb * AUG_W, q_tile), lambda b, g, i: (b, g, i)),
            pl.BlockSpec((None, s_len, hb * AUG_W), lambda b, g, i: (b, 0, g),
                         pipeline_mode=resident),
            pl.BlockSpec((None, hb, n_blk, VT_ROWS, MOBA_BLOCK),
                         lambda b, g, i: (b, g, 0, 0, 0), pipeline_mode=resident),
        ],
        out_specs=pl.BlockSpec((None, q_tile, hb * MOBA_HEAD_DIM), lambda b, g, i: (b, i, g)),
        out_shape=jax.ShapeDtypeStruct((b_sz, s_len, MOBA_WIDTH), BF16),
        scratch_shapes=[pltpu.VMEM((2, hb, MOBA_BLOCK, q_tile), F32),
                        pltpu.VMEM((2, hb, MOBA_BLOCK, q_tile), BF16),
                        pltpu.VMEM((hb, VT_ROWS, q_tile), F32)],
        compiler_params=pltpu.CompilerParams(
            dimension_semantics=("parallel", "parallel", "arbitrary"),
            vmem_limit_bytes=VMEM_LIMIT),
        name="moba_attn",
    )(qat, ka, vt)


def _out_proj_kernel(x_ref, odn_ref, omb_ref, gdn_ref, gmb_ref, wdn_ref, wmb_ref, wo_ref, h_ref):
    y_dn = jnp.dot(odn_ref[...], wdn_ref[...], preferred_element_type=F32)
    y_mb = jnp.dot(omb_ref[...], wmb_ref[...], preferred_element_type=F32)
    merged = (_sigmoid(gdn_ref[...].astype(F32)) * y_dn
              + _sigmoid(gmb_ref[...].astype(F32)) * y_mb)
    h_ref[...] = x_ref[...] + jnp.dot(merged.astype(BF16), wo_ref[...], preferred_element_type=F32)


def _out_proj(x2d, o_dn, o_mb, main2d, w_dn, w_mb, w_o, tm):
    t, d = x2d.shape
    row_spec = pl.BlockSpec((tm, d), lambda i: (i, 0))
    w_spec = pl.BlockSpec((d, d), lambda i: (0, 0))
    return pl.pallas_call(
        _out_proj_kernel,
        grid=(t // tm,),
        in_specs=[row_spec, row_spec, row_spec,
                  pl.BlockSpec((tm, d), lambda i: (i, 4)),
                  pl.BlockSpec((tm, d), lambda i: (i, 5)),
                  w_spec, w_spec, w_spec],
        out_specs=row_spec,
        out_shape=jax.ShapeDtypeStruct((t, d), F32),
        compiler_params=pltpu.CompilerParams(
            dimension_semantics=("parallel",), vmem_limit_bytes=VMEM_LIMIT),
        name="out_proj",
    )(x2d, o_dn, o_mb, main2d, main2d, w_dn, w_mb, w_o)


def _conv_ffn_kernel(h_ref, nw_ref, wup_ref, cw_ref, cb_ref, wdown_ref, fw_ref, out_ref,
                     u_scr, *, tiles_per_seq, n_split):
    tm = h_ref.shape[0]
    seq_start = (pl.program_id(0) % tiles_per_seq) == 0
    h = h_ref[...]
    var = jnp.mean(h * h, axis=-1, keepdims=True)
    n = ((h * lax.rsqrt(var + NORM_EPS)) * nw_ref[...]).astype(BF16)

    width = D_FF // n_split
    base = SUBLANES - (FFN_CONV - 1)
    acc = jnp.zeros((tm, D_MODEL), F32)
    for part in range(n_split):
        halves = []
        for half in range(2):
            c0 = half * D_FF + part * width
            slot = 2 * part + half
            @pl.when(seq_start)
            def _(slot=slot):
                u_scr[slot, 0:SUBLANES, :] = jnp.zeros((SUBLANES, width), F32)

            @pl.when(jnp.logical_not(seq_start))
            def _(slot=slot):
                u_scr[slot, 0:SUBLANES, :] = u_scr[slot, tm:tm + SUBLANES, :]

            u_scr[slot, SUBLANES:SUBLANES + tm, :] = jnp.dot(
                n, wup_ref[:, c0:c0 + width], preferred_element_type=F32)
            y = (u_scr[slot, SUBLANES:SUBLANES + tm, :]
                 * cw_ref[FFN_CONV - 1:FFN_CONV, c0:c0 + width])
            for j in range(FFN_CONV - 1):
                y = y + u_scr[slot, base + j:base + j + tm, :] * cw_ref[j:j + 1, c0:c0 + width]
            halves.append(y + cb_ref[:, c0:c0 + width])
        act = (_silu(halves[0]) * halves[1]).astype(BF16)
        acc = acc + jnp.dot(act, wdown_ref[part * width:(part + 1) * width, :],
                            preferred_element_type=F32)
    h2 = h + acc
    var2 = jnp.mean(h2 * h2, axis=-1, keepdims=True)
    out_ref[...] = (h2 * lax.rsqrt(var2 + NORM_EPS)) * fw_ref[...]


def _conv_ffn(h2d, norm_w, w_up, conv_w, conv_b, w_down, final_w, tm, s_len, n_split):
    t, d = h2d.shape
    width = D_FF // n_split
    kern = functools.partial(_conv_ffn_kernel, tiles_per_seq=s_len // tm, n_split=n_split)

    def const_spec(shape):
        return pl.BlockSpec(shape, lambda i: (0, 0), pipeline_mode=pl.Buffered(1))

    return pl.pallas_call(
        kern,
        grid=(t // tm,),
        in_specs=[
            pl.BlockSpec((tm, d), lambda i: (i, 0)),
            const_spec((1, d)),
            const_spec((d, 2 * D_FF)),
            const_spec((FFN_CONV, 2 * D_FF)),
            const_spec((1, 2 * D_FF)),
            const_spec((D_FF, d)),
            const_spec((1, d)),
        ],
        out_specs=pl.BlockSpec((tm, d), lambda i: (i, 0)),
        out_shape=jax.ShapeDtypeStruct((t, d), F32),
        scratch_shapes=[pltpu.VMEM((2 * n_split, tm + SUBLANES, width), F32)],
        compiler_params=pltpu.CompilerParams(
            dimension_semantics=("arbitrary",), vmem_limit_bytes=VMEM_LIMIT),
        name="conv_ffn",
    )(h2d, norm_w, w_up, conv_w, conv_b, w_down, final_w)


def _rope_tables(s_len):
    half = MOBA_HEAD_DIM // 2
    inv_freq = np.power(ROPE_THETA, -np.arange(half, dtype=np.float64) / half)
    ang = np.arange(s_len, dtype=np.float64)[:, None] * inv_freq[None, :]
    cos, sin = np.cos(ang), np.sin(ang)
    return (jnp.asarray(np.concatenate([cos, cos], axis=-1), F32),
            jnp.asarray(np.concatenate([-sin, sin], axis=-1), F32))


def _pick_tile(n, want):
    t = min(n, want)
    while n % t:
        t //= 2
    return t


def _layer(x, attn_norm_w, w_in, dn_conv_w, dn_a_log, dn_dt_bias, dn_norm_w, w_dn_out,
           w_moba_out, w_o, ffn_norm_w, w_up, ffn_conv_w, ffn_conv_b, w_down, final_norm_w):
    b_sz, s_len, d = x.shape
    t = b_sz * s_len
    x2d = x.reshape(t, d)

    small_lo = 4 * DN_WIDTH
    small_hi = small_lo + 2 * DN_HEADS
    w_lo = w_in[:, :small_lo].astype(BF16)
    w_hi = w_in[:, small_hi:].astype(BF16)
    w_small_cols = w_in[:, small_lo:small_hi]
    w_small = jnp.pad(w_small_cols, ((0, 0), (0, SMALL_W - 2 * DN_HEADS))).astype(BF16)
    w_small_t = w_small_cols.T.astype(BF16)

    q_tok, k_tok, v_tok, rest, small, small_t = _in_proj(
        x2d, attn_norm_w.reshape(1, d), w_lo, w_hi, w_small, w_small_t, _pick_tile(t, 1024))
    tok4d = (b_sz, s_len, DN_HEADS, DN_HEAD_DIM)
    main3d = rest.reshape(b_sz, s_len, rest.shape[1])
    small3d = small.reshape(b_sz, s_len, SMALL_W)

    pad_r = ((0, 0), (DN_HEADS, SMALL_W - 2 * DN_HEADS))
    alog_r = jnp.pad(dn_a_log.reshape(1, DN_HEADS), pad_r)
    dtb_r = jnp.pad(dn_dt_bias.reshape(1, DN_HEADS), pad_r)
    alog_c = jnp.pad(dn_a_log.reshape(DN_HEADS, 1), ((DN_HEADS, 0), (0, 0)))
    dtb_c = jnp.pad(dn_dt_bias.reshape(DN_HEADS, 1), ((DN_HEADS, 0), (0, 0)))
    o_dn = _deltanet(q_tok.reshape(tok4d), k_tok.reshape(tok4d), v_tok.reshape(tok4d), main3d,
                     small3d, small_t, dn_conv_w, alog_r, dtb_r, alog_c, dtb_c,
                     dn_norm_w.reshape(1, DN_HEAD_DIM), _pick_tile(s_len, 256))

    cos_t, sin_t = _rope_tables(s_len)
    qat, ka, vt = _moba_prep(main3d, cos_t, sin_t)
    o_mb = _moba_attn(qat, ka, vt, 4, _pick_tile(s_len, 2 * MOBA_BLOCK))

    h = _out_proj(x2d, o_dn.reshape(t, DN_WIDTH), o_mb.reshape(t, MOBA_WIDTH), rest,
                  w_dn_out.astype(BF16), w_moba_out.astype(BF16), w_o.astype(BF16),
                  _pick_tile(t, 1024))

    tm_ffn = _pick_tile(s_len, 512)
    out = _conv_ffn(h, ffn_norm_w.reshape(1, d), w_up.astype(BF16), ffn_conv_w,
                    ffn_conv_b.reshape(1, 2 * D_FF), w_down.astype(BF16),
                    final_norm_w.reshape(1, d), tm_ffn, s_len, 1)
    return out.reshape(b_sz, s_len, d)


def kernel(x, attn_norm_w, w_in, dn_conv_w, dn_A_log, dn_dt_bias, dn_norm_w, w_dn_out, w_moba_out,
           w_o, ffn_norm_w, w_up, ffn_conv_w, ffn_conv_b, w_down, final_norm_w):
    depth = w_in.shape[0]
    assert depth == 1, "the final RMSNorm is fused into the layer's conv_ffn call"
    return _layer(x, attn_norm_w[0], w_in[0], dn_conv_w[0], dn_A_log[0], dn_dt_bias[0],
                  dn_norm_w[0], w_dn_out[0], w_moba_out[0], w_o[0], ffn_norm_w[0], w_up[0],
                  ffn_conv_w[0], ffn_conv_b[0], w_down[0], final_norm_w)
```

```python
import functools

import jax
import jax.numpy as jnp
import numpy as np
from jax import lax
from jax.experimental import pallas as pl
from jax.experimental.pallas import tpu as pltpu

D_MODEL = 1024
DN_HEADS = 8
DN_HEAD_DIM = 128
DN_WIDTH = DN_HEADS * DN_HEAD_DIM
DN_CONV = 4
DN_CHUNK = 64
MOBA_HEADS = 8
MOBA_HEAD_DIM = 128
MOBA_WIDTH = MOBA_HEADS * MOBA_HEAD_DIM
MOBA_BLOCK = 256
MOBA_TOPK = 3
ROPE_THETA = 10000.0
D_FF = 2816
FFN_CONV = 3
NORM_EPS = 1e-6

SMALL_W = 128
DN_QKV_BLOCKS = 3
DN_CHUNK_GROUPS = 2
SUBLANES = 8
AUG_W = 2 * MOBA_HEAD_DIM
MASK_PENALTY = -(2.0 ** 100)
BF16_ROWS = 16
VT_ROWS = MOBA_HEAD_DIM + BF16_ROWS
MOBA_EXP2_SCALE = (MOBA_HEAD_DIM ** -0.5) * float(np.log2(np.e))
VMEM_LIMIT = 52 * 1024 * 1024

F32 = jnp.float32
BF16 = jnp.bfloat16
NEG_INF = float("-inf")


def _dot(a, b):
    return jnp.dot(a.astype(BF16), b.astype(BF16), preferred_element_type=F32)


def _dot_nt(a, b):
    return lax.dot_general(a.astype(BF16), b.astype(BF16), (((1,), (1,)), ((), ())),
                           preferred_element_type=F32)


def _dot_tn(a, b):
    return lax.dot_general(a.astype(BF16), b.astype(BF16), (((0,), (0,)), ((), ())),
                           preferred_element_type=F32)


def _dot_split(a, b):
    a_hi = a.astype(BF16)
    b_hi = b.astype(BF16)
    a_lo = (a - a_hi.astype(F32)).astype(BF16)
    b_lo = (b - b_hi.astype(F32)).astype(BF16)
    return (jnp.dot(a_hi, b_hi, preferred_element_type=F32)
            + (jnp.dot(a_hi, b_lo, preferred_element_type=F32)
               + jnp.dot(a_lo, b_hi, preferred_element_type=F32)))


def _sigmoid(x):
    return 1.0 / (1.0 + jnp.exp(-x))


def _silu(x):
    return x * _sigmoid(x)


def _softplus(x):
    return jnp.maximum(x, 0.0) + jnp.log1p(jnp.exp(-jnp.abs(x)))


def _in_proj_kernel_body(x_ref, nw_ref, wlo_ref, whi_ref, ws_ref, wst_ref, q_ref, k_ref, v_ref,
                         rest_ref, small_ref, smallt_ref, n_scr, *, n_lo):
    j = pl.program_id(1)
    tm = x_ref.shape[0]

    @pl.when(j == 0)
    def _():
        x = x_ref[...]
        var = jnp.mean(x * x, axis=-1, keepdims=True)
        n = ((x * lax.rsqrt(var + NORM_EPS)) * nw_ref[...]).astype(BF16)
        n_scr[...] = n
        small_ref[...] = jnp.dot(n, ws_ref[...], preferred_element_type=F32)
        smallt_ref[...] = lax.dot_general(wst_ref[...], n, (((1,), (1,)), ((), ())),
                                          preferred_element_type=F32)

    for blk, tok_ref in enumerate((q_ref, k_ref, v_ref)):
        @pl.when(j == blk)
        def _(tok_ref=tok_ref):
            res = jnp.dot(n_scr[...], wlo_ref[...], preferred_element_type=F32)
            for h in range(DN_HEADS):
                tok_ref[pl.ds(h, tm, stride=DN_HEADS), :] = res[:, h * DN_HEAD_DIM:(h + 1) * DN_HEAD_DIM]

    @pl.when(j == n_lo - 1)
    def _():
        rest_ref[...] = jnp.dot(n_scr[...], wlo_ref[...],
                                preferred_element_type=F32).astype(rest_ref.dtype)

    @pl.when(j >= n_lo)
    def _():
        rest_ref[...] = jnp.dot(n_scr[...], whi_ref[...],
                                preferred_element_type=F32).astype(rest_ref.dtype)


def _in_proj(x2d, norm_w, w_lo, w_hi, w_small, w_small_t, tm):
    t, d = x2d.shape
    tn = D_MODEL
    assert tn == DN_WIDTH
    n_lo = w_lo.shape[1] // tn
    assert n_lo == DN_QKV_BLOCKS + 1
    n_blocks = n_lo + w_hi.shape[1] // tn
    grid = (t // tm, n_blocks)
    tok_spec = pl.BlockSpec((tm * DN_HEADS, DN_HEAD_DIM), lambda i, j: (i, 0))
    tok_shape = jax.ShapeDtypeStruct((t * DN_HEADS, DN_HEAD_DIM), F32)
    return pl.pallas_call(
        functools.partial(_in_proj_kernel_body, n_lo=n_lo),
        grid=grid,
        in_specs=[
            pl.BlockSpec((tm, d), lambda i, j: (i, 0)),
            pl.BlockSpec((1, d), lambda i, j: (0, 0)),
            pl.BlockSpec((d, tn), lambda i, j: (0, jnp.minimum(j, n_lo - 1))),
            pl.BlockSpec((d, tn), lambda i, j: (0, jnp.maximum(j - n_lo, 0))),
            pl.BlockSpec((d, SMALL_W), lambda i, j: (0, 0)),
            pl.BlockSpec((2 * DN_HEADS, d), lambda i, j: (0, 0)),
        ],
        out_specs=[
            tok_spec, tok_spec, tok_spec,
            pl.BlockSpec((tm, tn), lambda i, j: (i, jnp.maximum(j - DN_QKV_BLOCKS, 0))),
            pl.BlockSpec((tm, SMALL_W), lambda i, j: (i, 0)),
            pl.BlockSpec((2 * DN_HEADS, tm), lambda i, j: (0, i)),
        ],
        out_shape=[
            tok_shape, tok_shape, tok_shape,
            jax.ShapeDtypeStruct((t, (n_blocks - DN_QKV_BLOCKS) * tn), BF16),
            jax.ShapeDtypeStruct((t, SMALL_W), F32),
            jax.ShapeDtypeStruct((2 * DN_HEADS, t), F32),
        ],
        scratch_shapes=[pltpu.VMEM((tm, d), BF16)],
        compiler_params=pltpu.CompilerParams(
            dimension_semantics=("parallel", "arbitrary"), vmem_limit_bytes=VMEM_LIMIT),
        name="in_proj",
    )(x2d, norm_w, w_lo, w_hi, w_small, w_small_t)


def _chunk_cumsum(x, axis):
    idx = lax.broadcasted_iota(jnp.int32, x.shape, axis) % DN_CHUNK
    shift = 1
    while shift < DN_CHUNK:
        x = x + jnp.where(idx >= shift, pltpu.roll(x, shift, axis), 0.0)
        shift *= 2
    return x


def _unit_lower_inverse(mats, row, col):
    c = mats[0].shape[0]
    eye = (row == col).astype(F32)
    blk = 8
    diag_blk = (row // blk) == (col // blk)
    n = [jnp.where(diag_blk, -a, 0.0) for a in mats]
    n2 = [_dot(x, x) for x in n]
    yield
    n4 = [_dot(x, x) for x in n2]
    t = [_dot(eye + x, eye + y) for x, y in zip(n, n2)]
    yield
    t = [_dot(x, eye + y) for x, y in zip(t, n4)]
    yield
    while blk < c:
        off = ((row // (2 * blk)) == (col // (2 * blk))) & ((row // blk) != (col // blk))
        a_t = [_dot(jnp.where(off, a, 0.0), x) for a, x in zip(mats, t)]
        yield
        t = [x - _dot(x, y) for x, y in zip(t, a_t)]
        yield
        blk *= 2
    return t


def _interleave(*stage_generators):
    live = list(stage_generators)
    while live:
        for g in list(live):
            try:
                next(g)
            except StopIteration:
                live.remove(g)


def _deltanet_kernel(q_ref, k_ref, v_ref, qh_ref, kh_ref, vh_ref, z_ref, small_ref, smallt_ref,
                     cwq_ref, cwk_ref, cwv_ref, alog_r_ref, dtb_r_ref, alog_c_ref, dtb_c_ref,
                     normw_ref, out_ref,
                     state_scr, qc_scr, kc_scr, vc_scr, u_scr, wq_scr, kd_scr,
                     qk_scr, gl_scr, bcol_scr, gcol_scr, grow_scr):
    tc = q_ref.shape[0]
    n_chunks = tc // DN_CHUNK
    first = pl.program_id(1) == 0
    heads = range(DN_HEADS)
    cols = [slice(h * DN_HEAD_DIM, (h + 1) * DN_HEAD_DIM) for h in heads]

    @pl.when(first)
    def _():
        state_scr[...] = jnp.zeros_like(state_scr)

    def conv_silu_norm(x_ref, halo_ref, cw_ref, dst, l2norm, scale):
        halo = jnp.where(first, 0.0, halo_ref[...])
        xp = jnp.concatenate([halo[SUBLANES - (DN_CONV - 1):], x_ref[...]], axis=0)
        y = xp[DN_CONV - 1:DN_CONV - 1 + tc] * cw_ref[DN_CONV - 1]
        for j in range(DN_CONV - 1):
            y = y + xp[j:j + tc] * cw_ref[j]
        y = _silu(y)
        if l2norm:
            y = y * lax.rsqrt(jnp.sum(y * y, axis=-1, keepdims=True) + NORM_EPS)
            if scale is not None:
                y = y * scale
        dst[...] = y.reshape(tc * DN_HEADS, DN_HEAD_DIM)

    conv_silu_norm(q_ref, qh_ref, cwq_ref, qc_scr, True, DN_HEAD_DIM ** -0.5)
    conv_silu_norm(k_ref, kh_ref, cwk_ref, kc_scr, True, None)
    conv_silu_norm(v_ref, vh_ref, cwv_ref, vc_scr, False, None)

    def head_rows(scr, c, h):
        return scr[pl.ds(c * DN_CHUNK * DN_HEADS + h, DN_CHUNK, stride=DN_HEADS), :]

    small = small_ref[...]
    bcol_scr[...] = _sigmoid(small)
    gcol_scr[...] = _chunk_cumsum(-jnp.exp(alog_r_ref[...]) * _softplus(small + dtb_r_ref[...]), 0)
    small_t = smallt_ref[...]
    g_t = _chunk_cumsum(-jnp.exp(alog_c_ref[...]) * _softplus(small_t + dtb_c_ref[...]), 1)
    for c in range(n_chunks):
        grow_scr[c] = g_t[:, c * DN_CHUNK:(c + 1) * DN_CHUNK]

    row = lax.broadcasted_iota(jnp.int32, (DN_CHUNK, DN_CHUNK), 0)
    col = lax.broadcasted_iota(jnp.int32, (DN_CHUNK, DN_CHUNK), 1)
    lower_incl = row >= col
    lower_strict = row > col

    def rows_at(start, size):
        if isinstance(start, int):
            return pl.ds(start, size)
        return pl.ds(pl.multiple_of(start, DN_CHUNK), size)

    def prep(c0, chunks):
        items = [(c0 + i, h) for i in range(chunks) for h in heads]
        rows = [rows_at(c * DN_CHUNK, DN_CHUNK) for c, _ in items]
        g_rows = [grow_scr[c0 + i] for i in range(chunks)]
        q = [head_rows(qc_scr, c, h) for c, h in items]
        k = [head_rows(kc_scr, c, h) for c, h in items]
        beta = [bcol_scr[r, h:h + 1] for r, (_, h) in zip(rows, items)]
        gc_b = [jnp.broadcast_to(gcol_scr[r, DN_HEADS + h:DN_HEADS + h + 1],
                                 (DN_CHUNK, DN_HEAD_DIM)) for r, (_, h) in zip(rows, items)]
        eg_b = [jnp.exp(x) for x in gc_b]
        decay = [jnp.exp(jnp.where(
            lower_incl,
            gc_b[n][:, :DN_CHUNK] - g_rows[n // DN_HEADS][DN_HEADS + h:DN_HEADS + h + 1, :],
            NEG_INF)) for n, (_, h) in enumerate(items)]
        kb = [x * y for x, y in zip(k, beta)]
        kq = [_dot_nt(jnp.concatenate([kb[n], q[n]], axis=0), k[n]) for n in range(len(items))]
        yield
        strict = [kq[n][:DN_CHUNK] * jnp.where(lower_strict, decay[n], 0.0)
                  for n in range(len(items))]
        for n, (c, h) in enumerate(items):
            qk_scr[h, rows[n], :] = (kq[n][DN_CHUNK:] * decay[n]).astype(BF16)
        t_mat = yield from _unit_lower_inverse(strict, row, col)
        uw = [_dot(t_mat[n], jnp.concatenate([head_rows(vc_scr, c, h) * beta[n], kb[n] * eg_b[n]],
                                             axis=1)) for n, (c, h) in enumerate(items)]
        yield
        for n, (c, h) in enumerate(items):
            u_scr[rows[n], cols[h]] = uw[n][:, :DN_HEAD_DIM]
            wq_scr[rows_at(c * 2 * DN_CHUNK, DN_CHUNK), cols[h]] = uw[n][:, DN_HEAD_DIM:].astype(BF16)
            wq_scr[rows_at(c * 2 * DN_CHUNK + DN_CHUNK, DN_CHUNK), cols[h]] = (
                q[n] * eg_b[n]).astype(BF16)
            g_last_b = jnp.broadcast_to(gc_b[n][DN_CHUNK - 1:DN_CHUNK, :], (DN_CHUNK, DN_HEAD_DIM))
            kd_scr[rows[n], cols[h]] = (k[n] * jnp.exp(g_last_b - gc_b[n])).astype(BF16)
            gl_scr[c, h:h + 1, :] = eg_b[n][DN_CHUNK - 1:DN_CHUNK, :]

    normw = normw_ref[...]

    def scan(c):
        rows = pl.ds(c * DN_CHUNK, DN_CHUNK)
        wq_rows = pl.ds(c * 2 * DN_CHUNK, 2 * DN_CHUNK)
        gl_all = gl_scr[c]
        state = [state_scr[h] for h in heads]
        state_b = [x.astype(BF16) for x in state]
        ws_qs = [jnp.dot(wq_scr[wq_rows, cols[h]], state_b[h], preferred_element_type=F32)
                 for h in heads]
        yield
        v_new_b = [(u_scr[rows, cols[h]] - ws_qs[h][:DN_CHUNK]).astype(BF16) for h in heads]
        intra = [jnp.dot(qk_scr[h, rows, :], v_new_b[h], preferred_element_type=F32) for h in heads]
        d_state = [lax.dot_general(kd_scr[rows, cols[h]], v_new_b[h], (((0,), (0,)), ((), ())),
                                   preferred_element_type=F32) for h in heads]
        yield
        for h in heads:
            state_scr[h] = state[h] * gl_all[h:h + 1, :] + d_state[h]
            o = ws_qs[h][DN_CHUNK:] + intra[h]
            var = jnp.mean(o * o, axis=-1, keepdims=True)
            o = (o * lax.rsqrt(var + NORM_EPS)) * normw
            out_ref[rows, cols[h]] = (o * _silu(z_ref[rows, cols[h]].astype(F32))).astype(out_ref.dtype)

    def scans(chunks):
        for c in chunks:
            yield from scan(c)

    assert n_chunks % DN_CHUNK_GROUPS == 0
    per_group = n_chunks // DN_CHUNK_GROUPS
    _interleave(prep(0, per_group))
    for c0 in range(per_group, n_chunks, per_group):
        _interleave(prep(c0, per_group), scans(range(c0 - per_group, c0)))
    _interleave(scans(range(n_chunks - per_group, n_chunks)))


def _deltanet(q4d, k4d, v4d, rest3d, small3d, small_t3, conv_w, alog_r, dtb_r, alog_c, dtb_c, norm_w, tc):
    b_sz, s_len, _ = rest3d.shape
    n_t = s_len // tc
    n_chunks = tc // DN_CHUNK
    halo_blocks = tc // SUBLANES

    conv_w3 = conv_w.reshape(DN_CONV, conv_w.shape[1] // DN_HEAD_DIM, DN_HEAD_DIM)

    tile_spec = pl.BlockSpec((None, tc, DN_HEADS, DN_HEAD_DIM), lambda b, i: (b, i, 0, 0))
    halo_spec = pl.BlockSpec((None, SUBLANES, DN_HEADS, DN_HEAD_DIM),
                             lambda b, i: (b, jnp.maximum(i * halo_blocks - 1, 0), 0, 0))

    def const_spec(shape):
        return pl.BlockSpec(shape, lambda b, i: tuple(0 for _ in shape))

    in_specs = [
        tile_spec, tile_spec, tile_spec,
        halo_spec, halo_spec, halo_spec,
        pl.BlockSpec((None, tc, DN_WIDTH), lambda b, i: (b, i, 0)),
        pl.BlockSpec((None, tc, SMALL_W), lambda b, i: (b, i, 0)),
        pl.BlockSpec((2 * DN_HEADS, tc), lambda b, i: (0, b * n_t + i)),
        pl.BlockSpec((DN_CONV, DN_HEADS, DN_HEAD_DIM), lambda b, i: (0, 0, 0)),
        pl.BlockSpec((DN_CONV, DN_HEADS, DN_HEAD_DIM), lambda b, i: (0, 1, 0)),
        pl.BlockSpec((DN_CONV, DN_HEADS, DN_HEAD_DIM), lambda b, i: (0, 2, 0)),
        const_spec((1, SMALL_W)), const_spec((1, SMALL_W)),
        const_spec((2 * DN_HEADS, 1)), const_spec((2 * DN_HEADS, 1)),
        const_spec((1, DN_HEAD_DIM)),
    ]
    scratch = [
        pltpu.VMEM((DN_HEADS, DN_HEAD_DIM, DN_HEAD_DIM), F32),
        pltpu.VMEM((tc * DN_HEADS, DN_HEAD_DIM), F32),
        pltpu.VMEM((tc * DN_HEADS, DN_HEAD_DIM), F32),
        pltpu.VMEM((tc * DN_HEADS, DN_HEAD_DIM), F32),
        pltpu.VMEM((tc, DN_WIDTH), F32),
        pltpu.VMEM((2 * tc, DN_WIDTH), BF16),
        pltpu.VMEM((tc, DN_WIDTH), BF16),
        pltpu.VMEM((DN_HEADS, tc, DN_CHUNK), BF16),
        pltpu.VMEM((n_chunks, DN_HEADS, DN_HEAD_DIM), F32),
        pltpu.VMEM((tc, SMALL_W), F32),
        pltpu.VMEM((tc, SMALL_W), F32),
        pltpu.VMEM((n_chunks, 2 * DN_HEADS, DN_CHUNK), F32),
    ]
    return pl.pallas_call(
        _deltanet_kernel,
        grid=(b_sz, n_t),
        in_specs=in_specs,
        out_specs=pl.BlockSpec((None, tc, DN_WIDTH), lambda b, i: (b, i, 0)),
        out_shape=jax.ShapeDtypeStruct((b_sz, s_len, DN_WIDTH), BF16),
        scratch_shapes=scratch,
        compiler_params=pltpu.CompilerParams(
            dimension_semantics=("parallel", "arbitrary"), vmem_limit_bytes=VMEM_LIMIT),
        name="deltanet",
    )(q4d, k4d, v4d, q4d, k4d, v4d, rest3d, small3d, small_t3,
      conv_w3, conv_w3, conv_w3, alog_r, dtb_r, alog_c, dtb_c, norm_w)


def _moba_prep_kernel(q_ref, k_ref, v_ref, cos_ref, sin_ref, qat_ref, ka_ref, vt_ref, kmean_scr):
    blk = pl.program_id(1)
    n_rows = q_ref.shape[0]

    @pl.when(blk == 0)
    def _():
        kmean_scr[...] = jnp.zeros_like(kmean_scr)

    cos = cos_ref[...]
    sin = sin_ref[...]
    half = MOBA_HEAD_DIM // 2
    lane = lax.broadcasted_iota(jnp.int32, (n_rows, MOBA_HEAD_DIM), 1)
    onehot = (lane == blk).astype(BF16)
    km_rows = kmean_scr.shape[1]
    blk_row = lax.broadcasted_iota(jnp.int32, (km_rows, n_rows), 0)
    blk_row_f = blk_row.astype(F32)
    mean_row = lax.broadcasted_iota(jnp.int32, (km_rows, MOBA_HEAD_DIM), 0)
    pen_pad = jnp.zeros((MOBA_HEAD_DIM - km_rows, n_rows), BF16)
    ones_rows = (lax.broadcasted_iota(jnp.int32, (BF16_ROWS, n_rows), 0) == 0).astype(BF16)

    for h in range(MOBA_HEADS):
        cols = slice(h * MOBA_HEAD_DIM, (h + 1) * MOBA_HEAD_DIM)
        q = q_ref[:, cols].astype(F32)
        k = k_ref[:, cols].astype(F32)
        q = q * cos + pltpu.roll(q, half, 1) * sin
        k = k * cos + pltpu.roll(k, half, 1) * sin
        q_t = q.T

        kmean = kmean_scr[h]
        gate = _dot_split(kmean, q_t)
        gate = jnp.where(blk_row < blk, gate, NEG_INF)
        sel = blk_row == blk
        for _ in range(MOBA_TOPK):
            mx = jnp.max(gate, axis=0, keepdims=True)
            first_idx = jnp.min(jnp.where(gate == mx, blk_row_f, float(km_rows)),
                                axis=0, keepdims=True)
            hit = blk_row_f == first_idx
            sel = sel | (hit & (mx > NEG_INF))
            gate = jnp.where(hit, NEG_INF, gate)
        qat_ref[h * AUG_W:h * AUG_W + MOBA_HEAD_DIM, :] = (q_t * MOBA_EXP2_SCALE).astype(BF16)
        pen0 = h * AUG_W + MOBA_HEAD_DIM
        qat_ref[pen0:pen0 + km_rows, :] = jnp.where(sel, 0.0, MASK_PENALTY).astype(BF16)
        qat_ref[pen0 + km_rows:(h + 1) * AUG_W, :] = pen_pad
        ka_ref[:, h * AUG_W:h * AUG_W + MOBA_HEAD_DIM] = k.astype(BF16)
        ka_ref[:, h * AUG_W + MOBA_HEAD_DIM:(h + 1) * AUG_W] = onehot
        vt_ref[h, 0:MOBA_HEAD_DIM, :] = v_ref[:, cols].astype(F32).T.astype(BF16)
        vt_ref[h, MOBA_HEAD_DIM:VT_ROWS, :] = ones_rows

        k_mean_row = jnp.mean(k, axis=0, keepdims=True)
        kmean_scr[h] = jnp.where(mean_row == blk, k_mean_row, kmean)


def _moba_prep(main3d, cos_t, sin_t):
    b_sz, s_len, _ = main3d.shape
    n_blk = s_len // MOBA_BLOCK
    km_rows = -(-n_blk // BF16_ROWS) * BF16_ROWS
    assert km_rows <= MOBA_HEAD_DIM

    def tile_spec(cb):
        return pl.BlockSpec((None, MOBA_BLOCK, MOBA_WIDTH), lambda b, i, cb=cb: (b, i, cb))

    tab_spec = pl.BlockSpec((MOBA_BLOCK, MOBA_HEAD_DIM), lambda b, i: (i, 0))
    return pl.pallas_call(
        _moba_prep_kernel,
        grid=(b_sz, n_blk),
        in_specs=[tile_spec(1), tile_spec(2), tile_spec(3), tab_spec, tab_spec],
        out_specs=[
            pl.BlockSpec((None, MOBA_HEADS * AUG_W, MOBA_BLOCK), lambda b, i: (b, 0, i)),
            pl.BlockSpec((None, MOBA_BLOCK, MOBA_HEADS * AUG_W), lambda b, i: (b, i, 0)),
            pl.BlockSpec((None, MOBA_HEADS, None, VT_ROWS, MOBA_BLOCK),
                         lambda b, i: (b, 0, i, 0, 0)),
        ],
        out_shape=[
            jax.ShapeDtypeStruct((b_sz, MOBA_HEADS * AUG_W, s_len), BF16),
            jax.ShapeDtypeStruct((b_sz, s_len, MOBA_HEADS * AUG_W), BF16),
            jax.ShapeDtypeStruct((b_sz, MOBA_HEADS, n_blk, VT_ROWS, MOBA_BLOCK), BF16),
        ],
        scratch_shapes=[pltpu.VMEM((MOBA_HEADS, km_rows, MOBA_HEAD_DIM), F32)],
        compiler_params=pltpu.CompilerParams(
            dimension_semantics=("parallel", "arbitrary"), vmem_limit_bytes=VMEM_LIMIT),
        name="moba_prep",
    )(main3d, main3d, main3d, cos_t, sin_t)


def _moba_attn_kernel(qat_ref, ka_ref, vt_ref, out_ref, s_scr, p_scr, acc_scr, *, heads_per_step):
    n_q = qat_ref.shape[1]
    blocks_per_tile = n_q // MOBA_BLOCK
    first_blk = pl.program_id(2) * blocks_per_tile
    heads = range(heads_per_step)
    q_t = [qat_ref[h * AUG_W:(h + 1) * AUG_W, :] for h in heads]

    def scores(j):
        rows = pl.ds(pl.multiple_of(j * MOBA_BLOCK, MOBA_BLOCK), MOBA_BLOCK)
        return [jnp.dot(ka_ref[rows, h * AUG_W:(h + 1) * AUG_W], q_t[h],
                        preferred_element_type=F32) for h in heads]

    key = lax.broadcasted_iota(jnp.int32, (MOBA_BLOCK, n_q), 0)
    qry = lax.broadcasted_iota(jnp.int32, (MOBA_BLOCK, n_q), 1)
    m = None
    for r in range(blocks_per_tile):
        s = [jnp.where(key + r * MOBA_BLOCK <= qry, x, NEG_INF) for x in scores(first_blk + r)]
        if r == 0:
            m = [jnp.max(x, axis=0, keepdims=True) for x in s]
            for h in heads:
                acc_scr[h] = jnp.dot(vt_ref[h, first_blk], jnp.exp2(s[h] - m[h]).astype(BF16),
                                     preferred_element_type=F32)
        else:
            m_new = [jnp.maximum(m[h], jnp.max(s[h], axis=0, keepdims=True)) for h in heads]
            for h in heads:
                acc_scr[h] = (jnp.exp2(m[h] - m_new[h]) * acc_scr[h]
                              + jnp.dot(vt_ref[h, first_blk + r],
                                        jnp.exp2(s[h] - m_new[h]).astype(BF16),
                                        preferred_element_type=F32))
            m = m_new

    s_first = scores(0)
    for h in heads:
        s_scr[0, h] = s_first[h]
        p_scr[0, h] = jnp.zeros(p_scr.shape[2:], BF16)
    alpha = [jnp.ones_like(x) for x in m]

    def half_trip(j, cur, nxt, m, alpha):
        s_cur = [s_scr[cur, h] for h in heads]
        m_new = [jnp.maximum(m[h], jnp.max(s_cur[h], axis=0, keepdims=True)) for h in heads]
        p_new = [jnp.exp2(s_cur[h] - m_new[h]).astype(BF16) for h in heads]
        s_next = scores(jnp.minimum(j + 1, first_blk - 1))
        j_prev = jnp.maximum(j - 1, 0)
        for h in heads:
            acc_scr[h] = alpha[h] * acc_scr[h] + jnp.dot(vt_ref[h, j_prev], p_scr[cur, h],
                                                         preferred_element_type=F32)
        for h in heads:
            p_scr[nxt, h] = p_new[h]
            s_scr[nxt, h] = s_next[h]
        alpha = [jnp.exp2(m[h] - m_new[h]) for h in heads]
        return m_new, alpha

    def body(jj, carry):
        m, alpha = carry
        m, alpha = half_trip(2 * jj, 0, 1, m, alpha)
        return half_trip(2 * jj + 1, 1, 0, m, alpha)

    assert blocks_per_tile % 2 == 0
    m, alpha = lax.fori_loop(0, first_blk // 2, body, (m, alpha))
    j_last = jnp.maximum(first_blk - 1, 0)
    for h in heads:
        a = alpha[h] * acc_scr[h] + jnp.dot(vt_ref[h, j_last], p_scr[0, h],
                                            preferred_element_type=F32)
        o_t = a[:MOBA_HEAD_DIM] / a[MOBA_HEAD_DIM:MOBA_HEAD_DIM + 1]
        out_ref[:, h * MOBA_HEAD_DIM:(h + 1) * MOBA_HEAD_DIM] = o_t.T.astype(out_ref.dtype)


def _moba_attn(qat, ka, vt, heads_per_step, q_tile):
    b_sz, s_len, _ = ka.shape
    n_blk = s_len // MOBA_BLOCK
    hb = heads_per_step
    resident = pl.Buffered(1)
    return pl.pallas_call(
        functools.partial(_moba_attn_kernel, heads_per_step=hb),
        grid=(b_sz, MOBA_HEADS // hb, s_len // q_tile),
        in_specs=[
            pl.BlockSpec((None, hb * AUG_W, q_tile), lambda b, g, i: (b, g, i)),
            pl.BlockSpec((None, s_len, hb * AUG_W), lambda b, g, i: (b, 0, g),
                         pipeline_mode=resident),
            pl.BlockSpec((None, hb, n_blk, VT_ROWS, MOBA_BLOCK),
                         lambda b, g, i: (b, g, 0, 0, 0), pipeline_mode=resident),
        ],
        out_specs=pl.BlockSpec((None, q_tile, hb * MOBA_HEAD_DIM), lambda b, g, i: (b, i, g)),
        out_shape=jax.ShapeDtypeStruct((b_sz, s_len, MOBA_WIDTH), BF16),
        scratch_shapes=[pltpu.VMEM((2, hb, MOBA_BLOCK, q_tile), F32),
                        pltpu.VMEM((2, hb, MOBA_BLOCK, q_tile), BF16),
                        pltpu.VMEM((hb, VT_ROWS, q_tile), F32)],
        compiler_params=pltpu.CompilerParams(
            dimension_semantics=("parallel", "parallel", "arbitrary"),
            vmem_limit_bytes=VMEM_LIMIT),
        name="moba_attn",
    )(qat, ka, vt)


def _out_proj_kernel(x_ref, odn_ref, omb_ref, gdn_ref, gmb_ref, wdn_ref, wmb_ref, wo_ref, h_ref):
    y_dn = jnp.dot(odn_ref[...], wdn_ref[...], preferred_element_type=F32)
    y_mb = jnp.dot(omb_ref[...], wmb_ref[...], preferred_element_type=F32)
    merged = (_sigmoid(gdn_ref[...].astype(F32)) * y_dn
              + _sigmoid(gmb_ref[...].astype(F32)) * y_mb)
    h_ref[...] = x_ref[...] + jnp.dot(merged.astype(BF16), wo_ref[...], preferred_element_type=F32)


def _out_proj(x2d, o_dn, o_mb, main2d, w_dn, w_mb, w_o, tm):
    t, d = x2d.shape
    row_spec = pl.BlockSpec((tm, d), lambda i: (i, 0))
    w_spec = pl.BlockSpec((d, d), lambda i: (0, 0))
    return pl.pallas_call(
        _out_proj_kernel,
        grid=(t // tm,),
        in_specs=[row_spec, row_spec, row_spec,
                  pl.BlockSpec((tm, d), lambda i: (i, 4)),
                  pl.BlockSpec((tm, d), lambda i: (i, 5)),
                  w_spec, w_spec, w_spec],
        out_specs=row_spec,
        out_shape=jax.ShapeDtypeStruct((t, d), F32),
        compiler_params=pltpu.CompilerParams(
            dimension_semantics=("parallel",), vmem_limit_bytes=VMEM_LIMIT),
        name="out_proj",
    )(x2d, o_dn, o_mb, main2d, main2d, w_dn, w_mb, w_o)


def _conv_ffn_kernel(h_ref, nw_ref, wup_ref, cw_ref, cb_ref, wdown_ref, fw_ref, out_ref,
                     u_scr, *, tiles_per_seq, n_split):
    tm = h_ref.shape[0]
    seq_start = (pl.program_id(0) % tiles_per_seq) == 0
    h = h_ref[...]
    var = jnp.mean(h * h, axis=-1, keepdims=True)
    n = ((h * lax.rsqrt(var + NORM_EPS)) * nw_ref[...]).astype(BF16)

    width = D_FF // n_split
    base = SUBLANES - (FFN_CONV - 1)
    acc = jnp.zeros((tm, D_MODEL), F32)
    for part in range(n_split):
        halves = []
        for half in range(2):
            c0 = half * D_FF + part * width
            slot = 2 * part + half
            @pl.when(seq_start)
            def _(slot=slot):
                u_scr[slot, 0:SUBLANES, :] = jnp.zeros((SUBLANES, width), F32)

            @pl.when(jnp.logical_not(seq_start))
            def _(slot=slot):
                u_scr[slot, 0:SUBLANES, :] = u_scr[slot, tm:tm + SUBLANES, :]

            u_scr[slot, SUBLANES:SUBLANES + tm, :] = jnp.dot(
                n, wup_ref[:, c0:c0 + width], preferred_element_type=F32)
            y = (u_scr[slot, SUBLANES:SUBLANES + tm, :]
                 * cw_ref[FFN_CONV - 1:FFN_CONV, c0:c0 + width])
            for j in range(FFN_CONV - 1):
                y = y + u_scr[slot, base + j:base + j + tm, :] * cw_ref[j:j + 1, c0:c0 + width]
            halves.append(y + cb_ref[:, c0:c0 + width])
        act = (_silu(halves[0]) * halves[1]).astype(BF16)
        acc = acc + jnp.dot(act, wdown_ref[part * width:(part + 1) * width, :],
                            preferred_element_type=F32)
    h2 = h + acc
    var2 = jnp.mean(h2 * h2, axis=-1, keepdims=True)
    out_ref[...] = (h2 * lax.rsqrt(var2 + NORM_EPS)) * fw_ref[...]


def _conv_ffn(h2d, norm_w, w_up, conv_w, conv_b, w_down, final_w, tm, s_len, n_split):
    t, d = h2d.shape
    width = D_FF // n_split
    kern = functools.partial(_conv_ffn_kernel, tiles_per_seq=s_len // tm, n_split=n_split)

    def const_spec(shape):
        return pl.BlockSpec(shape, lambda i: (0, 0), pipeline_mode=pl.Buffered(1))

    return pl.pallas_call(
        kern,
        grid=(t // tm,),
        in_specs=[
            pl.BlockSpec((tm, d), lambda i: (i, 0)),
            const_spec((1, d)),
            const_spec((d, 2 * D_FF)),
            const_spec((FFN_CONV, 2 * D_FF)),
            const_spec((1, 2 * D_FF)),
            const_spec((D_FF, d)),
            const_spec((1, d)),
        ],
        out_specs=pl.BlockSpec((tm, d), lambda i: (i, 0)),
        out_shape=jax.ShapeDtypeStruct((t, d), F32),
        scratch_shapes=[pltpu.VMEM((2 * n_split, tm + SUBLANES, width), F32)],
        compiler_params=pltpu.CompilerParams(
            dimension_semantics=("arbitrary",), vmem_limit_bytes=VMEM_LIMIT),
        name="conv_ffn",
    )(h2d, norm_w, w_up, conv_w, conv_b, w_down, final_w)


def _rope_tables(s_len):
    half = MOBA_HEAD_DIM // 2
    inv_freq = np.power(ROPE_THETA, -np.arange(half, dtype=np.float64) / half)
    ang = np.arange(s_len, dtype=np.float64)[:, None] * inv_freq[None, :]
    cos, sin = np.cos(ang), np.sin(ang)
    return (jnp.asarray(np.concatenate([cos, cos], axis=-1), F32),
            jnp.asarray(np.concatenate([-sin, sin], axis=-1), F32))


def _pick_tile(n, want):
    t = min(n, want)
    while n % t:
        t //= 2
    return t


def _layer(x, attn_norm_w, w_in, dn_conv_w, dn_a_log, dn_dt_bias, dn_norm_w, w_dn_out,
           w_moba_out, w_o, ffn_norm_w, w_up, ffn_conv_w, ffn_conv_b, w_down, final_norm_w):
    b_sz, s_len, d = x.shape
    t = b_sz * s_len
    x2d = x.reshape(t, d)

    small_lo = 4 * DN_WIDTH
    small_hi = small_lo + 2 * DN_HEADS
    w_lo = w_in[:, :small_lo].astype(BF16)
    w_hi = w_in[:, small_hi:].astype(BF16)
    w_small_cols = w_in[:, small_lo:small_hi]
    w_small = jnp.pad(w_small_cols, ((0, 0), (0, SMALL_W - 2 * DN_HEADS))).astype(BF16)
    w_small_t = w_small_cols.T.astype(BF16)

    q_tok, k_tok, v_tok, rest, small, small_t = _in_proj(
        x2d, attn_norm_w.reshape(1, d), w_lo, w_hi, w_small, w_small_t, _pick_tile(t, 1024))
    tok4d = (b_sz, s_len, DN_HEADS, DN_HEAD_DIM)
    main3d = rest.reshape(b_sz, s_len, rest.shape[1])
    small3d = small.reshape(b_sz, s_len, SMALL_W)

    pad_r = ((0, 0), (DN_HEADS, SMALL_W - 2 * DN_HEADS))
    alog_r = jnp.pad(dn_a_log.reshape(1, DN_HEADS), pad_r)
    dtb_r = jnp.pad(dn_dt_bias.reshape(1, DN_HEADS), pad_r)
    alog_c = jnp.pad(dn_a_log.reshape(DN_HEADS, 1), ((DN_HEADS, 0), (0, 0)))
    dtb_c = jnp.pad(dn_dt_bias.reshape(DN_HEADS, 1), ((DN_HEADS, 0), (0, 0)))
    o_dn = _deltanet(q_tok.reshape(tok4d), k_tok.reshape(tok4d), v_tok.reshape(tok4d), main3d,
                     small3d, small_t, dn_conv_w, alog_r, dtb_r, alog_c, dtb_c,
                     dn_norm_w.reshape(1, DN_HEAD_DIM), _pick_tile(s_len, 512))

    cos_t, sin_t = _rope_tables(s_len)
    qat, ka, vt = _moba_prep(main3d, cos_t, sin_t)
    o_mb = _moba_attn(qat, ka, vt, 4, _pick_tile(s_len, 2 * MOBA_BLOCK))

    h = _out_proj(x2d, o_dn.reshape(t, DN_WIDTH), o_mb.reshape(t, MOBA_WIDTH), rest,
                  w_dn_out.astype(BF16), w_moba_out.astype(BF16), w_o.astype(BF16),
                  _pick_tile(t, 1024))

    tm_ffn = _pick_tile(s_len, 512)
    out = _conv_ffn(h, ffn_norm_w.reshape(1, d), w_up.astype(BF16), ffn_conv_w,
                    ffn_conv_b.reshape(1, 2 * D_FF), w_down.astype(BF16),
                    final_norm_w.reshape(1, d), tm_ffn, s_len, 1)
    return out.reshape(b_sz, s_len, d)


def kernel(x, attn_norm_w, w_in, dn_conv_w, dn_A_log, dn_dt_bias, dn_norm_w, w_dn_out, w_moba_out,
           w_o, ffn_norm_w, w_up, ffn_conv_w, ffn_conv_b, w_down, final_norm_w):
    depth = w_in.shape[0]
    assert depth == 1, "the final RMSNorm is fused into the layer's conv_ffn call"
    return _layer(x, attn_norm_w[0], w_in[0], dn_conv_w[0], dn_A_log[0], dn_dt_bias[0],
                  dn_norm_w[0], w_dn_out[0], w_moba_out[0], w_o[0], ffn_norm_w[0], w_up[0],
                  ffn_conv_w[0], ffn_conv_b[0], w_down[0], final_norm_w)
```

```python
import functools

import jax
import jax.numpy as jnp
import numpy as np
from jax import lax
from jax.experimental import pallas as pl
from jax.experimental.pallas import tpu as pltpu

D_MODEL = 1024
DN_HEADS = 8
DN_HEAD_DIM = 128
DN_WIDTH = DN_HEADS * DN_HEAD_DIM
DN_CONV = 4
DN_CHUNK = 64
MOBA_HEADS = 8
MOBA_HEAD_DIM = 128
MOBA_WIDTH = MOBA_HEADS * MOBA_HEAD_DIM
MOBA_BLOCK = 256
MOBA_TOPK = 3
ROPE_THETA = 10000.0
D_FF = 2816
FFN_CONV = 3
NORM_EPS = 1e-6

SMALL_W = 128
DN_QKV_BLOCKS = 3
DN_CHUNK_GROUPS = 2
SUBLANES = 8
AUG_W = 2 * MOBA_HEAD_DIM
MASK_PENALTY = -(2.0 ** 100)
BF16_ROWS = 16
VT_ROWS = MOBA_HEAD_DIM + BF16_ROWS
MOBA_EXP2_SCALE = (MOBA_HEAD_DIM ** -0.5) * float(np.log2(np.e))
VMEM_LIMIT = 52 * 1024 * 1024

F32 = jnp.float32
BF16 = jnp.bfloat16
NEG_INF = float("-inf")


def _dot(a, b):
    return jnp.dot(a.astype(BF16), b.astype(BF16), preferred_element_type=F32)


def _dot_nt(a, b):
    return lax.dot_general(a.astype(BF16), b.astype(BF16), (((1,), (1,)), ((), ())),
                           preferred_element_type=F32)


def _dot_tn(a, b):
    return lax.dot_general(a.astype(BF16), b.astype(BF16), (((0,), (0,)), ((), ())),
                           preferred_element_type=F32)


def _dot_split(a, b):
    a_hi = a.astype(BF16)
    b_hi = b.astype(BF16)
    a_lo = (a - a_hi.astype(F32)).astype(BF16)
    b_lo = (b - b_hi.astype(F32)).astype(BF16)
    return (jnp.dot(a_hi, b_hi, preferred_element_type=F32)
            + (jnp.dot(a_hi, b_lo, preferred_element_type=F32)
               + jnp.dot(a_lo, b_hi, preferred_element_type=F32)))


def _sigmoid(x):
    return 1.0 / (1.0 + jnp.exp(-x))


def _silu(x):
    return x * _sigmoid(x)


def _softplus(x):
    return jnp.maximum(x, 0.0) + jnp.log1p(jnp.exp(-jnp.abs(x)))


def _in_proj_kernel_body(x_ref, nw_ref, wlo_ref, whi_ref, ws_ref, wst_ref, q_ref, k_ref, v_ref,
                         rest_ref, small_ref, smallt_ref, n_scr, *, n_lo):
    j = pl.program_id(1)
    tm = x_ref.shape[0]

    @pl.when(j == 0)
    def _():
        x = x_ref[...]
        var = jnp.mean(x * x, axis=-1, keepdims=True)
        n = ((x * lax.rsqrt(var + NORM_EPS)) * nw_ref[...]).astype(BF16)
        n_scr[...] = n
        small_ref[...] = jnp.dot(n, ws_ref[...], preferred_element_type=F32)
        smallt_ref[...] = lax.dot_general(wst_ref[...], n, (((1,), (1,)), ((), ())),
                                          preferred_element_type=F32)

    for blk, tok_ref in enumerate((q_ref, k_ref, v_ref)):
        @pl.when(j == blk)
        def _(tok_ref=tok_ref):
            res = jnp.dot(n_scr[...], wlo_ref[...], preferred_element_type=F32)
            for h in range(DN_HEADS):
                tok_ref[pl.ds(h, tm, stride=DN_HEADS), :] = res[:, h * DN_HEAD_DIM:(h + 1) * DN_HEAD_DIM]

    @pl.when(j == n_lo - 1)
    def _():
        rest_ref[...] = jnp.dot(n_scr[...], wlo_ref[...],
                                preferred_element_type=F32).astype(rest_ref.dtype)

    @pl.when(j >= n_lo)
    def _():
        rest_ref[...] = jnp.dot(n_scr[...], whi_ref[...],
                                preferred_element_type=F32).astype(rest_ref.dtype)


def _in_proj(x2d, norm_w, w_lo, w_hi, w_small, w_small_t, tm):
    t, d = x2d.shape
    tn = D_MODEL
    assert tn == DN_WIDTH
    n_lo = DN_QKV_BLOCKS + 1
    assert w_lo.shape[1] >= n_lo * tn
    n_blocks = n_lo + w_hi.shape[1] // tn
    grid = (t // tm, n_blocks)
    tok_spec = pl.BlockSpec((tm * DN_HEADS, DN_HEAD_DIM), lambda i, j: (i, 0))
    tok_shape = jax.ShapeDtypeStruct((t * DN_HEADS, DN_HEAD_DIM), F32)
    return pl.pallas_call(
        functools.partial(_in_proj_kernel_body, n_lo=n_lo),
        grid=grid,
        in_specs=[
            pl.BlockSpec((tm, d), lambda i, j: (i, 0)),
            pl.BlockSpec((1, d), lambda i, j: (0, 0)),
            pl.BlockSpec((d, tn), lambda i, j: (0, jnp.minimum(j, n_lo - 1))),
            pl.BlockSpec((d, tn), lambda i, j: (0, jnp.maximum(j - n_lo, 0))),
            pl.BlockSpec((d, SMALL_W), lambda i, j: (0, 0)),
            pl.BlockSpec((2 * DN_HEADS, d), lambda i, j: (0, 0)),
        ],
        out_specs=[
            tok_spec, tok_spec, tok_spec,
            pl.BlockSpec((tm, tn), lambda i, j: (i, jnp.maximum(j - DN_QKV_BLOCKS, 0))),
            pl.BlockSpec((tm, SMALL_W), lambda i, j: (i, 0)),
            pl.BlockSpec((2 * DN_HEADS, tm), lambda i, j: (0, i)),
        ],
        out_shape=[
            tok_shape, tok_shape, tok_shape,
            jax.ShapeDtypeStruct((t, (n_blocks - DN_QKV_BLOCKS) * tn), BF16),
            jax.ShapeDtypeStruct((t, SMALL_W), F32),
            jax.ShapeDtypeStruct((2 * DN_HEADS, t), F32),
        ],
        scratch_shapes=[pltpu.VMEM((tm, d), BF16)],
        compiler_params=pltpu.CompilerParams(
            dimension_semantics=("parallel", "arbitrary"), vmem_limit_bytes=VMEM_LIMIT),
        name="in_proj",
    )(x2d, norm_w, w_lo, w_hi, w_small, w_small_t)


def _chunk_cumsum(x, axis):
    idx = lax.broadcasted_iota(jnp.int32, x.shape, axis) % DN_CHUNK
    shift = 1
    while shift < DN_CHUNK:
        x = x + jnp.where(idx >= shift, pltpu.roll(x, shift, axis), 0.0)
        shift *= 2
    return x


def _unit_lower_inverse(mats, row, col):
    c = mats[0].shape[0]
    eye = (row == col).astype(F32)
    blk = 8
    diag_blk = (row // blk) == (col // blk)
    n = [jnp.where(diag_blk, -a, 0.0) for a in mats]
    n2 = [_dot(x, x) for x in n]
    yield
    n4 = [_dot(x, x) for x in n2]
    t = [_dot(eye + x, eye + y) for x, y in zip(n, n2)]
    yield
    t = [_dot(x, eye + y) for x, y in zip(t, n4)]
    yield
    while blk < c:
        off = ((row // (2 * blk)) == (col // (2 * blk))) & ((row // blk) != (col // blk))
        a_t = [_dot(jnp.where(off, a, 0.0), x) for a, x in zip(mats, t)]
        yield
        t = [x - _dot(x, y) for x, y in zip(t, a_t)]
        yield
        blk *= 2
    return t


def _interleave(*stage_generators):
    live = list(stage_generators)
    while live:
        for g in list(live):
            try:
                next(g)
            except StopIteration:
                live.remove(g)


def _deltanet_kernel(q_ref, k_ref, v_ref, qh_ref, kh_ref, vh_ref, z_ref, small_ref, smallt_ref,
                     cwq_ref, cwk_ref, cwv_ref, alog_r_ref, dtb_r_ref, alog_c_ref, dtb_c_ref,
                     normw_ref, out_ref,
                     state_scr, qc_scr, kc_scr, vc_scr, u_scr, wq_scr, kd_scr,
                     qk_scr, gl_scr, bcol_scr, gcol_scr, grow_scr):
    tc = q_ref.shape[0]
    n_chunks = tc // DN_CHUNK
    first = pl.program_id(1) == 0
    heads = range(DN_HEADS)
    cols = [slice(h * DN_HEAD_DIM, (h + 1) * DN_HEAD_DIM) for h in heads]

    @pl.when(first)
    def _():
        state_scr[...] = jnp.zeros_like(state_scr)

    def conv_silu_norm(x_ref, halo_ref, cw_ref, dst, l2norm, scale):
        halo = jnp.where(first, 0.0, halo_ref[...])
        xp = jnp.concatenate([halo[SUBLANES - (DN_CONV - 1):], x_ref[...]], axis=0)
        y = xp[DN_CONV - 1:DN_CONV - 1 + tc] * cw_ref[DN_CONV - 1]
        for j in range(DN_CONV - 1):
            y = y + xp[j:j + tc] * cw_ref[j]
        y = _silu(y)
        if l2norm:
            y = y * lax.rsqrt(jnp.sum(y * y, axis=-1, keepdims=True) + NORM_EPS)
            if scale is not None:
                y = y * scale
        dst[...] = y.reshape(tc * DN_HEADS, DN_HEAD_DIM)

    conv_silu_norm(q_ref, qh_ref, cwq_ref, qc_scr, True, DN_HEAD_DIM ** -0.5)
    conv_silu_norm(k_ref, kh_ref, cwk_ref, kc_scr, True, None)
    conv_silu_norm(v_ref, vh_ref, cwv_ref, vc_scr, False, None)

    def head_rows(scr, c, h):
        return scr[pl.ds(c * DN_CHUNK * DN_HEADS + h, DN_CHUNK, stride=DN_HEADS), :]

    small = small_ref[...]
    bcol_scr[...] = _sigmoid(small)
    gcol_scr[...] = _chunk_cumsum(-jnp.exp(alog_r_ref[...]) * _softplus(small + dtb_r_ref[...]), 0)
    small_t = smallt_ref[...]
    g_t = _chunk_cumsum(-jnp.exp(alog_c_ref[...]) * _softplus(small_t + dtb_c_ref[...]), 1)
    for c in range(n_chunks):
        grow_scr[c] = g_t[:, c * DN_CHUNK:(c + 1) * DN_CHUNK]

    row = lax.broadcasted_iota(jnp.int32, (DN_CHUNK, DN_CHUNK), 0)
    col = lax.broadcasted_iota(jnp.int32, (DN_CHUNK, DN_CHUNK), 1)
    lower_incl = row >= col
    lower_strict = row > col

    def rows_at(start, size):
        if isinstance(start, int):
            return pl.ds(start, size)
        return pl.ds(pl.multiple_of(start, DN_CHUNK), size)

    def prep(c0, chunks):
        items = [(c0 + i, h) for i in range(chunks) for h in heads]
        rows = [rows_at(c * DN_CHUNK, DN_CHUNK) for c, _ in items]
        g_rows = [grow_scr[c0 + i] for i in range(chunks)]
        q = [head_rows(qc_scr, c, h) for c, h in items]
        k = [head_rows(kc_scr, c, h) for c, h in items]
        beta = [bcol_scr[r, h:h + 1] for r, (_, h) in zip(rows, items)]
        gc_b = [jnp.broadcast_to(gcol_scr[r, DN_HEADS + h:DN_HEADS + h + 1],
                                 (DN_CHUNK, DN_HEAD_DIM)) for r, (_, h) in zip(rows, items)]
        eg_b = [jnp.exp(x) for x in gc_b]
        decay = [jnp.exp(jnp.where(
            lower_incl,
            gc_b[n][:, :DN_CHUNK] - g_rows[n // DN_HEADS][DN_HEADS + h:DN_HEADS + h + 1, :],
            NEG_INF)) for n, (_, h) in enumerate(items)]
        kb = [x * y for x, y in zip(k, beta)]
        kq = [_dot_nt(jnp.concatenate([kb[n], q[n]], axis=0), k[n]) for n in range(len(items))]
        yield
        strict = [kq[n][:DN_CHUNK] * jnp.where(lower_strict, decay[n], 0.0)
                  for n in range(len(items))]
        for n, (c, h) in enumerate(items):
            qk_scr[h, rows[n], :] = (kq[n][DN_CHUNK:] * decay[n]).astype(BF16)
        t_mat = yield from _unit_lower_inverse(strict, row, col)
        uw = [_dot(t_mat[n], jnp.concatenate([head_rows(vc_scr, c, h) * beta[n], kb[n] * eg_b[n]],
                                             axis=1)) for n, (c, h) in enumerate(items)]
        yield
        for n, (c, h) in enumerate(items):
            u_scr[rows[n], cols[h]] = uw[n][:, :DN_HEAD_DIM]
            wq_scr[rows_at(c * 2 * DN_CHUNK, DN_CHUNK), cols[h]] = uw[n][:, DN_HEAD_DIM:].astype(BF16)
            wq_scr[rows_at(c * 2 * DN_CHUNK + DN_CHUNK, DN_CHUNK), cols[h]] = (
                q[n] * eg_b[n]).astype(BF16)
            g_last_b = jnp.broadcast_to(gc_b[n][DN_CHUNK - 1:DN_CHUNK, :], (DN_CHUNK, DN_HEAD_DIM))
            kd_scr[rows[n], cols[h]] = (k[n] * jnp.exp(g_last_b - gc_b[n])).astype(BF16)
            gl_scr[c, h:h + 1, :] = eg_b[n][DN_CHUNK - 1:DN_CHUNK, :]

    normw = normw_ref[...]

    def scan(c):
        rows = pl.ds(c * DN_CHUNK, DN_CHUNK)
        wq_rows = pl.ds(c * 2 * DN_CHUNK, 2 * DN_CHUNK)
        gl_all = gl_scr[c]
        state = [state_scr[h] for h in heads]
        state_b = [x.astype(BF16) for x in state]
        ws_qs = [jnp.dot(wq_scr[wq_rows, cols[h]], state_b[h], preferred_element_type=F32)
                 for h in heads]
        yield
        v_new_b = [(u_scr[rows, cols[h]] - ws_qs[h][:DN_CHUNK]).astype(BF16) for h in heads]
        intra = [jnp.dot(qk_scr[h, rows, :], v_new_b[h], preferred_element_type=F32) for h in heads]
        d_state = [lax.dot_general(kd_scr[rows, cols[h]], v_new_b[h], (((0,), (0,)), ((), ())),
                                   preferred_element_type=F32) for h in heads]
        yield
        for h in heads:
            state_scr[h] = state[h] * gl_all[h:h + 1, :] + d_state[h]
            o = ws_qs[h][DN_CHUNK:] + intra[h]
            var = jnp.mean(o * o, axis=-1, keepdims=True)
            o = (o * lax.rsqrt(var + NORM_EPS)) * normw
            out_ref[rows, cols[h]] = (o * _silu(z_ref[rows, cols[h]].astype(F32))).astype(out_ref.dtype)

    def scans(chunks):
        for c in chunks:
            yield from scan(c)

    assert n_chunks % DN_CHUNK_GROUPS == 0
    per_group = n_chunks // DN_CHUNK_GROUPS
    _interleave(prep(0, per_group))
    for c0 in range(per_group, n_chunks, per_group):
        _interleave(prep(c0, per_group), scans(range(c0 - per_group, c0)))
    _interleave(scans(range(n_chunks - per_group, n_chunks)))


def _deltanet(q4d, k4d, v4d, rest3d, small3d, small_t3, conv_w, alog_r, dtb_r, alog_c, dtb_c, norm_w, tc):
    b_sz, s_len, _ = rest3d.shape
    n_t = s_len // tc
    n_chunks = tc // DN_CHUNK
    halo_blocks = tc // SUBLANES

    conv_w3 = conv_w.reshape(DN_CONV, conv_w.shape[1] // DN_HEAD_DIM, DN_HEAD_DIM)

    tile_spec = pl.BlockSpec((None, tc, DN_HEADS, DN_HEAD_DIM), lambda b, i: (b, i, 0, 0))
    halo_spec = pl.BlockSpec((None, SUBLANES, DN_HEADS, DN_HEAD_DIM),
                             lambda b, i: (b, jnp.maximum(i * halo_blocks - 1, 0), 0, 0))

    def const_spec(shape):
        return pl.BlockSpec(shape, lambda b, i: tuple(0 for _ in shape))

    in_specs = [
        tile_spec, tile_spec, tile_spec,
        halo_spec, halo_spec, halo_spec,
        pl.BlockSpec((None, tc, DN_WIDTH), lambda b, i: (b, i, 0)),
        pl.BlockSpec((None, tc, SMALL_W), lambda b, i: (b, i, 0)),
        pl.BlockSpec((2 * DN_HEADS, tc), lambda b, i: (0, b * n_t + i)),
        pl.BlockSpec((DN_CONV, DN_HEADS, DN_HEAD_DIM), lambda b, i: (0, 0, 0)),
        pl.BlockSpec((DN_CONV, DN_HEADS, DN_HEAD_DIM), lambda b, i: (0, 1, 0)),
        pl.BlockSpec((DN_CONV, DN_HEADS, DN_HEAD_DIM), lambda b, i: (0, 2, 0)),
        const_spec((1, SMALL_W)), const_spec((1, SMALL_W)),
        const_spec((2 * DN_HEADS, 1)), const_spec((2 * DN_HEADS, 1)),
        const_spec((1, DN_HEAD_DIM)),
    ]
    scratch = [
        pltpu.VMEM((DN_HEADS, DN_HEAD_DIM, DN_HEAD_DIM), F32),
        pltpu.VMEM((tc * DN_HEADS, DN_HEAD_DIM), F32),
        pltpu.VMEM((tc * DN_HEADS, DN_HEAD_DIM), F32),
        pltpu.VMEM((tc * DN_HEADS, DN_HEAD_DIM), F32),
        pltpu.VMEM((tc, DN_WIDTH), F32),
        pltpu.VMEM((2 * tc, DN_WIDTH), BF16),
        pltpu.VMEM((tc, DN_WIDTH), BF16),
        pltpu.VMEM((DN_HEADS, tc, DN_CHUNK), BF16),
        pltpu.VMEM((n_chunks, DN_HEADS, DN_HEAD_DIM), F32),
        pltpu.VMEM((tc, SMALL_W), F32),
        pltpu.VMEM((tc, SMALL_W), F32),
        pltpu.VMEM((n_chunks, 2 * DN_HEADS, DN_CHUNK), F32),
    ]
    return pl.pallas_call(
        _deltanet_kernel,
        grid=(b_sz, n_t),
        in_specs=in_specs,
        out_specs=pl.BlockSpec((None, tc, DN_WIDTH), lambda b, i: (b, i, 0)),
        out_shape=jax.ShapeDtypeStruct((b_sz, s_len, DN_WIDTH), BF16),
        scratch_shapes=scratch,
        compiler_params=pltpu.CompilerParams(
            dimension_semantics=("parallel", "arbitrary"), vmem_limit_bytes=VMEM_LIMIT),
        name="deltanet",
    )(q4d, k4d, v4d, q4d, k4d, v4d, rest3d, small3d, small_t3,
      conv_w3, conv_w3, conv_w3, alog_r, dtb_r, alog_c, dtb_c, norm_w)


def _moba_prep_kernel(q_ref, k_ref, v_ref, cos_ref, sin_ref, qat_ref, ka_ref, vt_ref, kmean_scr):
    blk = pl.program_id(1)
    n_rows = q_ref.shape[0]

    @pl.when(blk == 0)
    def _():
        kmean_scr[...] = jnp.zeros_like(kmean_scr)

    cos = cos_ref[...]
    sin = sin_ref[...]
    half = MOBA_HEAD_DIM // 2
    lane = lax.broadcasted_iota(jnp.int32, (n_rows, MOBA_HEAD_DIM), 1)
    onehot = (lane == blk).astype(BF16)
    km_rows = kmean_scr.shape[1]
    blk_row = lax.broadcasted_iota(jnp.int32, (km_rows, n_rows), 0)
    blk_row_f = blk_row.astype(F32)
    mean_row = lax.broadcasted_iota(jnp.int32, (km_rows, MOBA_HEAD_DIM), 0)
    pen_pad = jnp.zeros((MOBA_HEAD_DIM - km_rows, n_rows), BF16)
    ones_rows = (lax.broadcasted_iota(jnp.int32, (BF16_ROWS, n_rows), 0) == 0).astype(BF16)

    for h in range(MOBA_HEADS):
        cols = slice(h * MOBA_HEAD_DIM, (h + 1) * MOBA_HEAD_DIM)
        q = q_ref[:, cols].astype(F32)
        k = k_ref[:, cols].astype(F32)
        q = q * cos + pltpu.roll(q, half, 1) * sin
        k = k * cos + pltpu.roll(k, half, 1) * sin
        q_t = q.T

        kmean = kmean_scr[h]
        gate = _dot_split(kmean, q_t)
        gate = jnp.where(blk_row < blk, gate, NEG_INF)
        sel = blk_row == blk
        for _ in range(MOBA_TOPK):
            mx = jnp.max(gate, axis=0, keepdims=True)
            first_idx = jnp.min(jnp.where(gate == mx, blk_row_f, float(km_rows)),
                                axis=0, keepdims=True)
            hit = blk_row_f == first_idx
            sel = sel | (hit & (mx > NEG_INF))
            gate = jnp.where(hit, NEG_INF, gate)
        qat_ref[h * AUG_W:h * AUG_W + MOBA_HEAD_DIM, :] = (q_t * MOBA_EXP2_SCALE).astype(BF16)
        pen0 = h * AUG_W + MOBA_HEAD_DIM
        qat_ref[pen0:pen0 + km_rows, :] = jnp.where(sel, 0.0, MASK_PENALTY).astype(BF16)
        qat_ref[pen0 + km_rows:(h + 1) * AUG_W, :] = pen_pad
        ka_ref[:, h * AUG_W:h * AUG_W + MOBA_HEAD_DIM] = k.astype(BF16)
        ka_ref[:, h * AUG_W + MOBA_HEAD_DIM:(h + 1) * AUG_W] = onehot
        vt_ref[h, 0:MOBA_HEAD_DIM, :] = v_ref[:, cols].astype(F32).T.astype(BF16)
        vt_ref[h, MOBA_HEAD_DIM:VT_ROWS, :] = ones_rows

        k_mean_row = jnp.mean(k, axis=0, keepdims=True)
        kmean_scr[h] = jnp.where(mean_row == blk, k_mean_row, kmean)


def _moba_prep(main3d, cos_t, sin_t):
    b_sz, s_len, _ = main3d.shape
    n_blk = s_len // MOBA_BLOCK
    km_rows = -(-n_blk // BF16_ROWS) * BF16_ROWS
    assert km_rows <= MOBA_HEAD_DIM

    def tile_spec(cb):
        return pl.BlockSpec((None, MOBA_BLOCK, MOBA_WIDTH), lambda b, i, cb=cb: (b, i, cb))

    tab_spec = pl.BlockSpec((MOBA_BLOCK, MOBA_HEAD_DIM), lambda b, i: (i, 0))
    return pl.pallas_call(
        _moba_prep_kernel,
        grid=(b_sz, n_blk),
        in_specs=[tile_spec(1), tile_spec(2), tile_spec(3), tab_spec, tab_spec],
        out_specs=[
            pl.BlockSpec((None, MOBA_HEADS * AUG_W, MOBA_BLOCK), lambda b, i: (b, 0, i)),
            pl.BlockSpec((None, MOBA_BLOCK, MOBA_HEADS * AUG_W), lambda b, i: (b, i, 0)),
            pl.BlockSpec((None, MOBA_HEADS, None, VT_ROWS, MOBA_BLOCK),
                         lambda b, i: (b, 0, i, 0, 0)),
        ],
        out_shape=[
            jax.ShapeDtypeStruct((b_sz, MOBA_HEADS * AUG_W, s_len), BF16),
            jax.ShapeDtypeStruct((b_sz, s_len, MOBA_HEADS * AUG_W), BF16),
            jax.ShapeDtypeStruct((b_sz, MOBA_HEADS, n_blk, VT_ROWS, MOBA_BLOCK), BF16),
        ],
        scratch_shapes=[pltpu.VMEM((MOBA_HEADS, km_rows, MOBA_HEAD_DIM), F32)],
        compiler_params=pltpu.CompilerParams(
            dimension_semantics=("parallel", "arbitrary"), vmem_limit_bytes=VMEM_LIMIT),
        name="moba_prep",
    )(main3d, main3d, main3d, cos_t, sin_t)


def _moba_attn_kernel(qat_ref, ka_ref, vt_ref, out_ref, s_scr, p_scr, acc_scr, *, heads_per_step):
    n_q = qat_ref.shape[1]
    blocks_per_tile = n_q // MOBA_BLOCK
    first_blk = pl.program_id(2) * blocks_per_tile
    heads = range(heads_per_step)
    q_t = [qat_ref[h * AUG_W:(h + 1) * AUG_W, :] for h in heads]

    def scores(j):
        rows = pl.ds(pl.multiple_of(j * MOBA_BLOCK, MOBA_BLOCK), MOBA_BLOCK)
        return [jnp.dot(ka_ref[rows, h * AUG_W:(h + 1) * AUG_W], q_t[h],
                        preferred_element_type=F32) for h in heads]

    key = lax.broadcasted_iota(jnp.int32, (MOBA_BLOCK, n_q), 0)
    qry = lax.broadcasted_iota(jnp.int32, (MOBA_BLOCK, n_q), 1)
    m = None
    for r in range(blocks_per_tile):
        s = [jnp.where(key + r * MOBA_BLOCK <= qry, x, NEG_INF) for x in scores(first_blk + r)]
        if r == 0:
            m = [jnp.max(x, axis=0, keepdims=True) for x in s]
            for h in heads:
                acc_scr[h] = jnp.dot(vt_ref[h, first_blk], jnp.exp2(s[h] - m[h]).astype(BF16),
                                     preferred_element_type=F32)
        else:
            m_new = [jnp.maximum(m[h], jnp.max(s[h], axis=0, keepdims=True)) for h in heads]
            for h in heads:
                acc_scr[h] = (jnp.exp2(m[h] - m_new[h]) * acc_scr[h]
                              + jnp.dot(vt_ref[h, first_blk + r],
                                        jnp.exp2(s[h] - m_new[h]).astype(BF16),
                                        preferred_element_type=F32))
            m = m_new

    s_first = scores(0)
    for h in heads:
        s_scr[0, h] = s_first[h]
        p_scr[0, h] = jnp.zeros(p_scr.shape[2:], BF16)
    alpha = [jnp.ones_like(x) for x in m]

    def half_trip(j, cur, nxt, m, alpha):
        s_cur = [s_scr[cur, h] for h in heads]
        m_new = [jnp.maximum(m[h], jnp.max(s_cur[h], axis=0, keepdims=True)) for h in heads]
        p_new = [jnp.exp2(s_cur[h] - m_new[h]).astype(BF16) for h in heads]
        s_next = scores(jnp.minimum(j + 1, first_blk - 1))
        j_prev = jnp.maximum(j - 1, 0)
        for h in heads:
            acc_scr[h] = alpha[h] * acc_scr[h] + jnp.dot(vt_ref[h, j_prev], p_scr[cur, h],
                                                         preferred_element_type=F32)
        for h in heads:
            p_scr[nxt, h] = p_new[h]
            s_scr[nxt, h] = s_next[h]
        alpha = [jnp.exp2(m[h] - m_new[h]) for h in heads]
        return m_new, alpha

    def body(jj, carry):
        m, alpha = carry
        m, alpha = half_trip(2 * jj, 0, 1, m, alpha)
        return half_trip(2 * jj + 1, 1, 0, m, alpha)

    assert blocks_per_tile % 2 == 0
    m, alpha = lax.fori_loop(0, first_blk // 2, body, (m, alpha))
    j_last = jnp.maximum(first_blk - 1, 0)
    for h in heads:
        a = alpha[h] * acc_scr[h] + jnp.dot(vt_ref[h, j_last], p_scr[0, h],
                                            preferred_element_type=F32)
        o_t = a[:MOBA_HEAD_DIM] / a[MOBA_HEAD_DIM:MOBA_HEAD_DIM + 1]
        out_ref[:, h * MOBA_HEAD_DIM:(h + 1) * MOBA_HEAD_DIM] = o_t.T.astype(out_ref.dtype)


def _moba_attn(qat, ka, vt, heads_per_step, q_tile):
    b_sz, s_len, _ = ka.shape
    n_blk = s_len // MOBA_BLOCK
    hb = heads_per_step
    resident = pl.Buffered(1)
    return pl.pallas_call(
        functools.partial(_moba_attn_kernel, heads_per_step=hb),
        grid=(b_sz, MOBA_HEADS // hb, s_len // q_tile),
        in_specs=[
            pl.BlockSpec((None, hb * AUG_W, q_tile), lambda b, g, i: (b, g, i)),
            pl.BlockSpec((None, s_len, hb * AUG_W), lambda b, g, i: (b, 0, g),
                         pipeline_mode=resident),
            pl.BlockSpec((None, hb, n_blk, VT_ROWS, MOBA_BLOCK),
                         lambda b, g, i: (b, g, 0, 0, 0), pipeline_mode=resident),
        ],
        out_specs=pl.BlockSpec((None, q_tile, hb * MOBA_HEAD_DIM), lambda b, g, i: (b, i, g)),
        out_shape=jax.ShapeDtypeStruct((b_sz, s_len, MOBA_WIDTH), BF16),
        scratch_shapes=[pltpu.VMEM((2, hb, MOBA_BLOCK, q_tile), F32),
                        pltpu.VMEM((2, hb, MOBA_BLOCK, q_tile), BF16),
                        pltpu.VMEM((hb, VT_ROWS, q_tile), F32)],
        compiler_params=pltpu.CompilerParams(
            dimension_semantics=("parallel", "parallel", "arbitrary"),
            vmem_limit_bytes=VMEM_LIMIT),
        name="moba_attn",
    )(qat, ka, vt)


def _out_proj_kernel(x_ref, odn_ref, omb_ref, gdn_ref, gmb_ref, wdn_ref, wmb_ref, wo_ref, h_ref):
    y_dn = jnp.dot(odn_ref[...], wdn_ref[...], preferred_element_type=F32)
    y_mb = jnp.dot(omb_ref[...], wmb_ref[...], preferred_element_type=F32)
    merged = (_sigmoid(gdn_ref[...].astype(F32)) * y_dn
              + _sigmoid(gmb_ref[...].astype(F32)) * y_mb)
    h_ref[...] = x_ref[...] + jnp.dot(merged.astype(BF16), wo_ref[...], preferred_element_type=F32)


def _out_proj(x2d, o_dn, o_mb, main2d, w_dn, w_mb, w_o, tm):
    t, d = x2d.shape
    row_spec = pl.BlockSpec((tm, d), lambda i: (i, 0))
    w_spec = pl.BlockSpec((d, d), lambda i: (0, 0))
    return pl.pallas_call(
        _out_proj_kernel,
        grid=(t // tm,),
        in_specs=[row_spec, row_spec, row_spec,
                  pl.BlockSpec((tm, d), lambda i: (i, 4)),
                  pl.BlockSpec((tm, d), lambda i: (i, 5)),
                  w_spec, w_spec, w_spec],
        out_specs=row_spec,
        out_shape=jax.ShapeDtypeStruct((t, d), F32),
        compiler_params=pltpu.CompilerParams(
            dimension_semantics=("parallel",), vmem_limit_bytes=VMEM_LIMIT),
        name="out_proj",
    )(x2d, o_dn, o_mb, main2d, main2d, w_dn, w_mb, w_o)


def _conv_ffn_kernel(h_ref, nw_ref, wup_ref, cw_ref, cb_ref, wdown_ref, fw_ref, out_ref,
                     u_scr, *, tiles_per_seq, n_split):
    tm = h_ref.shape[0]
    seq_start = (pl.program_id(0) % tiles_per_seq) == 0
    h = h_ref[...]
    var = jnp.mean(h * h, axis=-1, keepdims=True)
    n = ((h * lax.rsqrt(var + NORM_EPS)) * nw_ref[...]).astype(BF16)

    width = D_FF // n_split
    base = SUBLANES - (FFN_CONV - 1)
    acc = jnp.zeros((tm, D_MODEL), F32)
    for part in range(n_split):
        halves = []
        for half in range(2):
            c0 = half * D_FF + part * width
            slot = 2 * part + half
            @pl.when(seq_start)
            def _(slot=slot):
                u_scr[slot, 0:SUBLANES, :] = jnp.zeros((SUBLANES, width), F32)

            @pl.when(jnp.logical_not(seq_start))
            def _(slot=slot):
                u_scr[slot, 0:SUBLANES, :] = u_scr[slot, tm:tm + SUBLANES, :]

            u_scr[slot, SUBLANES:SUBLANES + tm, :] = jnp.dot(
                n, wup_ref[:, c0:c0 + width], preferred_element_type=F32)
            y = (u_scr[slot, SUBLANES:SUBLANES + tm, :]
                 * cw_ref[FFN_CONV - 1:FFN_CONV, c0:c0 + width])
            for j in range(FFN_CONV - 1):
                y = y + u_scr[slot, base + j:base + j + tm, :] * cw_ref[j:j + 1, c0:c0 + width]
            halves.append(y + cb_ref[:, c0:c0 + width])
        act = (_silu(halves[0]) * halves[1]).astype(BF16)
        acc = acc + jnp.dot(act, wdown_ref[part * width:(part + 1) * width, :],
                            preferred_element_type=F32)
    h2 = h + acc
    var2 = jnp.mean(h2 * h2, axis=-1, keepdims=True)
    out_ref[...] = (h2 * lax.rsqrt(var2 + NORM_EPS)) * fw_ref[...]


def _conv_ffn(h2d, norm_w, w_up, conv_w, conv_b, w_down, final_w, tm, s_len, n_split):
    t, d = h2d.shape
    width = D_FF // n_split
    kern = functools.partial(_conv_ffn_kernel, tiles_per_seq=s_len // tm, n_split=n_split)

    def const_spec(shape):
        return pl.BlockSpec(shape, lambda i: (0, 0), pipeline_mode=pl.Buffered(1))

    return pl.pallas_call(
        kern,
        grid=(t // tm,),
        in_specs=[
            pl.BlockSpec((tm, d), lambda i: (i, 0)),
            const_spec((1, d)),
            const_spec((d, 2 * D_FF)),
            const_spec((FFN_CONV, 2 * D_FF)),
            const_spec((1, 2 * D_FF)),
            const_spec((D_FF, d)),
            const_spec((1, d)),
        ],
        out_specs=pl.BlockSpec((tm, d), lambda i: (i, 0)),
        out_shape=jax.ShapeDtypeStruct((t, d), F32),
        scratch_shapes=[pltpu.VMEM((2 * n_split, tm + SUBLANES, width), F32)],
        compiler_params=pltpu.CompilerParams(
            dimension_semantics=("arbitrary",), vmem_limit_bytes=VMEM_LIMIT),
        name="conv_ffn",
    )(h2d, norm_w, w_up, conv_w, conv_b, w_down, final_w)


def _rope_tables(s_len):
    half = MOBA_HEAD_DIM // 2
    inv_freq = np.power(ROPE_THETA, -np.arange(half, dtype=np.float64) / half)
    ang = np.arange(s_len, dtype=np.float64)[:, None] * inv_freq[None, :]
    cos, sin = np.cos(ang), np.sin(ang)
    return (jnp.asarray(np.concatenate([cos, cos], axis=-1), F32),
            jnp.asarray(np.concatenate([-sin, sin], axis=-1), F32))


def _pick_tile(n, want):
    t = min(n, want)
    while n % t:
        t //= 2
    return t


def _layer(x, attn_norm_w, w_in, dn_conv_w, dn_a_log, dn_dt_bias, dn_norm_w, w_dn_out,
           w_moba_out, w_o, ffn_norm_w, w_up, ffn_conv_w, ffn_conv_b, w_down, final_norm_w):
    b_sz, s_len, d = x.shape
    t = b_sz * s_len
    x2d = x.reshape(t, d)

    small_lo = 4 * DN_WIDTH
    small_hi = small_lo + 2 * DN_HEADS
    w_lo = w_in.astype(BF16)
    assert small_lo == (DN_QKV_BLOCKS + 1) * D_MODEL
    w_hi = w_lo[:, small_hi:]
    w_small_cols = w_lo[:, small_lo:small_hi]
    w_small = jnp.pad(w_small_cols, ((0, 0), (0, SMALL_W - 2 * DN_HEADS)))
    w_small_t = w_small_cols.T

    q_tok, k_tok, v_tok, rest, small, small_t = _in_proj(
        x2d, attn_norm_w.reshape(1, d), w_lo, w_hi, w_small, w_small_t, _pick_tile(t, 1024))
    tok4d = (b_sz, s_len, DN_HEADS, DN_HEAD_DIM)
    main3d = rest.reshape(b_sz, s_len, rest.shape[1])
    small3d = small.reshape(b_sz, s_len, SMALL_W)

    pad_r = ((0, 0), (DN_HEADS, SMALL_W - 2 * DN_HEADS))
    alog_r = jnp.pad(dn_a_log.reshape(1, DN_HEADS), pad_r)
    dtb_r = jnp.pad(dn_dt_bias.reshape(1, DN_HEADS), pad_r)
    alog_c = jnp.pad(dn_a_log.reshape(DN_HEADS, 1), ((DN_HEADS, 0), (0, 0)))
    dtb_c = jnp.pad(dn_dt_bias.reshape(DN_HEADS, 1), ((DN_HEADS, 0), (0, 0)))
    o_dn = _deltanet(q_tok.reshape(tok4d), k_tok.reshape(tok4d), v_tok.reshape(tok4d), main3d,
                     small3d, small_t, dn_conv_w, alog_r, dtb_r, alog_c, dtb_c,
                     dn_norm_w.reshape(1, DN_HEAD_DIM), _pick_tile(s_len, 512))

    cos_t, sin_t = _rope_tables(s_len)
    qat, ka, vt = _moba_prep(main3d, cos_t, sin_t)
    o_mb = _moba_attn(qat, ka, vt, 4, _pick_tile(s_len, 2 * MOBA_BLOCK))

    h = _out_proj(x2d, o_dn.reshape(t, DN_WIDTH), o_mb.reshape(t, MOBA_WIDTH), rest,
                  w_dn_out.astype(BF16), w_moba_out.astype(BF16), w_o.astype(BF16),
                  _pick_tile(t, 1024))

    tm_ffn = _pick_tile(s_len, 512)
    out = _conv_ffn(h, ffn_norm_w.reshape(1, d), w_up.astype(BF16), ffn_conv_w,
                    ffn_conv_b.reshape(1, 2 * D_FF), w_down.astype(BF16),
                    final_norm_w.reshape(1, d), tm_ffn, s_len, 1)
    return out.reshape(b_sz, s_len, d)


def kernel(x, attn_norm_w, w_in, dn_conv_w, dn_A_log, dn_dt_bias, dn_norm_w, w_dn_out, w_moba_out,
           w_o, ffn_norm_w, w_up, ffn_conv_w, ffn_conv_b, w_down, final_norm_w):
    depth = w_in.shape[0]
    assert depth == 1, "the final RMSNorm is fused into the layer's conv_ffn call"
    return _layer(x, attn_norm_w[0], w_in[0], dn_conv_w[0], dn_A_log[0], dn_dt_bias[0],
                  dn_norm_w[0], w_dn_out[0], w_moba_out[0], w_o[0], ffn_norm_w[0], w_up[0],
                  ffn_conv_w[0], ffn_conv_b[0], w_down[0], final_norm_w)
```

```python
import functools

import jax
import jax.numpy as jnp
import numpy as np
from jax import lax
from jax.experimental import pallas as pl
from jax.experimental.pallas import tpu as pltpu

D_MODEL = 1024
DN_HEADS = 8
DN_HEAD_DIM = 128
DN_WIDTH = DN_HEADS * DN_HEAD_DIM
DN_CONV = 4
DN_CHUNK = 64
MOBA_HEADS = 8
MOBA_HEAD_DIM = 128
MOBA_WIDTH = MOBA_HEADS * MOBA_HEAD_DIM
MOBA_BLOCK = 256
MOBA_TOPK = 3
ROPE_THETA = 10000.0
D_FF = 2816
FFN_CONV = 3
NORM_EPS = 1e-6

SMALL_W = 128
DN_QKV_BLOCKS = 3
DN_CHUNK_GROUPS = 2
SUBLANES = 8
AUG_W = 2 * MOBA_HEAD_DIM
MASK_PENALTY = -(2.0 ** 100)
BF16_ROWS = 16
VT_ROWS = MOBA_HEAD_DIM + BF16_ROWS
MOBA_EXP2_SCALE = (MOBA_HEAD_DIM ** -0.5) * float(np.log2(np.e))
VMEM_LIMIT = 52 * 1024 * 1024

F32 = jnp.float32
BF16 = jnp.bfloat16
NEG_INF = float("-inf")


def _dot(a, b):
    return jnp.dot(a.astype(BF16), b.astype(BF16), preferred_element_type=F32)


def _dot_nt(a, b):
    return lax.dot_general(a.astype(BF16), b.astype(BF16), (((1,), (1,)), ((), ())),
                           preferred_element_type=F32)


def _dot_tn(a, b):
    return lax.dot_general(a.astype(BF16), b.astype(BF16), (((0,), (0,)), ((), ())),
                           preferred_element_type=F32)


def _dot_split(a, b):
    a_hi = a.astype(BF16)
    b_hi = b.astype(BF16)
    a_lo = (a - a_hi.astype(F32)).astype(BF16)
    b_lo = (b - b_hi.astype(F32)).astype(BF16)
    return (jnp.dot(a_hi, b_hi, preferred_element_type=F32)
            + (jnp.dot(a_hi, b_lo, preferred_element_type=F32)
               + jnp.dot(a_lo, b_hi, preferred_element_type=F32)))


def _sigmoid(x):
    return 1.0 / (1.0 + jnp.exp(-x))


def _silu(x):
    return x * _sigmoid(x)


def _softplus(x):
    return jnp.maximum(x, 0.0) + jnp.log1p(jnp.exp(-jnp.abs(x)))


def _in_proj_kernel_body(x_ref, nw_ref, wlo_ref, whi_ref, ws_ref, wst_ref, tok_ref,
                         rest_ref, small_ref, smallt_ref, n_scr, *, n_lo, n_hi, tn):
    j = pl.program_id(1)
    tm = x_ref.shape[0]

    @pl.when(j == 0)
    def _():
        x = x_ref[...]
        var = jnp.mean(x * x, axis=-1, keepdims=True)
        n = ((x * lax.rsqrt(var + NORM_EPS)) * nw_ref[...]).astype(BF16)
        n_scr[...] = n
        small_ref[...] = jnp.dot(n, ws_ref[...], preferred_element_type=F32)
        smallt_ref[...] = lax.dot_general(wst_ref[...], n, (((1,), (1,)), ((), ())),
                                          preferred_element_type=F32)

    for blk in range(DN_QKV_BLOCKS):
        @pl.when(j == blk)
        def _(blk=blk):
            res = jnp.dot(n_scr[...], wlo_ref[:, blk * tn:(blk + 1) * tn],
                          preferred_element_type=F32)
            for h in range(DN_HEADS):
                tok_ref[pl.ds(h, tm, stride=DN_HEADS), :] = res[:, h * DN_HEAD_DIM:(h + 1) * DN_HEAD_DIM]

    @pl.when(j == n_lo - 1)
    def _():
        rest_ref[...] = jnp.dot(n_scr[...], wlo_ref[:, (n_lo - 1) * tn:n_lo * tn],
                                preferred_element_type=F32).astype(rest_ref.dtype)

    for blk in range(n_hi):
        @pl.when(j == n_lo + blk)
        def _(blk=blk):
            rest_ref[...] = jnp.dot(n_scr[...], whi_ref[:, blk * tn:(blk + 1) * tn],
                                    preferred_element_type=F32).astype(rest_ref.dtype)


def _in_proj(x2d, norm_w, w_lo, w_hi, w_small, w_small_t, tm):
    t, d = x2d.shape
    tn = D_MODEL
    assert tn == DN_WIDTH
    n_lo = w_lo.shape[1] // tn
    assert n_lo == DN_QKV_BLOCKS + 1
    n_hi = w_hi.shape[1] // tn
    n_blocks = n_lo + n_hi
    grid = (t // tm, n_blocks)
    resident = pl.Buffered(1)
    return pl.pallas_call(
        functools.partial(_in_proj_kernel_body, n_lo=n_lo, n_hi=n_hi, tn=tn),
        grid=grid,
        in_specs=[
            pl.BlockSpec((tm, d), lambda i, j: (i, 0)),
            pl.BlockSpec((1, d), lambda i, j: (0, 0)),
            pl.BlockSpec(w_lo.shape, lambda i, j: (0, 0), pipeline_mode=resident),
            pl.BlockSpec(w_hi.shape, lambda i, j: (0, 0), pipeline_mode=resident),
            pl.BlockSpec((d, SMALL_W), lambda i, j: (0, 0)),
            pl.BlockSpec((2 * DN_HEADS, d), lambda i, j: (0, 0)),
        ],
        out_specs=[
            pl.BlockSpec((None, tm * DN_HEADS, DN_HEAD_DIM),
                         lambda i, j: (jnp.minimum(j, DN_QKV_BLOCKS - 1), i, 0)),
            pl.BlockSpec((tm, tn), lambda i, j: (i, jnp.maximum(j - DN_QKV_BLOCKS, 0))),
            pl.BlockSpec((tm, SMALL_W), lambda i, j: (i, 0)),
            pl.BlockSpec((2 * DN_HEADS, tm), lambda i, j: (0, i)),
        ],
        out_shape=[
            jax.ShapeDtypeStruct((DN_QKV_BLOCKS, t * DN_HEADS, DN_HEAD_DIM), F32),
            jax.ShapeDtypeStruct((t, (n_blocks - DN_QKV_BLOCKS) * tn), BF16),
            jax.ShapeDtypeStruct((t, SMALL_W), F32),
            jax.ShapeDtypeStruct((2 * DN_HEADS, t), F32),
        ],
        scratch_shapes=[pltpu.VMEM((tm, d), BF16)],
        compiler_params=pltpu.CompilerParams(
            dimension_semantics=("parallel", "arbitrary"), vmem_limit_bytes=VMEM_LIMIT),
        name="in_proj",
    )(x2d, norm_w, w_lo, w_hi, w_small, w_small_t)


def _chunk_cumsum(x, axis):
    idx = lax.broadcasted_iota(jnp.int32, x.shape, axis) % DN_CHUNK
    shift = 1
    while shift < DN_CHUNK:
        x = x + jnp.where(idx >= shift, pltpu.roll(x, shift, axis), 0.0)
        shift *= 2
    return x


def _unit_lower_inverse(mats, row, col):
    c = mats[0].shape[0]
    eye = (row == col).astype(F32)
    blk = 8
    diag_blk = (row // blk) == (col // blk)
    n = [jnp.where(diag_blk, -a, 0.0) for a in mats]
    n2 = [_dot(x, x) for x in n]
    yield
    n4 = [_dot(x, x) for x in n2]
    t = [_dot(eye + x, eye + y) for x, y in zip(n, n2)]
    yield
    t = [_dot(x, eye + y) for x, y in zip(t, n4)]
    yield
    while blk < c:
        off = ((row // (2 * blk)) == (col // (2 * blk))) & ((row // blk) != (col // blk))
        a_t = [_dot(jnp.where(off, a, 0.0), x) for a, x in zip(mats, t)]
        yield
        t = [x - _dot(x, y) for x, y in zip(t, a_t)]
        yield
        blk *= 2
    return t


def _interleave(*stage_generators):
    live = list(stage_generators)
    while live:
        for g in list(live):
            try:
                next(g)
            except StopIteration:
                live.remove(g)


def _deltanet_kernel(q_ref, k_ref, v_ref, qh_ref, kh_ref, vh_ref, z_ref, small_ref, smallt_ref,
                     cwq_ref, cwk_ref, cwv_ref, alog_r_ref, dtb_r_ref, alog_c_ref, dtb_c_ref,
                     normw_ref, out_ref,
                     state_scr, qc_scr, kc_scr, vc_scr, u_scr, wq_scr, kd_scr,
                     qk_scr, gl_scr, bcol_scr, gcol_scr, grow_scr):
    tc = q_ref.shape[0]
    n_chunks = tc // DN_CHUNK
    first = pl.program_id(1) == 0
    heads = range(DN_HEADS)
    cols = [slice(h * DN_HEAD_DIM, (h + 1) * DN_HEAD_DIM) for h in heads]

    @pl.when(first)
    def _():
        state_scr[...] = jnp.zeros_like(state_scr)

    def conv_silu_norm(x_ref, halo_ref, cw_ref, dst, l2norm, scale):
        halo = jnp.where(first, 0.0, halo_ref[...])
        xp = jnp.concatenate([halo[SUBLANES - (DN_CONV - 1):], x_ref[...]], axis=0)
        y = xp[DN_CONV - 1:DN_CONV - 1 + tc] * cw_ref[DN_CONV - 1]
        for j in range(DN_CONV - 1):
            y = y + xp[j:j + tc] * cw_ref[j]
        y = _silu(y)
        if l2norm:
            y = y * lax.rsqrt(jnp.sum(y * y, axis=-1, keepdims=True) + NORM_EPS)
            if scale is not None:
                y = y * scale
        dst[...] = y.reshape(tc * DN_HEADS, DN_HEAD_DIM)

    conv_silu_norm(q_ref, qh_ref, cwq_ref, qc_scr, True, DN_HEAD_DIM ** -0.5)
    conv_silu_norm(k_ref, kh_ref, cwk_ref, kc_scr, True, None)
    conv_silu_norm(v_ref, vh_ref, cwv_ref, vc_scr, False, None)

    def head_rows(scr, c, h):
        return scr[pl.ds(c * DN_CHUNK * DN_HEADS + h, DN_CHUNK, stride=DN_HEADS), :]

    small = small_ref[...]
    bcol_scr[...] = _sigmoid(small)
    gcol_scr[...] = _chunk_cumsum(-jnp.exp(alog_r_ref[...]) * _softplus(small + dtb_r_ref[...]), 0)
    small_t = smallt_ref[...]
    g_t = _chunk_cumsum(-jnp.exp(alog_c_ref[...]) * _softplus(small_t + dtb_c_ref[...]), 1)
    for c in range(n_chunks):
        grow_scr[c] = g_t[:, c * DN_CHUNK:(c + 1) * DN_CHUNK]

    row = lax.broadcasted_iota(jnp.int32, (DN_CHUNK, DN_CHUNK), 0)
    col = lax.broadcasted_iota(jnp.int32, (DN_CHUNK, DN_CHUNK), 1)
    lower_incl = row >= col
    lower_strict = row > col

    def rows_at(start, size):
        if isinstance(start, int):
            return pl.ds(start, size)
        return pl.ds(pl.multiple_of(start, DN_CHUNK), size)

    def prep(c0, chunks):
        items = [(c0 + i, h) for i in range(chunks) for h in heads]
        rows = [rows_at(c * DN_CHUNK, DN_CHUNK) for c, _ in items]
        g_rows = [grow_scr[c0 + i] for i in range(chunks)]
        q = [head_rows(qc_scr, c, h) for c, h in items]
        k = [head_rows(kc_scr, c, h) for c, h in items]
        beta = [bcol_scr[r, h:h + 1] for r, (_, h) in zip(rows, items)]
        gc_b = [jnp.broadcast_to(gcol_scr[r, DN_HEADS + h:DN_HEADS + h + 1],
                                 (DN_CHUNK, DN_HEAD_DIM)) for r, (_, h) in zip(rows, items)]
        eg_b = [jnp.exp(x) for x in gc_b]
        decay = [jnp.exp(jnp.where(
            lower_incl,
            gc_b[n][:, :DN_CHUNK] - g_rows[n // DN_HEADS][DN_HEADS + h:DN_HEADS + h + 1, :],
            NEG_INF)) for n, (_, h) in enumerate(items)]
        kb = [x * y for x, y in zip(k, beta)]
        kq = [_dot_nt(jnp.concatenate([kb[n], q[n]], axis=0), k[n]) for n in range(len(items))]
        yield
        strict = [kq[n][:DN_CHUNK] * jnp.where(lower_strict, decay[n], 0.0)
                  for n in range(len(items))]
        for n, (c, h) in enumerate(items):
            qk_scr[h, rows[n], :] = (kq[n][DN_CHUNK:] * decay[n]).astype(BF16)
        t_mat = yield from _unit_lower_inverse(strict, row, col)
        uw = [_dot(t_mat[n], jnp.concatenate([head_rows(vc_scr, c, h) * beta[n], kb[n] * eg_b[n]],
                                             axis=1)) for n, (c, h) in enumerate(items)]
        yield
        for n, (c, h) in enumerate(items):
            u_scr[rows[n], cols[h]] = uw[n][:, :DN_HEAD_DIM]
            wq_scr[rows_at(c * 2 * DN_CHUNK, DN_CHUNK), cols[h]] = uw[n][:, DN_HEAD_DIM:].astype(BF16)
            wq_scr[rows_at(c * 2 * DN_CHUNK + DN_CHUNK, DN_CHUNK), cols[h]] = (
                q[n] * eg_b[n]).astype(BF16)
            g_last_b = jnp.broadcast_to(gc_b[n][DN_CHUNK - 1:DN_CHUNK, :], (DN_CHUNK, DN_HEAD_DIM))
            kd_scr[rows[n], cols[h]] = (k[n] * jnp.exp(g_last_b - gc_b[n])).astype(BF16)
            gl_scr[c, h:h + 1, :] = eg_b[n][DN_CHUNK - 1:DN_CHUNK, :]

    normw = normw_ref[...]

    def scan(c):
        rows = pl.ds(c * DN_CHUNK, DN_CHUNK)
        wq_rows = pl.ds(c * 2 * DN_CHUNK, 2 * DN_CHUNK)
        gl_all = gl_scr[c]
        state = [state_scr[h] for h in heads]
        state_b = [x.astype(BF16) for x in state]
        ws_qs = [jnp.dot(wq_scr[wq_rows, cols[h]], state_b[h], preferred_element_type=F32)
                 for h in heads]
        yield
        v_new_b = [(u_scr[rows, cols[h]] - ws_qs[h][:DN_CHUNK]).astype(BF16) for h in heads]
        intra = [jnp.dot(qk_scr[h, rows, :], v_new_b[h], preferred_element_type=F32) for h in heads]
        d_state = [lax.dot_general(kd_scr[rows, cols[h]], v_new_b[h], (((0,), (0,)), ((), ())),
                                   preferred_element_type=F32) for h in heads]
        yield
        for h in heads:
            state_scr[h] = state[h] * gl_all[h:h + 1, :] + d_state[h]
            o = ws_qs[h][DN_CHUNK:] + intra[h]
            var = jnp.mean(o * o, axis=-1, keepdims=True)
            o = (o * lax.rsqrt(var + NORM_EPS)) * normw
            out_ref[rows, cols[h]] = (o * _silu(z_ref[rows, cols[h]].astype(F32))).astype(out_ref.dtype)

    def scans(chunks):
        for c in chunks:
            yield from scan(c)

    assert n_chunks % DN_CHUNK_GROUPS == 0
    per_group = n_chunks // DN_CHUNK_GROUPS
    _interleave(prep(0, per_group))
    for c0 in range(per_group, n_chunks, per_group):
        _interleave(prep(c0, per_group), scans(range(c0 - per_group, c0)))
    _interleave(scans(range(n_chunks - per_group, n_chunks)))


def _deltanet(qkv5d, rest3d, small3d, small_t3, conv_w, alog_r, dtb_r, alog_c, dtb_c, norm_w, tc):
    b_sz, s_len, _ = rest3d.shape
    n_t = s_len // tc
    n_chunks = tc // DN_CHUNK
    halo_blocks = tc // SUBLANES

    conv_w3 = conv_w.reshape(DN_CONV, conv_w.shape[1] // DN_HEAD_DIM, DN_HEAD_DIM)

    def tile_spec(plane):
        return pl.BlockSpec((None, None, tc, DN_HEADS, DN_HEAD_DIM),
                            lambda b, i, plane=plane: (plane, b, i, 0, 0))

    def halo_spec(plane):
        return pl.BlockSpec((None, None, SUBLANES, DN_HEADS, DN_HEAD_DIM),
                            lambda b, i, plane=plane: (plane, b, jnp.maximum(i * halo_blocks - 1, 0),
                                                       0, 0))

    def const_spec(shape):
        return pl.BlockSpec(shape, lambda b, i: tuple(0 for _ in shape))

    in_specs = [
        tile_spec(0), tile_spec(1), tile_spec(2),
        halo_spec(0), halo_spec(1), halo_spec(2),
        pl.BlockSpec((None, tc, DN_WIDTH), lambda b, i: (b, i, 0)),
        pl.BlockSpec((None, tc, SMALL_W), lambda b, i: (b, i, 0)),
        pl.BlockSpec((2 * DN_HEADS, tc), lambda b, i: (0, b * n_t + i)),
        pl.BlockSpec((DN_CONV, DN_HEADS, DN_HEAD_DIM), lambda b, i: (0, 0, 0)),
        pl.BlockSpec((DN_CONV, DN_HEADS, DN_HEAD_DIM), lambda b, i: (0, 1, 0)),
        pl.BlockSpec((DN_CONV, DN_HEADS, DN_HEAD_DIM), lambda b, i: (0, 2, 0)),
        const_spec((1, SMALL_W)), const_spec((1, SMALL_W)),
        const_spec((2 * DN_HEADS, 1)), const_spec((2 * DN_HEADS, 1)),
        const_spec((1, DN_HEAD_DIM)),
    ]
    scratch = [
        pltpu.VMEM((DN_HEADS, DN_HEAD_DIM, DN_HEAD_DIM), F32),
        pltpu.VMEM((tc * DN_HEADS, DN_HEAD_DIM), F32),
        pltpu.VMEM((tc * DN_HEADS, DN_HEAD_DIM), F32),
        pltpu.VMEM((tc * DN_HEADS, DN_HEAD_DIM), F32),
        pltpu.VMEM((tc, DN_WIDTH), F32),
        pltpu.VMEM((2 * tc, DN_WIDTH), BF16),
        pltpu.VMEM((tc, DN_WIDTH), BF16),
        pltpu.VMEM((DN_HEADS, tc, DN_CHUNK), BF16),
        pltpu.VMEM((n_chunks, DN_HEADS, DN_HEAD_DIM), F32),
        pltpu.VMEM((tc, SMALL_W), F32),
        pltpu.VMEM((tc, SMALL_W), F32),
        pltpu.VMEM((n_chunks, 2 * DN_HEADS, DN_CHUNK), F32),
    ]
    return pl.pallas_call(
        _deltanet_kernel,
        grid=(b_sz, n_t),
        in_specs=in_specs,
        out_specs=pl.BlockSpec((None, tc, DN_WIDTH), lambda b, i: (b, i, 0)),
        out_shape=jax.ShapeDtypeStruct((b_sz, s_len, DN_WIDTH), BF16),
        scratch_shapes=scratch,
        compiler_params=pltpu.CompilerParams(
            dimension_semantics=("parallel", "arbitrary"), vmem_limit_bytes=VMEM_LIMIT),
        name="deltanet",
    )(qkv5d, qkv5d, qkv5d, qkv5d, qkv5d, qkv5d, rest3d, small3d, small_t3,
      conv_w3, conv_w3, conv_w3, alog_r, dtb_r, alog_c, dtb_c, norm_w)


def _moba_prep_kernel(q_ref, k_ref, v_ref, cos_ref, sin_ref, qat_ref, ka_ref, vt_ref, kmean_scr):
    blk = pl.program_id(1)
    n_rows = q_ref.shape[0]

    @pl.when(blk == 0)
    def _():
        kmean_scr[...] = jnp.zeros_like(kmean_scr)

    cos = cos_ref[...]
    sin = sin_ref[...]
    half = MOBA_HEAD_DIM // 2
    lane = lax.broadcasted_iota(jnp.int32, (n_rows, MOBA_HEAD_DIM), 1)
    onehot = (lane == blk).astype(BF16)
    km_rows = kmean_scr.shape[1]
    blk_row = lax.broadcasted_iota(jnp.int32, (km_rows, n_rows), 0)
    blk_row_f = blk_row.astype(F32)
    mean_row = lax.broadcasted_iota(jnp.int32, (km_rows, MOBA_HEAD_DIM), 0)
    pen_pad = jnp.zeros((MOBA_HEAD_DIM - km_rows, n_rows), BF16)
    ones_rows = (lax.broadcasted_iota(jnp.int32, (BF16_ROWS, n_rows), 0) == 0).astype(BF16)

    for h in range(MOBA_HEADS):
        cols = slice(h * MOBA_HEAD_DIM, (h + 1) * MOBA_HEAD_DIM)
        q = q_ref[:, cols].astype(F32)
        k = k_ref[:, cols].astype(F32)
        q = q * cos + pltpu.roll(q, half, 1) * sin
        k = k * cos + pltpu.roll(k, half, 1) * sin
        q_t = q.T

        kmean = kmean_scr[h]
        gate = _dot_split(kmean, q_t)
        gate = jnp.where(blk_row < blk, gate, NEG_INF)
        sel = blk_row == blk
        for _ in range(MOBA_TOPK):
            mx = jnp.max(gate, axis=0, keepdims=True)
            first_idx = jnp.min(jnp.where(gate == mx, blk_row_f, float(km_rows)),
                                axis=0, keepdims=True)
            hit = blk_row_f == first_idx
            sel = sel | (hit & (mx > NEG_INF))
            gate = jnp.where(hit, NEG_INF, gate)
        qat_ref[h * AUG_W:h * AUG_W + MOBA_HEAD_DIM, :] = (q_t * MOBA_EXP2_SCALE).astype(BF16)
        pen0 = h * AUG_W + MOBA_HEAD_DIM
        qat_ref[pen0:pen0 + km_rows, :] = jnp.where(sel, 0.0, MASK_PENALTY).astype(BF16)
        qat_ref[pen0 + km_rows:(h + 1) * AUG_W, :] = pen_pad
        ka_ref[:, h * AUG_W:h * AUG_W + MOBA_HEAD_DIM] = k.astype(BF16)
        ka_ref[:, h * AUG_W + MOBA_HEAD_DIM:(h + 1) * AUG_W] = onehot
        vt_ref[h, 0:MOBA_HEAD_DIM, :] = v_ref[:, cols].astype(F32).T.astype(BF16)
        vt_ref[h, MOBA_HEAD_DIM:VT_ROWS, :] = ones_rows

        k_mean_row = jnp.mean(k, axis=0, keepdims=True)
        kmean_scr[h] = jnp.where(mean_row == blk, k_mean_row, kmean)


def _moba_prep(main3d, cos_t, sin_t):
    b_sz, s_len, _ = main3d.shape
    n_blk = s_len // MOBA_BLOCK
    km_rows = -(-n_blk // BF16_ROWS) * BF16_ROWS
    assert km_rows <= MOBA_HEAD_DIM

    def tile_spec(cb):
        return pl.BlockSpec((None, MOBA_BLOCK, MOBA_WIDTH), lambda b, i, cb=cb: (b, i, cb))

    tab_spec = pl.BlockSpec((MOBA_BLOCK, MOBA_HEAD_DIM), lambda b, i: (i, 0))
    return pl.pallas_call(
        _moba_prep_kernel,
        grid=(b_sz, n_blk),
        in_specs=[tile_spec(1), tile_spec(2), tile_spec(3), tab_spec, tab_spec],
        out_specs=[
            pl.BlockSpec((None, MOBA_HEADS * AUG_W, MOBA_BLOCK), lambda b, i: (b, 0, i)),
            pl.BlockSpec((None, MOBA_BLOCK, MOBA_HEADS * AUG_W), lambda b, i: (b, i, 0)),
            pl.BlockSpec((None, MOBA_HEADS, None, VT_ROWS, MOBA_BLOCK),
                         lambda b, i: (b, 0, i, 0, 0)),
        ],
        out_shape=[
            jax.ShapeDtypeStruct((b_sz, MOBA_HEADS * AUG_W, s_len), BF16),
            jax.ShapeDtypeStruct((b_sz, s_len, MOBA_HEADS * AUG_W), BF16),
            jax.ShapeDtypeStruct((b_sz, MOBA_HEADS, n_blk, VT_ROWS, MOBA_BLOCK), BF16),
        ],
        scratch_shapes=[pltpu.VMEM((MOBA_HEADS, km_rows, MOBA_HEAD_DIM), F32)],
        compiler_params=pltpu.CompilerParams(
            dimension_semantics=("parallel", "arbitrary"), vmem_limit_bytes=VMEM_LIMIT),
        name="moba_prep",
    )(main3d, main3d, main3d, cos_t, sin_t)


def _moba_attn_kernel(qat_ref, ka_ref, vt_ref, out_ref, s_scr, p_scr, acc_scr, *, heads_per_step):
    n_q = qat_ref.shape[1]
    blocks_per_tile = n_q // MOBA_BLOCK
    first_blk = pl.program_id(2) * blocks_per_tile
    heads = range(heads_per_step)
    q_t = [qat_ref[h * AUG_W:(h + 1) * AUG_W, :] for h in heads]

    def scores(j):
        rows = pl.ds(pl.multiple_of(j * MOBA_BLOCK, MOBA_BLOCK), MOBA_BLOCK)
        return [jnp.dot(ka_ref[rows, h * AUG_W:(h + 1) * AUG_W], q_t[h],
                        preferred_element_type=F32) for h in heads]

    key = lax.broadcasted_iota(jnp.int32, (MOBA_BLOCK, n_q), 0)
    qry = lax.broadcasted_iota(jnp.int32, (MOBA_BLOCK, n_q), 1)
    m = None
    for r in range(blocks_per_tile):
        s = [jnp.where(key + r * MOBA_BLOCK <= qry, x, NEG_INF) for x in scores(first_blk + r)]
        if r == 0:
            m = [jnp.max(x, axis=0, keepdims=True) for x in s]
            for h in heads:
                acc_scr[h] = jnp.dot(vt_ref[h, first_blk], jnp.exp2(s[h] - m[h]).astype(BF16),
                                     preferred_element_type=F32)
        else:
            m_new = [jnp.maximum(m[h], jnp.max(s[h], axis=0, keepdims=True)) for h in heads]
            for h in heads:
                acc_scr[h] = (jnp.exp2(m[h] - m_new[h]) * acc_scr[h]
                              + jnp.dot(vt_ref[h, first_blk + r],
                                        jnp.exp2(s[h] - m_new[h]).astype(BF16),
                                        preferred_element_type=F32))
            m = m_new

    s_first = scores(0)
    for h in heads:
        s_scr[0, h] = s_first[h]
        p_scr[0, h] = jnp.zeros(p_scr.shape[2:], BF16)
    alpha = [jnp.ones_like(x) for x in m]

    def half_trip(j, cur, nxt, m, alpha):
        s_cur = [s_scr[cur, h] for h in heads]
        m_new = [jnp.maximum(m[h], jnp.max(s_cur[h], axis=0, keepdims=True)) for h in heads]
        p_new = [jnp.exp2(s_cur[h] - m_new[h]).astype(BF16) for h in heads]
        s_next = scores(jnp.minimum(j + 1, first_blk - 1))
        j_prev = jnp.maximum(j - 1, 0)
        for h in heads:
            acc_scr[h] = alpha[h] * acc_scr[h] + jnp.dot(vt_ref[h, j_prev], p_scr[cur, h],
                                                         preferred_element_type=F32)
        for h in heads:
            p_scr[nxt, h] = p_new[h]
            s_scr[nxt, h] = s_next[h]
        alpha = [jnp.exp2(m[h] - m_new[h]) for h in heads]
        return m_new, alpha

    def body(jj, carry):
        m, alpha = carry
        m, alpha = half_trip(2 * jj, 0, 1, m, alpha)
        return half_trip(2 * jj + 1, 1, 0, m, alpha)

    assert blocks_per_tile % 2 == 0
    m, alpha = lax.fori_loop(0, first_blk // 2, body, (m, alpha))
    j_last = jnp.maximum(first_blk - 1, 0)
    for h in heads:
        a = alpha[h] * acc_scr[h] + jnp.dot(vt_ref[h, j_last], p_scr[0, h],
                                            preferred_element_type=F32)
        o_t = a[:MOBA_HEAD_DIM] / a[MOBA_HEAD_DIM:MOBA_HEAD_DIM + 1]
        out_ref[:, h * MOBA_HEAD_DIM:(h + 1) * MOBA_HEAD_DIM] = o_t.T.astype(out_ref.dtype)


def _moba_attn(qat, ka, vt, heads_per_step, q_tile):
    b_sz, s_len, _ = ka.shape
    n_blk = s_len // MOBA_BLOCK
    hb = heads_per_step
    resident = pl.Buffered(1)
    return pl.pallas_call(
        functools.partial(_moba_attn_kernel, heads_per_step=hb),
        grid=(b_sz, MOBA_HEADS // hb, s_len // q_tile),
        in_specs=[
            pl.BlockSpec((None, hb * AUG_W, q_tile), lambda b, g, i: (b, g, i)),
            pl.BlockSpec((None, s_len, hb * AUG_W), lambda b, g, i: (b, 0, g),
                         pipeline_mode=resident),
            pl.BlockSpec((None, hb, n_blk, VT_ROWS, MOBA_BLOCK),
                         lambda b, g, i: (b, g, 0, 0, 0), pipeline_mode=resident),
        ],
        out_specs=pl.BlockSpec((None, q_tile, hb * MOBA_HEAD_DIM), lambda b, g, i: (b, i, g)),
        out_shape=jax.ShapeDtypeStruct((b_sz, s_len, MOBA_WIDTH), BF16),
        scratch_shapes=[pltpu.VMEM((2, hb, MOBA_BLOCK, q_tile), F32),
                        pltpu.VMEM((2, hb, MOBA_BLOCK, q_tile), BF16),
                        pltpu.VMEM((hb, VT_ROWS, q_tile), F32)],
        compiler_params=pltpu.CompilerParams(
            dimension_semantics=("parallel", "parallel", "arbitrary"),
            vmem_limit_bytes=VMEM_LIMIT),
        name="moba_attn",
    )(qat, ka, vt)


def _out_proj_kernel(x_ref, odn_ref, omb_ref, gdn_ref, gmb_ref, wdn_ref, wmb_ref, wo_ref, h_ref):
    y_dn = jnp.dot(odn_ref[...], wdn_ref[...], preferred_element_type=F32)
    y_mb = jnp.dot(omb_ref[...], wmb_ref[...], preferred_element_type=F32)
    merged = (_sigmoid(gdn_ref[...].astype(F32)) * y_dn
              + _sigmoid(gmb_ref[...].astype(F32)) * y_mb)
    h_ref[...] = x_ref[...] + jnp.dot(merged.astype(BF16), wo_ref[...], preferred_element_type=F32)


def _out_proj(x2d, o_dn, o_mb, main2d, w_dn, w_mb, w_o, tm):
    t, d = x2d.shape
    row_spec = pl.BlockSpec((tm, d), lambda i: (i, 0))
    w_spec = pl.BlockSpec((d, d), lambda i: (0, 0))
    return pl.pallas_call(
        _out_proj_kernel,
        grid=(t // tm,),
        in_specs=[row_spec, row_spec, row_spec,
                  pl.BlockSpec((tm, d), lambda i: (i, 4)),
                  pl.BlockSpec((tm, d), lambda i: (i, 5)),
                  w_spec, w_spec, w_spec],
        out_specs=row_spec,
        out_shape=jax.ShapeDtypeStruct((t, d), F32),
        compiler_params=pltpu.CompilerParams(
            dimension_semantics=("parallel",), vmem_limit_bytes=VMEM_LIMIT),
        name="out_proj",
    )(x2d, o_dn, o_mb, main2d, main2d, w_dn, w_mb, w_o)


def _conv_ffn_kernel(h_ref, nw_ref, wup_ref, cw_ref, cb_ref, wdown_ref, fw_ref, out_ref,
                     u_scr, *, tiles_per_seq, n_split):
    tm = h_ref.shape[0]
    seq_start = (pl.program_id(0) % tiles_per_seq) == 0
    h = h_ref[...]
    var = jnp.mean(h * h, axis=-1, keepdims=True)
    n = ((h * lax.rsqrt(var + NORM_EPS)) * nw_ref[...]).astype(BF16)

    width = D_FF // n_split
    base = SUBLANES - (FFN_CONV - 1)
    acc = jnp.zeros((tm, D_MODEL), F32)
    for part in range(n_split):
        halves = []
        for half in range(2):
            c0 = half * D_FF + part * width
            slot = 2 * part + half
            @pl.when(seq_start)
            def _(slot=slot):
                u_scr[slot, 0:SUBLANES, :] = jnp.zeros((SUBLANES, width), F32)

            @pl.when(jnp.logical_not(seq_start))
            def _(slot=slot):
                u_scr[slot, 0:SUBLANES, :] = u_scr[slot, tm:tm + SUBLANES, :]

            u_scr[slot, SUBLANES:SUBLANES + tm, :] = jnp.dot(
                n, wup_ref[:, c0:c0 + width], preferred_element_type=F32)
            y = (u_scr[slot, SUBLANES:SUBLANES + tm, :]
                 * cw_ref[FFN_CONV - 1:FFN_CONV, c0:c0 + width])
            for j in range(FFN_CONV - 1):
                y = y + u_scr[slot, base + j:base + j + tm, :] * cw_ref[j:j + 1, c0:c0 + width]
            halves.append(y + cb_ref[:, c0:c0 + width])
        act = (_silu(halves[0]) * halves[1]).astype(BF16)
        acc = acc + jnp.dot(act, wdown_ref[part * width:(part + 1) * width, :],
                            preferred_element_type=F32)
    h2 = h + acc
    var2 = jnp.mean(h2 * h2, axis=-1, keepdims=True)
    out_ref[...] = (h2 * lax.rsqrt(var2 + NORM_EPS)) * fw_ref[...]


def _conv_ffn(h2d, norm_w, w_up, conv_w, conv_b, w_down, final_w, tm, s_len, n_split):
    t, d = h2d.shape
    width = D_FF // n_split
    kern = functools.partial(_conv_ffn_kernel, tiles_per_seq=s_len // tm, n_split=n_split)

    def const_spec(shape):
        return pl.BlockSpec(shape, lambda i: (0, 0), pipeline_mode=pl.Buffered(1))

    return pl.pallas_call(
        kern,
        grid=(t // tm,),
        in_specs=[
            pl.BlockSpec((tm, d), lambda i: (i, 0)),
            const_spec((1, d)),
            const_spec((d, 2 * D_FF)),
            const_spec((FFN_CONV, 2 * D_FF)),
            const_spec((1, 2 * D_FF)),
            const_spec((D_FF, d)),
            const_spec((1, d)),
        ],
        out_specs=pl.BlockSpec((tm, d), lambda i: (i, 0)),
        out_shape=jax.ShapeDtypeStruct((t, d), F32),
        scratch_shapes=[pltpu.VMEM((2 * n_split, tm + SUBLANES, width), F32)],
        compiler_params=pltpu.CompilerParams(
            dimension_semantics=("arbitrary",), vmem_limit_bytes=VMEM_LIMIT),
        name="conv_ffn",
    )(h2d, norm_w, w_up, conv_w, conv_b, w_down, final_w)


def _rope_tables(s_len):
    half = MOBA_HEAD_DIM // 2
    inv_freq = np.power(ROPE_THETA, -np.arange(half, dtype=np.float64) / half)
    ang = np.arange(s_len, dtype=np.float64)[:, None] * inv_freq[None, :]
    cos, sin = np.cos(ang), np.sin(ang)
    return (jnp.asarray(np.concatenate([cos, cos], axis=-1), F32),
            jnp.asarray(np.concatenate([-sin, sin], axis=-1), F32))


def _pick_tile(n, want):
    t = min(n, want)
    while n % t:
        t //= 2
    return t


def _layer(x, attn_norm_w, w_in, dn_conv_w, dn_a_log, dn_dt_bias, dn_norm_w, w_dn_out,
           w_moba_out, w_o, ffn_norm_w, w_up, ffn_conv_w, ffn_conv_b, w_down, final_norm_w):
    b_sz, s_len, d = x.shape
    t = b_sz * s_len
    x2d = x.reshape(t, d)

    small_lo = 4 * DN_WIDTH
    small_hi = small_lo + 2 * DN_HEADS
    w_lo = w_in[:, :small_lo].astype(BF16)
    w_hi = w_in[:, small_hi:].astype(BF16)
    w_small_cols = w_in[:, small_lo:small_hi]
    w_small = jnp.pad(w_small_cols, ((0, 0), (0, SMALL_W - 2 * DN_HEADS))).astype(BF16)
    w_small_t = w_small_cols.T.astype(BF16)

    qkv_tok, rest, small, small_t = _in_proj(
        x2d, attn_norm_w.reshape(1, d), w_lo, w_hi, w_small, w_small_t, _pick_tile(t, 1024))
    qkv5d = qkv_tok.reshape(DN_QKV_BLOCKS, b_sz, s_len, DN_HEADS, DN_HEAD_DIM)
    main3d = rest.reshape(b_sz, s_len, rest.shape[1])
    small3d = small.reshape(b_sz, s_len, SMALL_W)

    pad_r = ((0, 0), (DN_HEADS, SMALL_W - 2 * DN_HEADS))
    alog_r = jnp.pad(dn_a_log.reshape(1, DN_HEADS), pad_r)
    dtb_r = jnp.pad(dn_dt_bias.reshape(1, DN_HEADS), pad_r)
    alog_c = jnp.pad(dn_a_log.reshape(DN_HEADS, 1), ((DN_HEADS, 0), (0, 0)))
    dtb_c = jnp.pad(dn_dt_bias.reshape(DN_HEADS, 1), ((DN_HEADS, 0), (0, 0)))
    o_dn = _deltanet(qkv5d, main3d, small3d, small_t, dn_conv_w, alog_r, dtb_r, alog_c, dtb_c,
                     dn_norm_w.reshape(1, DN_HEAD_DIM), _pick_tile(s_len, 512))

    cos_t, sin_t = _rope_tables(s_len)
    qat, ka, vt = _moba_prep(main3d, cos_t, sin_t)
    o_mb = _moba_attn(qat, ka, vt, 4, _pick_tile(s_len, 2 * MOBA_BLOCK))

    h = _out_proj(x2d, o_dn.reshape(t, DN_WIDTH), o_mb.reshape(t, MOBA_WIDTH), rest,
                  w_dn_out.astype(BF16), w_moba_out.astype(BF16), w_o.astype(BF16),
                  _pick_tile(t, 1024))

    tm_ffn = _pick_tile(s_len, 512)
    out = _conv_ffn(h, ffn_norm_w.reshape(1, d), w_up.astype(BF16), ffn_conv_w,
                    ffn_conv_b.reshape(1, 2 * D_FF), w_down.astype(BF16),
                    final_norm_w.reshape(1, d), tm_ffn, s_len, 1)
    return out.reshape(b_sz, s_len, d)


def kernel(x, attn_norm_w, w_in, dn_conv_w, dn_A_log, dn_dt_bias, dn_norm_w, w_dn_out, w_moba_out,
           w_o, ffn_norm_w, w_up, ffn_conv_w, ffn_conv_b, w_down, final_norm_w):
    depth = w_in.shape[0]
    assert depth == 1, "the final RMSNorm is fused into the layer's conv_ffn call"
    return _layer(x, attn_norm_w[0], w_in[0], dn_conv_w[0], dn_A_log[0], dn_dt_bias[0],
                  dn_norm_w[0], w_dn_out[0], w_moba_out[0], w_o[0], ffn_norm_w[0], w_up[0],
                  ffn_conv_w[0], ffn_conv_b[0], w_down[0], final_norm_w)
```

```python
import functools

import jax
import jax.numpy as jnp
import numpy as np
from jax import lax
from jax.experimental import pallas as pl
from jax.experimental.pallas import tpu as pltpu

D_MODEL = 1024
DN_HEADS = 8
DN_HEAD_DIM = 128
DN_WIDTH = DN_HEADS * DN_HEAD_DIM
DN_CONV = 4
DN_CHUNK = 64
MOBA_HEADS = 8
MOBA_HEAD_DIM = 128
MOBA_WIDTH = MOBA_HEADS * MOBA_HEAD_DIM
MOBA_BLOCK = 256
MOBA_TOPK = 3
ROPE_THETA = 10000.0
D_FF = 2816
FFN_CONV = 3
NORM_EPS = 1e-6

SMALL_W = 128
DN_QKV_BLOCKS = 3
DN_CHUNK_GROUPS = 3
SUBLANES = 8
AUG_W = 2 * MOBA_HEAD_DIM
MASK_PENALTY = -(2.0 ** 100)
BF16_ROWS = 16
VT_ROWS = MOBA_HEAD_DIM + BF16_ROWS
MOBA_EXP2_SCALE = (MOBA_HEAD_DIM ** -0.5) * float(np.log2(np.e))
VMEM_LIMIT = 52 * 1024 * 1024

F32 = jnp.float32
BF16 = jnp.bfloat16
NEG_INF = float("-inf")
NEG_LOG2_E = -float(np.log2(np.e))


def _dot(a, b):
    return jnp.dot(a.astype(BF16), b.astype(BF16), preferred_element_type=F32)


def _dot_nt(a, b):
    return lax.dot_general(a.astype(BF16), b.astype(BF16), (((1,), (1,)), ((), ())),
                           preferred_element_type=F32)


def _dot_tn(a, b):
    return lax.dot_general(a.astype(BF16), b.astype(BF16), (((0,), (0,)), ((), ())),
                           preferred_element_type=F32)


def _dot_split(a, b):
    a_hi = a.astype(BF16)
    b_hi = b.astype(BF16)
    a_lo = (a - a_hi.astype(F32)).astype(BF16)
    b_lo = (b - b_hi.astype(F32)).astype(BF16)
    return (jnp.dot(a_hi, b_hi, preferred_element_type=F32)
            + (jnp.dot(a_hi, b_lo, preferred_element_type=F32)
               + jnp.dot(a_lo, b_hi, preferred_element_type=F32)))


def _sigmoid(x):
    return 1.0 / (1.0 + jnp.exp2(x * NEG_LOG2_E))


def _silu(x):
    return x * _sigmoid(x)


def _softplus(x):
    return jnp.maximum(x, 0.0) + jnp.log1p(jnp.exp(-jnp.abs(x)))


def _in_proj_kernel_body(x_ref, nw_ref, wlo_ref, whi_ref, ws_ref, wst_ref, tok_ref,
                         rest_ref, small_ref, smallt_ref, n_scr, *, n_lo, n_hi, tn):
    j = pl.program_id(1)
    tm = x_ref.shape[0]

    @pl.when(j == 0)
    def _():
        x = x_ref[...]
        var = jnp.mean(x * x, axis=-1, keepdims=True)
        n = ((x * lax.rsqrt(var + NORM_EPS)) * nw_ref[...]).astype(BF16)
        n_scr[...] = n
        small_ref[...] = jnp.dot(n, ws_ref[...], preferred_element_type=F32)
        smallt_ref[...] = lax.dot_general(wst_ref[...], n, (((1,), (1,)), ((), ())),
                                          preferred_element_type=F32)

    for blk in range(DN_QKV_BLOCKS):
        @pl.when(j == blk)
        def _(blk=blk):
            res = jnp.dot(n_scr[...], wlo_ref[:, blk * tn:(blk + 1) * tn],
                          preferred_element_type=F32)
            for h in range(DN_HEADS):
                tok_ref[pl.ds(h, tm, stride=DN_HEADS), :] = res[:, h * DN_HEAD_DIM:(h + 1) * DN_HEAD_DIM]

    @pl.when(j == n_lo - 1)
    def _():
        rest_ref[...] = jnp.dot(n_scr[...], wlo_ref[:, (n_lo - 1) * tn:n_lo * tn],
                                preferred_element_type=F32).astype(rest_ref.dtype)

    for blk in range(n_hi):
        @pl.when(j == n_lo + blk)
        def _(blk=blk):
            rest_ref[...] = jnp.dot(n_scr[...], whi_ref[:, blk * tn:(blk + 1) * tn],
                                    preferred_element_type=F32).astype(rest_ref.dtype)


def _in_proj(x2d, norm_w, w_lo, w_hi, w_small, w_small_t, tm):
    t, d = x2d.shape
    tn = D_MODEL
    assert tn == DN_WIDTH
    n_lo = w_lo.shape[1] // tn
    assert n_lo == DN_QKV_BLOCKS + 1
    n_hi = w_hi.shape[1] // tn
    n_blocks = n_lo + n_hi
    grid = (t // tm, n_blocks)
    resident = pl.Buffered(1)
    return pl.pallas_call(
        functools.partial(_in_proj_kernel_body, n_lo=n_lo, n_hi=n_hi, tn=tn),
        grid=grid,
        in_specs=[
            pl.BlockSpec((tm, d), lambda i, j: (i, 0)),
            pl.BlockSpec((1, d), lambda i, j: (0, 0)),
            pl.BlockSpec(w_lo.shape, lambda i, j: (0, 0), pipeline_mode=resident),
            pl.BlockSpec(w_hi.shape, lambda i, j: (0, 0), pipeline_mode=resident),
            pl.BlockSpec((d, SMALL_W), lambda i, j: (0, 0)),
            pl.BlockSpec((2 * DN_HEADS, d), lambda i, j: (0, 0)),
        ],
        out_specs=[
            pl.BlockSpec((None, tm * DN_HEADS, DN_HEAD_DIM),
                         lambda i, j: (jnp.minimum(j, DN_QKV_BLOCKS - 1), i, 0)),
            pl.BlockSpec((tm, tn), lambda i, j: (i, jnp.maximum(j - DN_QKV_BLOCKS, 0))),
            pl.BlockSpec((tm, SMALL_W), lambda i, j: (i, 0)),
            pl.BlockSpec((2 * DN_HEADS, tm), lambda i, j: (0, i)),
        ],
        out_shape=[
            jax.ShapeDtypeStruct((DN_QKV_BLOCKS, t * DN_HEADS, DN_HEAD_DIM), F32),
            jax.ShapeDtypeStruct((t, (n_blocks - DN_QKV_BLOCKS) * tn), BF16),
            jax.ShapeDtypeStruct((t, SMALL_W), F32),
            jax.ShapeDtypeStruct((2 * DN_HEADS, t), F32),
        ],
        scratch_shapes=[pltpu.VMEM((tm, d), BF16)],
        compiler_params=pltpu.CompilerParams(
            dimension_semantics=("parallel", "arbitrary"), vmem_limit_bytes=VMEM_LIMIT),
        name="in_proj",
    )(x2d, norm_w, w_lo, w_hi, w_small, w_small_t)


def _chunk_cumsum(x, axis):
    idx = lax.broadcasted_iota(jnp.int32, x.shape, axis) % DN_CHUNK
    shift = 1
    while shift < DN_CHUNK:
        x = x + jnp.where(idx >= shift, pltpu.roll(x, shift, axis), 0.0)
        shift *= 2
    return x


def _unit_lower_inverse(mats, row, col):
    c = mats[0].shape[0]
    eye = (row == col).astype(F32)
    blk = 8
    diag_blk = (row // blk) == (col // blk)
    n = [jnp.where(diag_blk, -a, 0.0) for a in mats]
    n2 = [_dot(x, x) for x in n]
    yield
    n4 = [_dot(x, x) for x in n2]
    t = [_dot(eye + x, eye + y) for x, y in zip(n, n2)]
    yield
    t = [_dot(x, eye + y) for x, y in zip(t, n4)]
    yield
    while blk < c:
        off = ((row // (2 * blk)) == (col // (2 * blk))) & ((row // blk) != (col // blk))
        a_t = [_dot(jnp.where(off, a, 0.0), x) for a, x in zip(mats, t)]
        yield
        t = [x - _dot(x, y) for x, y in zip(t, a_t)]
        yield
        blk *= 2
    return t


def _interleave(*stage_generators):
    live = list(stage_generators)
    while live:
        for g in list(live):
            try:
                next(g)
            except StopIteration:
                live.remove(g)


def _deltanet_kernel(q_ref, k_ref, v_ref, qh_ref, kh_ref, vh_ref, z_ref, small_ref, smallt_ref,
                     cwq_ref, cwk_ref, cwv_ref, alog_r_ref, dtb_r_ref, alog_c_ref, dtb_c_ref,
                     normw_ref, out_ref,
                     state_scr, qc_scr, kc_scr, vc_scr, u_scr, wq_scr, kd_scr,
                     qk_scr, gl_scr, bcol_scr, gcol_scr, grow_scr):
    tc = q_ref.shape[0]
    n_chunks = tc // DN_CHUNK
    first = pl.program_id(1) == 0
    heads = range(DN_HEADS)
    cols = [slice(h * DN_HEAD_DIM, (h + 1) * DN_HEAD_DIM) for h in heads]

    @pl.when(first)
    def _():
        state_scr[...] = jnp.zeros_like(state_scr)

    def conv_silu_norm(x_ref, halo_ref, cw_ref, dst, l2norm, scale):
        halo = jnp.where(first, 0.0, halo_ref[...])
        xp = jnp.concatenate([halo[SUBLANES - (DN_CONV - 1):], x_ref[...]], axis=0)
        y = xp[DN_CONV - 1:DN_CONV - 1 + tc] * cw_ref[DN_CONV - 1]
        for j in range(DN_CONV - 1):
            y = y + xp[j:j + tc] * cw_ref[j]
        y = _silu(y)
        if l2norm:
            y = y * lax.rsqrt(jnp.sum(y * y, axis=-1, keepdims=True) + NORM_EPS)
            if scale is not None:
                y = y * scale
        dst[...] = y.reshape(tc * DN_HEADS, DN_HEAD_DIM)

    conv_silu_norm(q_ref, qh_ref, cwq_ref, qc_scr, True, DN_HEAD_DIM ** -0.5)
    conv_silu_norm(k_ref, kh_ref, cwk_ref, kc_scr, True, None)
    conv_silu_norm(v_ref, vh_ref, cwv_ref, vc_scr, False, None)

    def head_rows(scr, c, h):
        return scr[pl.ds(c * DN_CHUNK * DN_HEADS + h, DN_CHUNK, stride=DN_HEADS), :]

    small = small_ref[...]
    bcol_scr[...] = _sigmoid(small)
    gcol_scr[...] = _chunk_cumsum(-jnp.exp(alog_r_ref[...]) * _softplus(small + dtb_r_ref[...]), 0)
    small_t = smallt_ref[...]
    g_t = _chunk_cumsum(-jnp.exp(alog_c_ref[...]) * _softplus(small_t + dtb_c_ref[...]), 1)
    for c in range(n_chunks):
        grow_scr[c] = g_t[:, c * DN_CHUNK:(c + 1) * DN_CHUNK]

    row = lax.broadcasted_iota(jnp.int32, (DN_CHUNK, DN_CHUNK), 0)
    col = lax.broadcasted_iota(jnp.int32, (DN_CHUNK, DN_CHUNK), 1)
    lower_incl = row >= col
    lower_strict = row > col

    def rows_at(start, size):
        if isinstance(start, int):
            return pl.ds(start, size)
        return pl.ds(pl.multiple_of(start, DN_CHUNK), size)

    def prep(c0, chunks):
        items = [(c0 + i, h) for i in range(chunks) for h in heads]
        rows = [rows_at(c * DN_CHUNK, DN_CHUNK) for c, _ in items]
        g_rows = [grow_scr[c0 + i] for i in range(chunks)]
        q = [head_rows(qc_scr, c, h) for c, h in items]
        k = [head_rows(kc_scr, c, h) for c, h in items]
        beta = [bcol_scr[r, h:h + 1] for r, (_, h) in zip(rows, items)]
        gc_b = [jnp.broadcast_to(gcol_scr[r, DN_HEADS + h:DN_HEADS + h + 1],
                                 (DN_CHUNK, DN_HEAD_DIM)) for r, (_, h) in zip(rows, items)]
        eg_b = [jnp.exp(x) for x in gc_b]
        decay = [jnp.exp(jnp.where(
            lower_incl,
            gc_b[n][:, :DN_CHUNK] - g_rows[n // DN_HEADS][DN_HEADS + h:DN_HEADS + h + 1, :],
            NEG_INF)) for n, (_, h) in enumerate(items)]
        kb = [x * y for x, y in zip(k, beta)]
        kq = [_dot_nt(jnp.concatenate([kb[n], q[n]], axis=0), k[n]) for n in range(len(items))]
        yield
        strict = [kq[n][:DN_CHUNK] * jnp.where(lower_strict, decay[n], 0.0)
                  for n in range(len(items))]
        for n, (c, h) in enumerate(items):
            qk_scr[h, rows[n], :] = (kq[n][DN_CHUNK:] * decay[n]).astype(BF16)
        t_mat = yield from _unit_lower_inverse(strict, row, col)
        uw = [_dot(t_mat[n], jnp.concatenate([head_rows(vc_scr, c, h) * beta[n], kb[n] * eg_b[n]],
                                             axis=1)) for n, (c, h) in enumerate(items)]
        yield
        for n, (c, h) in enumerate(items):
            u_scr[rows[n], cols[h]] = uw[n][:, :DN_HEAD_DIM]
            wq_scr[rows_at(c * 2 * DN_CHUNK, DN_CHUNK), cols[h]] = uw[n][:, DN_HEAD_DIM:].astype(BF16)
            wq_scr[rows_at(c * 2 * DN_CHUNK + DN_CHUNK, DN_CHUNK), cols[h]] = (
                q[n] * eg_b[n]).astype(BF16)
            g_last_b = jnp.broadcast_to(gc_b[n][DN_CHUNK - 1:DN_CHUNK, :], (DN_CHUNK, DN_HEAD_DIM))
            kd_scr[rows[n], cols[h]] = (k[n] * jnp.exp(g_last_b - gc_b[n])).astype(BF16)
            gl_scr[c, h:h + 1, :] = eg_b[n][DN_CHUNK - 1:DN_CHUNK, :]

    normw = normw_ref[...]

    def scan(c):
        rows = pl.ds(c * DN_CHUNK, DN_CHUNK)
        wq_rows = pl.ds(c * 2 * DN_CHUNK, 2 * DN_CHUNK)
        gl_all = gl_scr[c]
        state = [state_scr[h] for h in heads]
        state_b = [x.astype(BF16) for x in state]
        ws_qs = [jnp.dot(wq_scr[wq_rows, cols[h]], state_b[h], preferred_element_type=F32)
                 for h in heads]
        yield
        v_new_b = [(u_scr[rows, cols[h]] - ws_qs[h][:DN_CHUNK]).astype(BF16) for h in heads]
        intra = [jnp.dot(qk_scr[h, rows, :], v_new_b[h], preferred_element_type=F32) for h in heads]
        d_state = [lax.dot_general(kd_scr[rows, cols[h]], v_new_b[h], (((0,), (0,)), ((), ())),
                                   preferred_element_type=F32) for h in heads]
        yield
        for h in heads:
            state_scr[h] = state[h] * gl_all[h:h + 1, :] + d_state[h]
            o = ws_qs[h][DN_CHUNK:] + intra[h]
            var = jnp.mean(o * o, axis=-1, keepdims=True)
            o = (o * lax.rsqrt(var + NORM_EPS)) * normw
            out_ref[rows, cols[h]] = (o * _silu(z_ref[rows, cols[h]].astype(F32))).astype(out_ref.dtype)

    def scans(chunks):
        for c in chunks:
            yield from scan(c)

    n_groups = min(DN_CHUNK_GROUPS, n_chunks)
    starts = [(g * n_chunks + n_groups - 1) // n_groups for g in range(n_groups + 1)]
    _interleave(prep(starts[0], starts[1] - starts[0]))
    for g in range(1, n_groups):
        _interleave(prep(starts[g], starts[g + 1] - starts[g]), scans(range(starts[g - 1], starts[g])))
    _interleave(scans(range(starts[-2], n_chunks)))


def _deltanet(qkv5d, rest3d, small3d, small_t3, conv_w, alog_r, dtb_r, alog_c, dtb_c, norm_w, tc):
    b_sz, s_len, _ = rest3d.shape
    n_t = s_len // tc
    n_chunks = tc // DN_CHUNK
    halo_blocks = tc // SUBLANES

    conv_w3 = conv_w.reshape(DN_CONV, conv_w.shape[1] // DN_HEAD_DIM, DN_HEAD_DIM)

    def tile_spec(plane):
        return pl.BlockSpec((None, None, tc, DN_HEADS, DN_HEAD_DIM),
                            lambda b, i, plane=plane: (plane, b, i, 0, 0))

    def halo_spec(plane):
        return pl.BlockSpec((None, None, SUBLANES, DN_HEADS, DN_HEAD_DIM),
                            lambda b, i, plane=plane: (plane, b, jnp.maximum(i * halo_blocks - 1, 0),
                                                       0, 0))

    def const_spec(shape):
        return pl.BlockSpec(shape, lambda b, i: tuple(0 for _ in shape))

    in_specs = [
        tile_spec(0), tile_spec(1), tile_spec(2),
        halo_spec(0), halo_spec(1), halo_spec(2),
        pl.BlockSpec((None, tc, DN_WIDTH), lambda b, i: (b, i, 0)),
        pl.BlockSpec((None, tc, SMALL_W), lambda b, i: (b, i, 0)),
        pl.BlockSpec((2 * DN_HEADS, tc), lambda b, i: (0, b * n_t + i)),
        pl.BlockSpec((DN_CONV, DN_HEADS, DN_HEAD_DIM), lambda b, i: (0, 0, 0)),
        pl.BlockSpec((DN_CONV, DN_HEADS, DN_HEAD_DIM), lambda b, i: (0, 1, 0)),
        pl.BlockSpec((DN_CONV, DN_HEADS, DN_HEAD_DIM), lambda b, i: (0, 2, 0)),
        const_spec((1, SMALL_W)), const_spec((1, SMALL_W)),
        const_spec((2 * DN_HEADS, 1)), const_spec((2 * DN_HEADS, 1)),
        const_spec((1, DN_HEAD_DIM)),
    ]
    scratch = [
        pltpu.VMEM((DN_HEADS, DN_HEAD_DIM, DN_HEAD_DIM), F32),
        pltpu.VMEM((tc * DN_HEADS, DN_HEAD_DIM), F32),
        pltpu.VMEM((tc * DN_HEADS, DN_HEAD_DIM), F32),
        pltpu.VMEM((tc * DN_HEADS, DN_HEAD_DIM), F32),
        pltpu.VMEM((tc, DN_WIDTH), F32),
        pltpu.VMEM((2 * tc, DN_WIDTH), BF16),
        pltpu.VMEM((tc, DN_WIDTH), BF16),
        pltpu.VMEM((DN_HEADS, tc, DN_CHUNK), BF16),
        pltpu.VMEM((n_chunks, DN_HEADS, DN_HEAD_DIM), F32),
        pltpu.VMEM((tc, SMALL_W), F32),
        pltpu.VMEM((tc, SMALL_W), F32),
        pltpu.VMEM((n_chunks, 2 * DN_HEADS, DN_CHUNK), F32),
    ]
    return pl.pallas_call(
        _deltanet_kernel,
        grid=(b_sz, n_t),
        in_specs=in_specs,
        out_specs=pl.BlockSpec((None, tc, DN_WIDTH), lambda b, i: (b, i, 0)),
        out_shape=jax.ShapeDtypeStruct((b_sz, s_len, DN_WIDTH), BF16),
        scratch_shapes=scratch,
        compiler_params=pltpu.CompilerParams(
            dimension_semantics=("parallel", "arbitrary"), vmem_limit_bytes=VMEM_LIMIT),
        name="deltanet",
    )(qkv5d, qkv5d, qkv5d, qkv5d, qkv5d, qkv5d, rest3d, small3d, small_t3,
      conv_w3, conv_w3, conv_w3, alog_r, dtb_r, alog_c, dtb_c, norm_w)


def _moba_prep_kernel(q_ref, k_ref, v_ref, cos_ref, sin_ref, qat_ref, ka_ref, vt_ref, kmean_scr):
    blk = pl.program_id(1)
    n_rows = q_ref.shape[0]

    @pl.when(blk == 0)
    def _():
        kmean_scr[...] = jnp.zeros_like(kmean_scr)

    cos = cos_ref[...]
    sin = sin_ref[...]
    half = MOBA_HEAD_DIM // 2
    lane = lax.broadcasted_iota(jnp.int32, (n_rows, MOBA_HEAD_DIM), 1)
    onehot = (lane == blk).astype(BF16)
    km_rows = kmean_scr.shape[1]
    blk_row = lax.broadcasted_iota(jnp.int32, (km_rows, n_rows), 0)
    blk_row_f = blk_row.astype(F32)
    mean_row = lax.broadcasted_iota(jnp.int32, (km_rows, MOBA_HEAD_DIM), 0)
    pen_pad = jnp.zeros((MOBA_HEAD_DIM - km_rows, n_rows), BF16)
    ones_rows = (lax.broadcasted_iota(jnp.int32, (BF16_ROWS, n_rows), 0) == 0).astype(BF16)

    for h in range(MOBA_HEADS):
        cols = slice(h * MOBA_HEAD_DIM, (h + 1) * MOBA_HEAD_DIM)
        q = q_ref[:, cols].astype(F32)
        k = k_ref[:, cols].astype(F32)
        q = q * cos + pltpu.roll(q, half, 1) * sin
        k = k * cos + pltpu.roll(k, half, 1) * sin
        q_t = q.T

        kmean = kmean_scr[h]
        gate = _dot_split(kmean, q_t)
        gate = jnp.where(blk_row < blk, gate, NEG_INF)
        sel = blk_row == blk
        for _ in range(MOBA_TOPK):
            mx = jnp.max(gate, axis=0, keepdims=True)
            first_idx = jnp.min(jnp.where(gate == mx, blk_row_f, float(km_rows)),
                                axis=0, keepdims=True)
            hit = blk_row_f == first_idx
            sel = sel | (hit & (mx > NEG_INF))
            gate = jnp.where(hit, NEG_INF, gate)
        qat_ref[h * AUG_W:h * AUG_W + MOBA_HEAD_DIM, :] = (q_t * MOBA_EXP2_SCALE).astype(BF16)
        pen0 = h * AUG_W + MOBA_HEAD_DIM
        qat_ref[pen0:pen0 + km_rows, :] = jnp.where(sel, 0.0, MASK_PENALTY).astype(BF16)
        qat_ref[pen0 + km_rows:(h + 1) * AUG_W, :] = pen_pad
        ka_ref[:, h * AUG_W:h * AUG_W + MOBA_HEAD_DIM] = k.astype(BF16)
        ka_ref[:, h * AUG_W + MOBA_HEAD_DIM:(h + 1) * AUG_W] = onehot
        vt_ref[h, 0:MOBA_HEAD_DIM, :] = v_ref[:, cols].astype(F32).T.astype(BF16)
        vt_ref[h, MOBA_HEAD_DIM:VT_ROWS, :] = ones_rows

        k_mean_row = jnp.mean(k, axis=0, keepdims=True)
        kmean_scr[h] = jnp.where(mean_row == blk, k_mean_row, kmean)


def _moba_prep(main3d, cos_t, sin_t):
    b_sz, s_len, _ = main3d.shape
    n_blk = s_len // MOBA_BLOCK
    km_rows = -(-n_blk // BF16_ROWS) * BF16_ROWS
    assert km_rows <= MOBA_HEAD_DIM

    def tile_spec(cb):
        return pl.BlockSpec((None, MOBA_BLOCK, MOBA_WIDTH), lambda b, i, cb=cb: (b, i, cb))

    tab_spec = pl.BlockSpec((MOBA_BLOCK, MOBA_HEAD_DIM), lambda b, i: (i, 0))
    return pl.pallas_call(
        _moba_prep_kernel,
        grid=(b_sz, n_blk),
        in_specs=[tile_spec(1), tile_spec(2), tile_spec(3), tab_spec, tab_spec],
        out_specs=[
            pl.BlockSpec((None, MOBA_HEADS * AUG_W, MOBA_BLOCK), lambda b, i: (b, 0, i)),
            pl.BlockSpec((None, MOBA_BLOCK, MOBA_HEADS * AUG_W), lambda b, i: (b, i, 0)),
            pl.BlockSpec((None, MOBA_HEADS, None, VT_ROWS, MOBA_BLOCK),
                         lambda b, i: (b, 0, i, 0, 0)),
        ],
        out_shape=[
            jax.ShapeDtypeStruct((b_sz, MOBA_HEADS * AUG_W, s_len), BF16),
            jax.ShapeDtypeStruct((b_sz, s_len, MOBA_HEADS * AUG_W), BF16),
            jax.ShapeDtypeStruct((b_sz, MOBA_HEADS, n_blk, VT_ROWS, MOBA_BLOCK), BF16),
        ],
        scratch_shapes=[pltpu.VMEM((MOBA_HEADS, km_rows, MOBA_HEAD_DIM), F32)],
        compiler_params=pltpu.CompilerParams(
            dimension_semantics=("parallel", "arbitrary"), vmem_limit_bytes=VMEM_LIMIT),
        name="moba_prep",
    )(main3d, main3d, main3d, cos_t, sin_t)


def _moba_attn_kernel(qat_ref, ka_ref, vt_ref, out_ref, s_scr, p_scr, acc_scr, *, heads_per_step):
    n_q = qat_ref.shape[1]
    blocks_per_tile = n_q // MOBA_BLOCK
    first_blk = pl.program_id(2) * blocks_per_tile
    heads = range(heads_per_step)
    q_t = [qat_ref[h * AUG_W:(h + 1) * AUG_W, :] for h in heads]

    def scores(j):
        rows = pl.ds(pl.multiple_of(j * MOBA_BLOCK, MOBA_BLOCK), MOBA_BLOCK)
        return [jnp.dot(ka_ref[rows, h * AUG_W:(h + 1) * AUG_W], q_t[h],
                        preferred_element_type=F32) for h in heads]

    key = lax.broadcasted_iota(jnp.int32, (MOBA_BLOCK, n_q), 0)
    qry = lax.broadcasted_iota(jnp.int32, (MOBA_BLOCK, n_q), 1)
    m = None
    for r in range(blocks_per_tile):
        s = [jnp.where(key + r * MOBA_BLOCK <= qry, x, NEG_INF) for x in scores(first_blk + r)]
        if r == 0:
            m = [jnp.max(x, axis=0, keepdims=True) for x in s]
            for h in heads:
                acc_scr[h] = jnp.dot(vt_ref[h, first_blk], jnp.exp2(s[h] - m[h]).astype(BF16),
                                     preferred_element_type=F32)
        else:
            m_new = [jnp.maximum(m[h], jnp.max(s[h], axis=0, keepdims=True)) for h in heads]
            for h in heads:
                acc_scr[h] = (jnp.exp2(m[h] - m_new[h]) * acc_scr[h]
                              + jnp.dot(vt_ref[h, first_blk + r],
                                        jnp.exp2(s[h] - m_new[h]).astype(BF16),
                                        preferred_element_type=F32))
            m = m_new

    s_first = scores(0)
    for h in heads:
        s_scr[0, h] = s_first[h]
        p_scr[0, h] = jnp.zeros(p_scr.shape[2:], BF16)
    alpha = [jnp.ones_like(x) for x in m]

    def half_trip(j, cur, nxt, m, alpha):
        s_cur = [s_scr[cur, h] for h in heads]
        m_new = [jnp.maximum(m[h], jnp.max(s_cur[h], axis=0, keepdims=True)) for h in heads]
        p_new = [jnp.exp2(s_cur[h] - m_new[h]).astype(BF16) for h in heads]
        s_next = scores(jnp.minimum(j + 1, first_blk - 1))
        j_prev = jnp.maximum(j - 1, 0)
        for h in heads:
            acc_scr[h] = alpha[h] * acc_scr[h] + jnp.dot(vt_ref[h, j_prev], p_scr[cur, h],
                                                         preferred_element_type=F32)
        for h in heads:
            p_scr[nxt, h] = p_new[h]
            s_scr[nxt, h] = s_next[h]
        alpha = [jnp.exp2(m[h] - m_new[h]) for h in heads]
        return m_new, alpha

    def body(jj, carry):
        m, alpha = carry
        m, alpha = half_trip(2 * jj, 0, 1, m, alpha)
        return half_trip(2 * jj + 1, 1, 0, m, alpha)

    assert blocks_per_tile % 2 == 0
    m, alpha = lax.fori_loop(0, first_blk // 2, body, (m, alpha))
    j_last = jnp.maximum(first_blk - 1, 0)
    for h in heads:
        a = alpha[h] * acc_scr[h] + jnp.dot(vt_ref[h, j_last], p_scr[0, h],
                                            preferred_element_type=F32)
        o_t = a[:MOBA_HEAD_DIM] / a[MOBA_HEAD_DIM:MOBA_HEAD_DIM + 1]
        out_ref[:, h * MOBA_HEAD_DIM:(h + 1) * MOBA_HEAD_DIM] = o_t.T.astype(out_ref.dtype)


def _moba_attn(qat, ka, vt, heads_per_step, q_tile):
    b_sz, s_len, _ = ka.shape
    n_blk = s_len // MOBA_BLOCK
    hb = heads_per_step
    resident = pl.Buffered(1)
    return pl.pallas_call(
        functools.partial(_moba_attn_kernel, heads_per_step=hb),
        grid=(b_sz, MOBA_HEADS // hb, s_len // q_tile),
        in_specs=[
            pl.BlockSpec((None, hb * AUG_W, q_tile), lambda b, g, i: (b, g, i)),
            pl.BlockSpec((None, s_len, hb * AUG_W), lambda b, g, i: (b, 0, g),
                         pipeline_mode=resident),
            pl.BlockSpec((None, hb, n_blk, VT_ROWS, MOBA_BLOCK),
                         lambda b, g, i: (b, g, 0, 0, 0), pipeline_mode=resident),
        ],
        out_specs=pl.BlockSpec((None, q_tile, hb * MOBA_HEAD_DIM), lambda b, g, i: (b, i, g)),
        out_shape=jax.ShapeDtypeStruct((b_sz, s_len, MOBA_WIDTH), BF16),
        scratch_shapes=[pltpu.VMEM((2, hb, MOBA_BLOCK, q_tile), F32),
                        pltpu.VMEM((2, hb, MOBA_BLOCK, q_tile), BF16),
                        pltpu.VMEM((hb, VT_ROWS, q_tile), F32)],
        compiler_params=pltpu.CompilerParams(
            dimension_semantics=("parallel", "parallel", "arbitrary"),
            vmem_limit_bytes=VMEM_LIMIT),
        name="moba_attn",
    )(qat, ka, vt)


def _out_proj_kernel(x_ref, odn_ref, omb_ref, gdn_ref, gmb_ref, wdn_ref, wmb_ref, wo_ref, h_ref):
    y_dn = jnp.dot(odn_ref[...], wdn_ref[...], preferred_element_type=F32)
    y_mb = jnp.dot(omb_ref[...], wmb_ref[...], preferred_element_type=F32)
    merged = (_sigmoid(gdn_ref[...].astype(F32)) * y_dn
              + _sigmoid(gmb_ref[...].astype(F32)) * y_mb)
    h_ref[...] = x_ref[...] + jnp.dot(merged.astype(BF16), wo_ref[...], preferred_element_type=F32)


def _out_proj(x2d, o_dn, o_mb, main2d, w_dn, w_mb, w_o, tm):
    t, d = x2d.shape
    row_spec = pl.BlockSpec((tm, d), lambda i: (i, 0))
    w_spec = pl.BlockSpec((d, d), lambda i: (0, 0))
    return pl.pallas_call(
        _out_proj_kernel,
        grid=(t // tm,),
        in_specs=[row_spec, row_spec, row_spec,
                  pl.BlockSpec((tm, d), lambda i: (i, 4)),
                  pl.BlockSpec((tm, d), lambda i: (i, 5)),
                  w_spec, w_spec, w_spec],
        out_specs=row_spec,
        out_shape=jax.ShapeDtypeStruct((t, d), F32),
        compiler_params=pltpu.CompilerParams(
            dimension_semantics=("parallel",), vmem_limit_bytes=VMEM_LIMIT),
        name="out_proj",
    )(x2d, o_dn, o_mb, main2d, main2d, w_dn, w_mb, w_o)


def _conv_ffn_kernel(h_ref, nw_ref, wup_ref, cw_ref, cb_ref, wdown_ref, fw_ref, out_ref,
                     u_scr, *, tiles_per_seq, n_split):
    tm = h_ref.shape[0]
    seq_start = (pl.program_id(0) % tiles_per_seq) == 0
    h = h_ref[...]
    var = jnp.mean(h * h, axis=-1, keepdims=True)
    n = ((h * lax.rsqrt(var + NORM_EPS)) * nw_ref[...]).astype(BF16)

    width = D_FF // n_split
    base = SUBLANES - (FFN_CONV - 1)
    acc = jnp.zeros((tm, D_MODEL), F32)
    for part in range(n_split):
        halves = []
        for half in range(2):
            c0 = half * D_FF + part * width
            slot = 2 * part + half
            @pl.when(seq_start)
            def _(slot=slot):
                u_scr[slot, 0:SUBLANES, :] = jnp.zeros((SUBLANES, width), F32)

            @pl.when(jnp.logical_not(seq_start))
            def _(slot=slot):
                u_scr[slot, 0:SUBLANES, :] = u_scr[slot, tm:tm + SUBLANES, :]

            u_scr[slot, SUBLANES:SUBLANES + tm, :] = jnp.dot(
                n, wup_ref[:, c0:c0 + width], preferred_element_type=F32)
            y = (u_scr[slot, SUBLANES:SUBLANES + tm, :]
                 * cw_ref[FFN_CONV - 1:FFN_CONV, c0:c0 + width])
            for j in range(FFN_CONV - 1):
                y = y + u_scr[slot, base + j:base + j + tm, :] * cw_ref[j:j + 1, c0:c0 + width]
            halves.append(y + cb_ref[:, c0:c0 + width])
        act = (_silu(halves[0]) * halves[1]).astype(BF16)
        acc = acc + jnp.dot(act, wdown_ref[part * width:(part + 1) * width, :],
                            preferred_element_type=F32)
    h2 = h + acc
    var2 = jnp.mean(h2 * h2, axis=-1, keepdims=True)
    out_ref[...] = (h2 * lax.rsqrt(var2 + NORM_EPS)) * fw_ref[...]


def _conv_ffn(h2d, norm_w, w_up, conv_w, conv_b, w_down, final_w, tm, s_len, n_split):
    t, d = h2d.shape
    width = D_FF // n_split
    kern = functools.partial(_conv_ffn_kernel, tiles_per_seq=s_len // tm, n_split=n_split)

    def const_spec(shape):
        return pl.BlockSpec(shape, lambda i: (0, 0), pipeline_mode=pl.Buffered(1))

    return pl.pallas_call(
        kern,
        grid=(t // tm,),
        in_specs=[
            pl.BlockSpec((tm, d), lambda i: (i, 0)),
            const_spec((1, d)),
            const_spec((d, 2 * D_FF)),
            const_spec((FFN_CONV, 2 * D_FF)),
            const_spec((1, 2 * D_FF)),
            const_spec((D_FF, d)),
            const_spec((1, d)),
        ],
        out_specs=pl.BlockSpec((tm, d), lambda i: (i, 0)),
        out_shape=jax.ShapeDtypeStruct((t, d), F32),
        scratch_shapes=[pltpu.VMEM((2 * n_split, tm + SUBLANES, width), F32)],
        compiler_params=pltpu.CompilerParams(
            dimension_semantics=("arbitrary",), vmem_limit_bytes=VMEM_LIMIT),
        name="conv_ffn",
    )(h2d, norm_w, w_up, conv_w, conv_b, w_down, final_w)


def _rope_tables(s_len):
    half = MOBA_HEAD_DIM // 2
    inv_freq = np.power(ROPE_THETA, -np.arange(half, dtype=np.float64) / half)
    ang = np.arange(s_len, dtype=np.float64)[:, None] * inv_freq[None, :]
    cos, sin = np.cos(ang), np.sin(ang)
    return (jnp.asarray(np.concatenate([cos, cos], axis=-1), F32),
            jnp.asarray(np.concatenate([-sin, sin], axis=-1), F32))


def _pick_tile(n, want):
    t = min(n, want)
    while n % t:
        t //= 2
    return t


def _layer(x, attn_norm_w, w_in, dn_conv_w, dn_a_log, dn_dt_bias, dn_norm_w, w_dn_out,
           w_moba_out, w_o, ffn_norm_w, w_up, ffn_conv_w, ffn_conv_b, w_down, final_norm_w):
    b_sz, s_len, d = x.shape
    t = b_sz * s_len
    x2d = x.reshape(t, d)

    small_lo = 4 * DN_WIDTH
    small_hi = small_lo + 2 * DN_HEADS
    w_lo = w_in[:, :small_lo].astype(BF16)
    w_hi = w_in[:, small_hi:].astype(BF16)
    w_small_cols = w_in[:, small_lo:small_hi]
    w_small = jnp.pad(w_small_cols, ((0, 0), (0, SMALL_W - 2 * DN_HEADS))).astype(BF16)
    w_small_t = w_small_cols.T.astype(BF16)

    qkv_tok, rest, small, small_t = _in_proj(
        x2d, attn_norm_w.reshape(1, d), w_lo, w_hi, w_small, w_small_t, _pick_tile(t, 1024))
    qkv5d = qkv_tok.reshape(DN_QKV_BLOCKS, b_sz, s_len, DN_HEADS, DN_HEAD_DIM)
    main3d = rest.reshape(b_sz, s_len, rest.shape[1])
    small3d = small.reshape(b_sz, s_len, SMALL_W)

    pad_r = ((0, 0), (DN_HEADS, SMALL_W - 2 * DN_HEADS))
    alog_r = jnp.pad(dn_a_log.reshape(1, DN_HEADS), pad_r)
    dtb_r = jnp.pad(dn_dt_bias.reshape(1, DN_HEADS), pad_r)
    alog_c = jnp.pad(dn_a_log.reshape(DN_HEADS, 1), ((DN_HEADS, 0), (0, 0)))
    dtb_c = jnp.pad(dn_dt_bias.reshape(DN_HEADS, 1), ((DN_HEADS, 0), (0, 0)))
    o_dn = _deltanet(qkv5d, main3d, small3d, small_t, dn_conv_w, alog_r, dtb_r, alog_c, dtb_c,
                     dn_norm_w.reshape(1, DN_HEAD_DIM), _pick_tile(s_len, 512))

    cos_t, sin_t = _rope_tables(s_len)
    qat, ka, vt = _moba_prep(main3d, cos_t, sin_t)
    o_mb = _moba_attn(qat, ka, vt, 4, _pick_tile(s_len, 2 * MOBA_BLOCK))

    h = _out_proj(x2d, o_dn.reshape(t, DN_WIDTH), o_mb.reshape(t, MOBA_WIDTH), rest,
                  w_dn_out.astype(BF16), w_moba_out.astype(BF16), w_o.astype(BF16),
                  _pick_tile(t, 1024))

    tm_ffn = _pick_tile(s_len, 512)
    out = _conv_ffn(h, ffn_norm_w.reshape(1, d), w_up.astype(BF16), ffn_conv_w,
                    ffn_conv_b.reshape(1, 2 * D_FF), w_down.astype(BF16),
                    final_norm_w.reshape(1, d), tm_ffn, s_len, 1)
    return out.reshape(b_sz, s_len, d)


def kernel(x, attn_norm_w, w_in, dn_conv_w, dn_A_log, dn_dt_bias, dn_norm_w, w_dn_out, w_moba_out,
           w_o, ffn_norm_w, w_up, ffn_conv_w, ffn_conv_b, w_down, final_norm_w):
    depth = w_in.shape[0]
    assert depth == 1, "the final RMSNorm is fused into the layer's conv_ffn call"
    return _layer(x, attn_norm_w[0], w_in[0], dn_conv_w[0], dn_A_log[0], dn_dt_bias[0],
                  dn_norm_w[0], w_dn_out[0], w_moba_out[0], w_o[0], ffn_norm_w[0], w_up[0],
                  ffn_conv_w[0], ffn_conv_b[0], w_down[0], final_norm_w)
```
